```python
import math
import jax, jax.numpy as jnp
from jax import lax
import numpy as np

D_MODEL = 4096
BATCH = 16
SEQ = 2048
DEPTH = 1

MIX_WIDTH = D_MODEL
POOL_WIDTH = MIX_WIDTH // 2
SSM_WIDTH = MIX_WIDTH - POOL_WIDTH
POOL_WINDOWS = (2, 4, 8, 16)
POOL_GROUPS = len(POOL_WINDOWS)
POOL_GROUP_WIDTH = POOL_WIDTH // POOL_GROUPS
SSM_GROUP_CH = 16
SSM_GROUPS = SSM_WIDTH // SSM_GROUP_CH
SSM_STATE = 64
D_FF = ((8 * D_MODEL // 3 + 255) // 256) * 256
DT_MIN = 1e-3
DT_MAX = 1e-1
NORM_EPS = 1e-6

kernel_name = "macaron_pool_s5_hybrid_block"


def rms_norm(x, g):
    xf = x.astype(jnp.float32)
    y = xf * lax.rsqrt(jnp.mean(xf * xf, axis=-1, keepdims=True) + NORM_EPS)
    return (y * g.astype(jnp.float32)).astype(x.dtype)


def swiglu_ffn(h, w_gate, w_up, w_down):
    return (jax.nn.silu(h @ w_gate) * (h @ w_up)) @ w_down


def causal_multiscale_pool(z, w_pool, pool_scale):
    b, s, _ = z.shape
    zf = z.astype(jnp.float32).reshape(b, s, POOL_GROUPS, POOL_GROUP_WIDTH)
    cs = jnp.cumsum(zf, axis=1)
    t = jnp.arange(s)
    diffs = []
    for g, w in enumerate(POOL_WINDOWS):
        c = cs[:, :, g]
        lagged = jnp.pad(c[:, : s - w], ((0, 0), (w, 0), (0, 0)))
        cnt = jnp.minimum(t + 1, w).astype(jnp.float32)[None, :, None]
        diffs.append((c - lagged) / cnt - zf[:, :, g])
    d = jnp.stack(diffs, axis=2).astype(z.dtype)
    out = jnp.einsum("bsgc,gcd->bsgd", d, w_pool).reshape(b, s, POOL_WIDTH)
    return out * pool_scale


def _ssm_combine(e1, e2):
    a1, b1 = e1
    a2, b2 = e2
    return a1 * a2, a2 * b1 + b2


def s5_mixer(z, lam_re, lam_im, log_dt, b_re, b_im, c_re, c_im, d_skip, w_glu, b_glu):
    b, s, _ = z.shape
    u = z.astype(jnp.float32).reshape(b, s, SSM_GROUPS, SSM_GROUP_CH)
    lam = lax.complex(lam_re.astype(jnp.float32), lam_im.astype(jnp.float32))
    dt = jnp.exp(log_dt.astype(jnp.float32))[:, None]
    lam_bar = jnp.exp(lam * dt)
    b_mat = lax.complex(b_re.astype(jnp.float32), b_im.astype(jnp.float32))
    b_bar = ((lam_bar - 1.0) / lam)[:, :, None] * b_mat
    c_mat = lax.complex(c_re.astype(jnp.float32), c_im.astype(jnp.float32))
    bu = jnp.einsum("bsgh,gph->bsgp", u.astype(jnp.complex64), b_bar)
    a = jnp.broadcast_to(lam_bar, bu.shape)
    _, states = lax.associative_scan(_ssm_combine, (a, bu), axis=1)
    y = jnp.einsum("ghp,bsgp->bsgh", c_mat, states).real
    y = y + d_skip.astype(jnp.float32).reshape(SSM_GROUPS, SSM_GROUP_CH) * u
    y = jax.nn.gelu(y.reshape(b, s, SSM_WIDTH)).astype(z.dtype)
    return y * jax.nn.sigmoid(y @ w_glu + b_glu)


def _fwd_setup_inputs(seed: int = 0) -> dict:
    key = jax.random.key(seed)
    ks = jax.random.split(key, 32)
    f32 = jnp.float32
    nrm = lambda k, shape, scale: jax.random.normal(k, shape, f32) * scale
    gain = lambda k, n: 1.0 + 0.02 * jax.random.normal(k, (n,), f32)
    n_idx = jnp.arange(SSM_STATE, dtype=f32)[None, :]
    return {
        "x": jax.random.normal(ks[0], (BATCH, SEQ, D_MODEL), f32),
        "ffn1_norm": gain(ks[1], D_MODEL),
        "ffn1_gate": nrm(ks[2], (D_MODEL, D_FF), D_MODEL ** -0.5),
        "ffn1_up": nrm(ks[3], (D_MODEL, D_FF), D_MODEL ** -0.5),
        "ffn1_down": nrm(ks[4], (D_FF, D_MODEL), D_FF ** -0.5),
        "mix_norm": gain(ks[5], D_MODEL),
        "w_in": nrm(ks[6], (D_MODEL, MIX_WIDTH), D_MODEL ** -0.5),
        "w_pool": nrm(ks[7], (POOL_GROUPS, POOL_GROUP_WIDTH, POOL_GROUP_WIDTH), POOL_GROUP_WIDTH ** -0.5),
        "pool_scale": gain(ks[8], POOL_WIDTH),
        "lam_re": -0.5 + 0.01 * jax.random.normal(ks[9], (SSM_GROUPS, SSM_STATE), f32),
        "lam_im": math.pi * n_idx + 0.01 * jax.random.normal(ks[10], (SSM_GROUPS, SSM_STATE), f32),
        "log_dt": jax.random.uniform(ks[11], (SSM_GROUPS,), f32, math.log(DT_MIN), math.log(DT_MAX)),
        "b_re": nrm(ks[12], (SSM_GROUPS, SSM_STATE, SSM_GROUP_CH), (2.0 * SSM_GROUP_CH) ** -0.5),
        "b_im": nrm(ks[13], (SSM_GROUPS, SSM_STATE, SSM_GROUP_CH), (2.0 * SSM_GROUP_CH) ** -0.5),
        "c_re": nrm(ks[14], (SSM_GROUPS, SSM_GROUP_CH, SSM_STATE), (2.0 * SSM_STATE) ** -0.5),
        "c_im": nrm(ks[15], (SSM_GROUPS, SSM_GROUP_CH, SSM_STATE), (2.0 * SSM_STATE) ** -0.5),
        "d_skip": jax.random.normal(ks[16], (SSM_WIDTH,), f32),
        "w_glu": nrm(ks[17], (SSM_WIDTH, SSM_WIDTH), SSM_WIDTH ** -0.5),
        "b_glu": nrm(ks[18], (SSM_WIDTH,), 0.01),
        "pool_out_norm": gain(ks[19], POOL_WIDTH),
        "ssm_out_norm": gain(ks[20], SSM_WIDTH),
        "w_out": nrm(ks[21], (MIX_WIDTH, D_MODEL), MIX_WIDTH ** -0.5),
        "ffn2_norm": gain(ks[22], D_MODEL),
        "ffn2_gate": nrm(ks[23], (D_MODEL, D_FF), D_MODEL ** -0.5),
        "ffn2_up": nrm(ks[24], (D_MODEL, D_FF), D_MODEL ** -0.5),
        "ffn2_down": nrm(ks[25], (D_FF, D_MODEL), D_FF ** -0.5),
        "final_norm": gain(ks[26], D_MODEL),
    }


def _fwd_reference(x, ffn1_norm, ffn1_gate, ffn1_up, ffn1_down, mix_norm, w_in, w_pool, pool_scale,
              lam_re, lam_im, log_dt, b_re, b_im, c_re, c_im, d_skip, w_glu, b_glu,
              pool_out_norm, ssm_out_norm, w_out, ffn2_norm, ffn2_gate, ffn2_up, ffn2_down,
              final_norm):
    h = x
    for _ in range(DEPTH):
        h = h + 0.5 * swiglu_ffn(rms_norm(h, ffn1_norm), ffn1_gate, ffn1_up, ffn1_down)
        z = rms_norm(h, mix_norm) @ w_in
        z_pool = z[..., :POOL_WIDTH]
        z_ssm = z[..., POOL_WIDTH:]
        y_pool = causal_multiscale_pool(z_pool, w_pool, pool_scale)
        y_ssm = s5_mixer(z_ssm, lam_re, lam_im, log_dt, b_re, b_im, c_re, c_im,
                         d_skip, w_glu, b_glu)
        merged = jnp.concatenate(
            [rms_norm(y_pool, pool_out_norm), rms_norm(y_ssm, ssm_out_norm)], axis=-1)
        h = h + merged @ w_out
        h = h + 0.5 * swiglu_ffn(rms_norm(h, ffn2_norm), ffn2_gate, ffn2_up, ffn2_down)
    return rms_norm(h, final_norm)


import jax as _jax
import jax.numpy as _jnp

TWIN_FORMAT = 'train_step'
FWD_PARAMS = ['x', 'ffn1_norm', 'ffn1_gate', 'ffn1_up', 'ffn1_down', 'mix_norm', 'w_in', 'w_pool', 'pool_scale', 'lam_re', 'lam_im', 'log_dt', 'b_re', 'b_im', 'c_re', 'c_im', 'd_skip', 'w_glu', 'b_glu', 'pool_out_norm', 'ssm_out_norm', 'w_out', 'ffn2_norm', 'ffn2_gate', 'ffn2_up', 'ffn2_down', 'final_norm']
TWIN_WEIGHTS = ['ffn1_norm', 'ffn1_gate', 'ffn1_up', 'ffn1_down', 'mix_norm', 'w_in', 'w_pool', 'pool_scale', 'lam_re', 'lam_im', 'log_dt', 'b_re', 'b_im', 'c_re', 'c_im', 'd_skip', 'w_glu', 'b_glu', 'pool_out_norm', 'ssm_out_norm', 'w_out', 'ffn2_norm', 'ffn2_gate', 'ffn2_up', 'ffn2_down', 'final_norm']
TWIN_DIFF_INPUT = 'x'
TWIN_INPUTS = ['x', 'ffn1_norm', 'ffn1_gate', 'ffn1_up', 'ffn1_down', 'mix_norm', 'w_in', 'w_pool', 'pool_scale', 'lam_re', 'lam_im', 'log_dt', 'b_re', 'b_im', 'c_re', 'c_im', 'd_skip', 'w_glu', 'b_glu', 'pool_out_norm', 'ssm_out_norm', 'w_out', 'ffn2_norm', 'ffn2_gate', 'ffn2_up', 'ffn2_down', 'final_norm', 'loss_target', 'm_ffn1_norm', 'm_ffn1_gate', 'm_ffn1_up', 'm_ffn1_down', 'm_mix_norm', 'm_w_in', 'm_w_pool', 'm_pool_scale', 'm_lam_re', 'm_lam_im', 'm_log_dt', 'm_b_re', 'm_b_im', 'm_c_re', 'm_c_im', 'm_d_skip', 'm_w_glu', 'm_b_glu', 'm_pool_out_norm', 'm_ssm_out_norm', 'm_w_out', 'm_ffn2_norm', 'm_ffn2_gate', 'm_ffn2_up', 'm_ffn2_down', 'm_final_norm', 'v_ffn1_norm', 'v_ffn1_gate', 'v_ffn1_up', 'v_ffn1_down', 'v_mix_norm', 'v_w_in', 'v_w_pool', 'v_pool_scale', 'v_lam_re', 'v_lam_im', 'v_log_dt', 'v_b_re', 'v_b_im', 'v_c_re', 'v_c_im', 'v_d_skip', 'v_w_glu', 'v_b_glu', 'v_pool_out_norm', 'v_ssm_out_norm', 'v_w_out', 'v_ffn2_norm', 'v_ffn2_gate', 'v_ffn2_up', 'v_ffn2_down', 'v_final_norm']
TWIN_OUTPUTS = ['loss', 'grad_x', 'grad_ffn1_norm', 'grad_ffn1_gate', 'grad_ffn1_up', 'grad_ffn1_down', 'grad_mix_norm', 'grad_w_in', 'grad_w_pool', 'grad_pool_scale', 'grad_lam_re', 'grad_lam_im', 'grad_log_dt', 'grad_b_re', 'grad_b_im', 'grad_c_re', 'grad_c_im', 'grad_d_skip', 'grad_w_glu', 'grad_b_glu', 'grad_pool_out_norm', 'grad_ssm_out_norm', 'grad_w_out', 'grad_ffn2_norm', 'grad_ffn2_gate', 'grad_ffn2_up', 'grad_ffn2_down', 'grad_final_norm', 'delta_ffn1_norm', 'delta_ffn1_gate', 'delta_ffn1_up', 'delta_ffn1_down', 'delta_mix_norm', 'delta_w_in', 'delta_w_pool', 'delta_pool_scale', 'delta_lam_re', 'delta_lam_im', 'delta_log_dt', 'delta_b_re', 'delta_b_im', 'delta_c_re', 'delta_c_im', 'delta_d_skip', 'delta_w_glu', 'delta_b_glu', 'delta_pool_out_norm', 'delta_ssm_out_norm', 'delta_w_out', 'delta_ffn2_norm', 'delta_ffn2_gate', 'delta_ffn2_up', 'delta_ffn2_down', 'delta_final_norm', 'new_m_ffn1_norm', 'new_m_ffn1_gate', 'new_m_ffn1_up', 'new_m_ffn1_down', 'new_m_mix_norm', 'new_m_w_in', 'new_m_w_pool', 'new_m_pool_scale', 'new_m_lam_re', 'new_m_lam_im', 'new_m_log_dt', 'new_m_b_re', 'new_m_b_im', 'new_m_c_re', 'new_m_c_im', 'new_m_d_skip', 'new_m_w_glu', 'new_m_b_glu', 'new_m_pool_out_norm', 'new_m_ssm_out_norm', 'new_m_w_out', 'new_m_ffn2_norm', 'new_m_ffn2_gate', 'new_m_ffn2_up', 'new_m_ffn2_down', 'new_m_final_norm', 'new_v_ffn1_norm', 'new_v_ffn1_gate', 'new_v_ffn1_up', 'new_v_ffn1_down', 'new_v_mix_norm', 'new_v_w_in', 'new_v_w_pool', 'new_v_pool_scale', 'new_v_lam_re', 'new_v_lam_im', 'new_v_log_dt', 'new_v_b_re', 'new_v_b_im', 'new_v_c_re', 'new_v_c_im', 'new_v_d_skip', 'new_v_w_glu', 'new_v_b_glu', 'new_v_pool_out_norm', 'new_v_ssm_out_norm', 'new_v_w_out', 'new_v_ffn2_norm', 'new_v_ffn2_gate', 'new_v_ffn2_up', 'new_v_ffn2_down', 'new_v_final_norm']
TWIN_LEAF_KINDS = {'loss': 'loss', 'grad_x': 'grad_x', 'grad_ffn1_norm': 'grad_w', 'grad_ffn1_gate': 'grad_w', 'grad_ffn1_up': 'grad_w', 'grad_ffn1_down': 'grad_w', 'grad_mix_norm': 'grad_w', 'grad_w_in': 'grad_w', 'grad_w_pool': 'grad_w', 'grad_pool_scale': 'grad_w', 'grad_lam_re': 'grad_w', 'grad_lam_im': 'grad_w', 'grad_log_dt': 'grad_w', 'grad_b_re': 'grad_w', 'grad_b_im': 'grad_w', 'grad_c_re': 'grad_w', 'grad_c_im': 'grad_w', 'grad_d_skip': 'grad_w', 'grad_w_glu': 'grad_w', 'grad_b_glu': 'grad_w', 'grad_pool_out_norm': 'grad_w', 'grad_ssm_out_norm': 'grad_w', 'grad_w_out': 'grad_w', 'grad_ffn2_norm': 'grad_w', 'grad_ffn2_gate': 'grad_w', 'grad_ffn2_up': 'grad_w', 'grad_ffn2_down': 'grad_w', 'grad_final_norm': 'grad_w', 'delta_ffn1_norm': 'delta_w', 'delta_ffn1_gate': 'delta_w', 'delta_ffn1_up': 'delta_w', 'delta_ffn1_down': 'delta_w', 'delta_mix_norm': 'delta_w', 'delta_w_in': 'delta_w', 'delta_w_pool': 'delta_w', 'delta_pool_scale': 'delta_w', 'delta_lam_re': 'delta_w', 'delta_lam_im': 'delta_w', 'delta_log_dt': 'delta_w', 'delta_b_re': 'delta_w', 'delta_b_im': 'delta_w', 'delta_c_re': 'delta_w', 'delta_c_im': 'delta_w', 'delta_d_skip': 'delta_w', 'delta_w_glu': 'delta_w', 'delta_b_glu': 'delta_w', 'delta_pool_out_norm': 'delta_w', 'delta_ssm_out_norm': 'delta_w', 'delta_w_out': 'delta_w', 'delta_ffn2_norm': 'delta_w', 'delta_ffn2_gate': 'delta_w', 'delta_ffn2_up': 'delta_w', 'delta_ffn2_down': 'delta_w', 'delta_final_norm': 'delta_w', 'new_m_ffn1_norm': 'new_m', 'new_m_ffn1_gate': 'new_m', 'new_m_ffn1_up': 'new_m', 'new_m_ffn1_down': 'new_m', 'new_m_mix_norm': 'new_m', 'new_m_w_in': 'new_m', 'new_m_w_pool': 'new_m', 'new_m_pool_scale': 'new_m', 'new_m_lam_re': 'new_m', 'new_m_lam_im': 'new_m', 'new_m_log_dt': 'new_m', 'new_m_b_re': 'new_m', 'new_m_b_im': 'new_m', 'new_m_c_re': 'new_m', 'new_m_c_im': 'new_m', 'new_m_d_skip': 'new_m', 'new_m_w_glu': 'new_m', 'new_m_b_glu': 'new_m', 'new_m_pool_out_norm': 'new_m', 'new_m_ssm_out_norm': 'new_m', 'new_m_w_out': 'new_m', 'new_m_ffn2_norm': 'new_m', 'new_m_ffn2_gate': 'new_m', 'new_m_ffn2_up': 'new_m', 'new_m_ffn2_down': 'new_m', 'new_m_final_norm': 'new_m', 'new_v_ffn1_norm': 'new_v', 'new_v_ffn1_gate': 'new_v', 'new_v_ffn1_up': 'new_v', 'new_v_ffn1_down': 'new_v', 'new_v_mix_norm': 'new_v', 'new_v_w_in': 'new_v', 'new_v_w_pool': 'new_v', 'new_v_pool_scale': 'new_v', 'new_v_lam_re': 'new_v', 'new_v_lam_im': 'new_v', 'new_v_log_dt': 'new_v', 'new_v_b_re': 'new_v', 'new_v_b_im': 'new_v', 'new_v_c_re': 'new_v', 'new_v_c_im': 'new_v', 'new_v_d_skip': 'new_v', 'new_v_w_glu': 'new_v', 'new_v_b_glu': 'new_v', 'new_v_pool_out_norm': 'new_v', 'new_v_ssm_out_norm': 'new_v', 'new_v_w_out': 'new_v', 'new_v_ffn2_norm': 'new_v', 'new_v_ffn2_gate': 'new_v', 'new_v_ffn2_up': 'new_v', 'new_v_ffn2_down': 'new_v', 'new_v_final_norm': 'new_v'}


def _forward(args):
    return _fwd_reference(*[args[k] for k in FWD_PARAMS])


def _output_shape():
    def fwd():
        inp = _fwd_setup_inputs(0)
        return _fwd_reference(*[inp[k] for k in FWD_PARAMS])
    out = _jax.eval_shape(fwd)
    return out.shape, out.dtype

N_MICROBATCH = 1
ADAM_LR = 0.001
ADAM_B1 = 0.9
ADAM_B2 = 0.999
ADAM_EPS = 1e-08
ADAM_WD = 0.01
ADAM_STEP = 10
PER_EXAMPLE_BATCH_AXIS = {'x': 0, 'loss_target': 0}
SHARED_INPUTS = []
_WEIGHT_DTYPES = {'ffn1_norm': _jnp.float32, 'ffn1_gate': _jnp.float32, 'ffn1_up': _jnp.float32, 'ffn1_down': _jnp.float32, 'mix_norm': _jnp.float32, 'w_in': _jnp.float32, 'w_pool': _jnp.float32, 'pool_scale': _jnp.float32, 'lam_re': _jnp.float32, 'lam_im': _jnp.float32, 'log_dt': _jnp.float32, 'b_re': _jnp.float32, 'b_im': _jnp.float32, 'c_re': _jnp.float32, 'c_im': _jnp.float32, 'd_skip': _jnp.float32, 'w_glu': _jnp.float32, 'b_glu': _jnp.float32, 'pool_out_norm': _jnp.float32, 'ssm_out_norm': _jnp.float32, 'w_out': _jnp.float32, 'ffn2_norm': _jnp.float32, 'ffn2_gate': _jnp.float32, 'ffn2_up': _jnp.float32, 'ffn2_down': _jnp.float32, 'final_norm': _jnp.float32}
MOMENT_SCALE = {'ffn1_norm': 1.900573e-02, 'ffn1_gate': 8.293669e-03, 'ffn1_up': 8.036058e-03, 'ffn1_down': 1.318229e-02, 'mix_norm': 3.297457e-02, 'w_in': 3.256982e-02, 'w_pool': 3.140403e-02, 'pool_scale': 3.108813e-02, 'lam_re': 1.639201e-03, 'lam_im': 1.536569e-03, 'log_dt': 8.906440e-01, 'b_re': 1.075960e-03, 'b_im': 1.079045e-03, 'c_re': 2.164730e-03, 'c_im': 2.165635e-03, 'd_skip': 3.998829e-02, 'w_glu': 9.518673e-03, 'b_glu': 1.555813e-02, 'pool_out_norm': 3.115774e-02, 'ssm_out_norm': 3.703471e-02, 'w_out': 3.310920e-02, 'ffn2_norm': 1.316566e-02, 'ffn2_gate': 5.633653e-03, 'ffn2_up': 5.480576e-03, 'ffn2_down': 8.969563e-03, 'final_norm': 8.018514e+00}


def _to_microbatches(a, axis):
    t = _jnp.moveaxis(a, axis, 0)
    t = t.reshape((N_MICROBATCH, t.shape[0] // N_MICROBATCH) + t.shape[1:])
    return _jnp.moveaxis(t, 1, axis + 1)


def setup_inputs(seed: int = 0) -> dict:
    inp = _fwd_setup_inputs(seed)
    key = _jax.random.fold_in(_jax.random.key(seed), 7919)
    shape, _ = _output_shape()
    out = dict(inp)
    out["loss_target"] = _jax.random.normal(_jax.random.fold_in(key, 0), shape, _jnp.float32)
    for i, name in enumerate(TWIN_WEIGHTS):
        w = inp[name].astype(_jnp.float32)
        if MOMENT_SCALE is None:
            s = _jnp.sqrt(_jnp.mean(_jnp.square(w)) + 1e-30)
        else:
            s = MOMENT_SCALE[name]
        km, kv = _jax.random.split(_jax.random.fold_in(key, i + 1))
        out[name] = w
        out["m_" + name] = s * _jax.random.normal(km, w.shape, _jnp.float32)
        out["v_" + name] = (s * s) * _jax.random.uniform(kv, w.shape, _jnp.float32, 0.5, 1.5)
    if N_MICROBATCH > 1:
        for name, axis in PER_EXAMPLE_BATCH_AXIS.items():
            out[name] = _to_microbatches(out[name], axis)
    return {'x': out['x'], 'ffn1_norm': out['ffn1_norm'], 'ffn1_gate': out['ffn1_gate'], 'ffn1_up': out['ffn1_up'], 'ffn1_down': out['ffn1_down'], 'mix_norm': out['mix_norm'], 'w_in': out['w_in'], 'w_pool': out['w_pool'], 'pool_scale': out['pool_scale'], 'lam_re': out['lam_re'], 'lam_im': out['lam_im'], 'log_dt': out['log_dt'], 'b_re': out['b_re'], 'b_im': out['b_im'], 'c_re': out['c_re'], 'c_im': out['c_im'], 'd_skip': out['d_skip'], 'w_glu': out['w_glu'], 'b_glu': out['b_glu'], 'pool_out_norm': out['pool_out_norm'], 'ssm_out_norm': out['ssm_out_norm'], 'w_out': out['w_out'], 'ffn2_norm': out['ffn2_norm'], 'ffn2_gate': out['ffn2_gate'], 'ffn2_up': out['ffn2_up'], 'ffn2_down': out['ffn2_down'], 'final_norm': out['final_norm'], 'loss_target': out['loss_target'], 'm_ffn1_norm': out['m_ffn1_norm'], 'm_ffn1_gate': out['m_ffn1_gate'], 'm_ffn1_up': out['m_ffn1_up'], 'm_ffn1_down': out['m_ffn1_down'], 'm_mix_norm': out['m_mix_norm'], 'm_w_in': out['m_w_in'], 'm_w_pool': out['m_w_pool'], 'm_pool_scale': out['m_pool_scale'], 'm_lam_re': out['m_lam_re'], 'm_lam_im': out['m_lam_im'], 'm_log_dt': out['m_log_dt'], 'm_b_re': out['m_b_re'], 'm_b_im': out['m_b_im'], 'm_c_re': out['m_c_re'], 'm_c_im': out['m_c_im'], 'm_d_skip': out['m_d_skip'], 'm_w_glu': out['m_w_glu'], 'm_b_glu': out['m_b_glu'], 'm_pool_out_norm': out['m_pool_out_norm'], 'm_ssm_out_norm': out['m_ssm_out_norm'], 'm_w_out': out['m_w_out'], 'm_ffn2_norm': out['m_ffn2_norm'], 'm_ffn2_gate': out['m_ffn2_gate'], 'm_ffn2_up': out['m_ffn2_up'], 'm_ffn2_down': out['m_ffn2_down'], 'm_final_norm': out['m_final_norm'], 'v_ffn1_norm': out['v_ffn1_norm'], 'v_ffn1_gate': out['v_ffn1_gate'], 'v_ffn1_up': out['v_ffn1_up'], 'v_ffn1_down': out['v_ffn1_down'], 'v_mix_norm': out['v_mix_norm'], 'v_w_in': out['v_w_in'], 'v_w_pool': out['v_w_pool'], 'v_pool_scale': out['v_pool_scale'], 'v_lam_re': out['v_lam_re'], 'v_lam_im': out['v_lam_im'], 'v_log_dt': out['v_log_dt'], 'v_b_re': out['v_b_re'], 'v_b_im': out['v_b_im'], 'v_c_re': out['v_c_re'], 'v_c_im': out['v_c_im'], 'v_d_skip': out['v_d_skip'], 'v_w_glu': out['v_w_glu'], 'v_b_glu': out['v_b_glu'], 'v_pool_out_norm': out['v_pool_out_norm'], 'v_ssm_out_norm': out['v_ssm_out_norm'], 'v_w_out': out['v_w_out'], 'v_ffn2_norm': out['v_ffn2_norm'], 'v_ffn2_gate': out['v_ffn2_gate'], 'v_ffn2_up': out['v_ffn2_up'], 'v_ffn2_down': out['v_ffn2_down'], 'v_final_norm': out['v_final_norm']}


def _loss(weights, diff, rest, loss_target):
    with _jax.named_scope("forward"):
        args = {**rest, TWIN_DIFF_INPUT: diff, **{k: w.astype(_WEIGHT_DTYPES[k]) for k, w in weights.items()}}
        y = _forward(args)
    with _jax.named_scope("loss_head"):
        err = _jnp.square(y.astype(_jnp.float32) - loss_target)
        return 0.5 * _jnp.sum(_jnp.mean(err, axis=-1)) if err.ndim else 0.5 * err


def _adamw(w, g, m, v):
    m = ADAM_B1 * m + (1.0 - ADAM_B1) * g
    v = ADAM_B2 * v + (1.0 - ADAM_B2) * _jnp.square(g)
    m_hat = m / (1.0 - ADAM_B1 ** ADAM_STEP)
    v_hat = v / (1.0 - ADAM_B2 ** ADAM_STEP)
    delta = -ADAM_LR * (m_hat / (_jnp.sqrt(v_hat) + ADAM_EPS) + ADAM_WD * w)
    return delta, m, v


def reference(x, ffn1_norm, ffn1_gate, ffn1_up, ffn1_down, mix_norm, w_in, w_pool, pool_scale, lam_re, lam_im, log_dt, b_re, b_im, c_re, c_im, d_skip, w_glu, b_glu, pool_out_norm, ssm_out_norm, w_out, ffn2_norm, ffn2_gate, ffn2_up, ffn2_down, final_norm, loss_target, m_ffn1_norm, m_ffn1_gate, m_ffn1_up, m_ffn1_down, m_mix_norm, m_w_in, m_w_pool, m_pool_scale, m_lam_re, m_lam_im, m_log_dt, m_b_re, m_b_im, m_c_re, m_c_im, m_d_skip, m_w_glu, m_b_glu, m_pool_out_norm, m_ssm_out_norm, m_w_out, m_ffn2_norm, m_ffn2_gate, m_ffn2_up, m_ffn2_down, m_final_norm, v_ffn1_norm, v_ffn1_gate, v_ffn1_up, v_ffn1_down, v_mix_norm, v_w_in, v_w_pool, v_pool_scale, v_lam_re, v_lam_im, v_log_dt, v_b_re, v_b_im, v_c_re, v_c_im, v_d_skip, v_w_glu, v_b_glu, v_pool_out_norm, v_ssm_out_norm, v_w_out, v_ffn2_norm, v_ffn2_gate, v_ffn2_up, v_ffn2_down, v_final_norm):
    given = dict(x=x, ffn1_norm=ffn1_norm, ffn1_gate=ffn1_gate, ffn1_up=ffn1_up, ffn1_down=ffn1_down, mix_norm=mix_norm, w_in=w_in, w_pool=w_pool, pool_scale=pool_scale, lam_re=lam_re, lam_im=lam_im, log_dt=log_dt, b_re=b_re, b_im=b_im, c_re=c_re, c_im=c_im, d_skip=d_skip, w_glu=w_glu, b_glu=b_glu, pool_out_norm=pool_out_norm, ssm_out_norm=ssm_out_norm, w_out=w_out, ffn2_norm=ffn2_norm, ffn2_gate=ffn2_gate, ffn2_up=ffn2_up, ffn2_down=ffn2_down, final_norm=final_norm, loss_target=loss_target, m_ffn1_norm=m_ffn1_norm, m_ffn1_gate=m_ffn1_gate, m_ffn1_up=m_ffn1_up, m_ffn1_down=m_ffn1_down, m_mix_norm=m_mix_norm, m_w_in=m_w_in, m_w_pool=m_w_pool, m_pool_scale=m_pool_scale, m_lam_re=m_lam_re, m_lam_im=m_lam_im, m_log_dt=m_log_dt, m_b_re=m_b_re, m_b_im=m_b_im, m_c_re=m_c_re, m_c_im=m_c_im, m_d_skip=m_d_skip, m_w_glu=m_w_glu, m_b_glu=m_b_glu, m_pool_out_norm=m_pool_out_norm, m_ssm_out_norm=m_ssm_out_norm, m_w_out=m_w_out, m_ffn2_norm=m_ffn2_norm, m_ffn2_gate=m_ffn2_gate, m_ffn2_up=m_ffn2_up, m_ffn2_down=m_ffn2_down, m_final_norm=m_final_norm, v_ffn1_norm=v_ffn1_norm, v_ffn1_gate=v_ffn1_gate, v_ffn1_up=v_ffn1_up, v_ffn1_down=v_ffn1_down, v_mix_norm=v_mix_norm, v_w_in=v_w_in, v_w_pool=v_w_pool, v_pool_scale=v_pool_scale, v_lam_re=v_lam_re, v_lam_im=v_lam_im, v_log_dt=v_log_dt, v_b_re=v_b_re, v_b_im=v_b_im, v_c_re=v_c_re, v_c_im=v_c_im, v_d_skip=v_d_skip, v_w_glu=v_w_glu, v_b_glu=v_b_glu, v_pool_out_norm=v_pool_out_norm, v_ssm_out_norm=v_ssm_out_norm, v_w_out=v_w_out, v_ffn2_norm=v_ffn2_norm, v_ffn2_gate=v_ffn2_gate, v_ffn2_up=v_ffn2_up, v_ffn2_down=v_ffn2_down, v_final_norm=v_final_norm)
    weights = {n: given[n] for n in TWIN_WEIGHTS}
    shared = {n: given[n] for n in SHARED_INPUTS}
    per_example = {n: given[n] for n in ['x']}
    grad_fn = _jax.value_and_grad(_loss, argnums=(0, 1))

    def one_microbatch(ex, loss_target):
        ex = dict(ex)
        diff = ex.pop(TWIN_DIFF_INPUT)
        return grad_fn(weights, diff, {**shared, **ex}, loss_target)

    if N_MICROBATCH == 1:
        loss, (grad_w, grad_x) = one_microbatch(per_example, given["loss_target"])
    else:
        def body(carry, xs):
            loss_sum, grad_sum = carry
            l_k, (gw_k, gx_k) = one_microbatch(xs[0], xs[1])
            with _jax.named_scope("update"):
                return (loss_sum + l_k, _jax.tree.map(_jnp.add, grad_sum, gw_k)), gx_k

        init = (_jnp.zeros((), _jnp.float32), _jax.tree.map(_jnp.zeros_like, weights))
        (loss, grad_w), grad_x = _jax.lax.scan(body, init, (per_example, given["loss_target"]))
    with _jax.named_scope("update"):
        delta_w, new_m, new_v = {}, {}, {}
        for n in TWIN_WEIGHTS:
            delta_w[n], new_m[n], new_v[n] = _adamw(weights[n], grad_w[n], given["m_" + n], given["v_" + n])
    return (loss, grad_x, *[grad_w[n] for n in TWIN_WEIGHTS], *[delta_w[n] for n in TWIN_WEIGHTS],
            *[new_m[n] for n in TWIN_WEIGHTS], *[new_v[n] for n in TWIN_WEIGHTS])
```

```python
import functools
import math

import jax
import jax.numpy as jnp
from jax import lax
from jax.experimental import pallas as pl
from jax.experimental.pallas import tpu as pltpu

F32 = jnp.float32
CDT = jnp.bfloat16
NORM_EPS = 1e-6
POOL_WINDOWS = (2, 4, 8, 16)
POOL_HALO = 16
SSM_H = 16
LANES = 128
FF_ALIGN = 256
NCHIP = 4
VMEM_LIMIT = 48 * 1024 * 1024
ADAM_LR = 0.001
ADAM_B1 = 0.9
ADAM_B2 = 0.999
ADAM_EPS = 1e-08
ADAM_WD = 0.01
ADAM_STEP = 10
MESH = pl.DeviceIdType.MESH


def _cp(n_grid):
    return pltpu.CompilerParams(dimension_semantics=("arbitrary",) * n_grid, vmem_limit_bytes=VMEM_LIMIT)


def _tile(n, pref, align=8):
    if n <= pref:
        return n
    t = (pref // align) * align
    while t >= align:
        if n % t == 0:
            return t
        t -= align
    return n


def _silu(x):
    return x * jax.nn.sigmoid(x)


_GELU_C = math.sqrt(2.0 / math.pi)


def _gelu(x):
    return x * (0.5 * (1.0 + jnp.tanh(_GELU_C * (x + 0.044715 * (x * x * x)))))


def _gelu_grad(x):
    t = jnp.tanh(_GELU_C * (x + 0.044715 * (x * x * x)))
    return 0.5 * (1.0 + t) + 0.5 * x * (1.0 - t * t) * (_GELU_C * (1.0 + 3.0 * 0.044715 * x * x))


def _mm(name, dn, grid, a_list, a_specs, b_list, b_specs, e_list, e_specs, out_shapes, out_specs,
        acc_shape, pair_to_acc, epi):
    na, nb, ne, no = len(a_list), len(b_list), len(e_list), len(out_shapes)
    assert na in (1, nb)
    n_acc = max(pair_to_acc) + 1
    nk = grid[2]

    def body(*refs):
        a_refs = refs[:na]
        b_refs = refs[na:na + nb]
        e_refs = refs[na + nb:na + nb + ne]
        o_refs = refs[na + nb + ne:na + nb + ne + no]
        acc_refs = refs[na + nb + ne + no:]
        prods = [None] * n_acc
        a_vals = [a_ref[...].astype(CDT) for a_ref in a_refs]
        for p in range(nb):
            d = lax.dot_general(a_vals[p if na > 1 else 0], b_refs[p][...].astype(CDT), dn,
                                preferred_element_type=F32)
            q = pair_to_acc[p]
            prods[q] = d if prods[q] is None else prods[q] + d

        def finish(accs):
            outs = epi(accs, [e[...] for e in e_refs])
            for o_ref, o in zip(o_refs, outs):
                o_ref[...] = o.astype(o_ref.dtype)

        if nk == 1:
            finish(prods)
        else:
            k = pl.program_id(2)

            @pl.when(k == 0)
            def _():
                for acc, p in zip(acc_refs, prods):
                    acc[...] = p

            @pl.when(k > 0)
            def _():
                for acc, p in zip(acc_refs, prods):
                    acc[...] += p

            @pl.when(k == nk - 1)
            def _():
                finish([acc[...] for acc in acc_refs])

    scratch = [pltpu.VMEM(acc_shape, F32) for _ in range(n_acc)] if nk > 1 else []
    outs = pl.pallas_call(
        body, name=name, grid=grid,
        in_specs=list(a_specs) + list(b_specs) + list(e_specs),
        out_specs=list(out_specs), out_shape=list(out_shapes),
        scratch_shapes=scratch, compiler_params=_cp(3),
    )(*a_list, *b_list, *e_list)
    return outs


def _extra_specs(extras, tm, tn):
    arrs, specs = [], []
    for arr, kind, off in extras:
        arrs.append(arr)
        if kind == "mn":
            specs.append(pl.BlockSpec((tm, tn), lambda i, j, k, off=off: (i, j + off)))
        elif kind == "n":
            specs.append(pl.BlockSpec((1, tn), lambda i, j, k, off=off: (0, j + off)))
        else:
            raise ValueError(kind)
    return arrs, specs


def _first(accs, extras):
    return (accs[0],)


def mm_nn(name, a_list, b_list, *, tm, tn, tk=None, b3=False, extras=(), out_dtypes=(F32,), epi=_first,
          pair_to_acc=None):
    M, K = a_list[0].shape
    if b3:
        s4, _, ns = b_list[0].shape
        N = s4 * ns
        tn = _tile(ns, tn, LANES)
        nps = ns // tn
    else:
        N = b_list[0].shape[1]
        tn = _tile(N, tn, LANES)
    tm = _tile(M, tm, 16)
    tk = _tile(K, tk or K, LANES)
    grid = (M // tm, N // tn, K // tk)
    a_specs = [pl.BlockSpec((tm, tk), lambda i, j, k: (i, k)) for _ in a_list]
    if b3:
        b_specs = [pl.BlockSpec((None, tk, tn), lambda i, j, k: (j // nps, k, j % nps)) for _ in b_list]
    else:
        b_specs = [pl.BlockSpec((tk, tn), lambda i, j, k: (k, j)) for _ in b_list]
    e_list, e_specs = _extra_specs(extras, tm, tn)
    out_shapes = [jax.ShapeDtypeStruct((M, N), dt) for dt in out_dtypes]
    out_specs = [pl.BlockSpec((tm, tn), lambda i, j, k: (i, j)) for _ in out_dtypes]
    pair_to_acc = pair_to_acc or [0] * len(b_list)
    dn = (((1,), (0,)), ((), ()))
    return _mm(name, dn, grid, a_list, a_specs, b_list, b_specs, e_list, e_specs, out_shapes, out_specs,
               (tm, tn), pair_to_acc, epi)


def mm_nt(name, a_list, b_list, *, tm, tn, tk=None, b3=False, extras=(), out_dtypes=(F32,), epi=_first,
          pair_to_acc=None):
    M, K = a_list[0].shape
    if b3:
        s4, N, ks = b_list[0].shape
        tk = _tile(ks, tk or ks, LANES)
        kps = ks // tk
    else:
        N = b_list[0].shape[0]
        tk = _tile(K, tk or K, LANES)
    tm = _tile(M, tm, 16)
    tn = _tile(N, tn, LANES)
    grid = (M // tm, N // tn, K // tk)
    a_specs = [pl.BlockSpec((tm, tk), lambda i, j, k: (i, k)) for _ in a_list]
    if b3:
        b_specs = [pl.BlockSpec((None, tn, tk), lambda i, j, k: (k // kps, j, k % kps)) for _ in b_list]
    else:
        b_specs = [pl.BlockSpec((tn, tk), lambda i, j, k: (j, k)) for _ in b_list]
    e_list, e_specs = _extra_specs(extras, tm, tn)
    out_shapes = [jax.ShapeDtypeStruct((M, N), dt) for dt in out_dtypes]
    out_specs = [pl.BlockSpec((tm, tn), lambda i, j, k: (i, j)) for _ in out_dtypes]
    pair_to_acc = pair_to_acc or [0] * len(b_list)
    dn = (((1,), (1,)), ((), ()))
    return _mm(name, dn, grid, a_list, a_specs, b_list, b_specs, e_list, e_specs, out_shapes, out_specs,
               (tm, tn), pair_to_acc, epi)


def mm_tn(name, a_list, b_list, *, ti, tj, o3=0, out_dtype=CDT):
    T, I = a_list[0].shape
    J = b_list[0].shape[1]
    ti = _tile(I, ti, LANES)
    if o3:
        js = J // o3
        tj = _tile(js, tj, LANES)
        jps = js // tj
        out_shapes = [jax.ShapeDtypeStruct((o3, I, js), out_dtype) for _ in b_list]
        out_specs = [pl.BlockSpec((None, ti, tj), lambda i, j, k: (j // jps, i, j % jps)) for _ in b_list]
    else:
        tj = _tile(J, tj, LANES)
        out_shapes = [jax.ShapeDtypeStruct((I, J), out_dtype) for _ in b_list]
        out_specs = [pl.BlockSpec((ti, tj), lambda i, j, k: (i, j)) for _ in b_list]
    grid = (I // ti, J // tj, 1)
    a_specs = [pl.BlockSpec((T, ti), lambda i, j, k: (0, i)) for _ in a_list]
    b_specs = [pl.BlockSpec((T, tj), lambda i, j, k: (0, j)) for _ in b_list]
    dn = (((0,), (0,)), ((), ()))
    return _mm(name, dn, grid, a_list, a_specs, b_list, b_specs, [], [], out_shapes, out_specs,
               (ti, tj), list(range(len(b_list))), lambda accs, extras: tuple(accs))


def rms_fwd(name, x, gain):
    N, D = x.shape
    tm = _tile(N, 128)

    def body(x_ref, g_ref, n_ref, r_ref):
        xf = x_ref[...]
        r = lax.rsqrt(jnp.mean(xf * xf, axis=-1, keepdims=True) + NORM_EPS)
        n_ref[...] = ((xf * r) * g_ref[...]).astype(n_ref.dtype)
        r_ref[...] = r

    return pl.pallas_call(
        body, name=name, grid=(N // tm,),
        in_specs=[pl.BlockSpec((tm, D), lambda i: (i, 0)), pl.BlockSpec((1, D), lambda i: (0, 0))],
        out_specs=[pl.BlockSpec((tm, D), lambda i: (i, 0)), pl.BlockSpec((tm, 1), lambda i: (i, 0))],
        out_shape=[jax.ShapeDtypeStruct((N, D), CDT), jax.ShapeDtypeStruct((N, 1), F32)],
        compiler_params=_cp(1),
    )(x, gain)


def _rms_bwd_math(dn, x, r, g):
    xhat = x * r
    dgain = jnp.sum(dn * xhat, axis=0, keepdims=True)
    dxhat = dn * g
    dx = r * (dxhat - xhat * jnp.mean(dxhat * xhat, axis=-1, keepdims=True))
    return dx, dgain


def rms_bwd(name, dn, x, rstd, gain, dres, cast_scale):
    N, D = x.shape
    tm = _tile(N, 128)

    def body(dn_ref, x_ref, r_ref, g_ref, dres_ref, dx_ref, dc_ref, dg_ref):
        dx, dgain = _rms_bwd_math(dn_ref[...], x_ref[...], r_ref[...], g_ref[...])
        tot = dres_ref[...] + dx
        dx_ref[...] = tot
        dc_ref[...] = (cast_scale * tot).astype(dc_ref.dtype)

        @pl.when(pl.program_id(0) == 0)
        def _():
            dg_ref[...] = jnp.zeros_like(dg_ref)

        dg_ref[...] += dgain

    row = pl.BlockSpec((tm, D), lambda i: (i, 0))
    vec = pl.BlockSpec((1, D), lambda i: (0, 0))
    return pl.pallas_call(
        body, name=name, grid=(N // tm,),
        in_specs=[row, row, pl.BlockSpec((tm, 1), lambda i: (i, 0)), vec, row],
        out_specs=[row, row, vec],
        out_shape=[jax.ShapeDtypeStruct((N, D), F32), jax.ShapeDtypeStruct((N, D), CDT),
                   jax.ShapeDtypeStruct((1, D), F32)],
        compiler_params=_cp(1),
    )(dn, x, rstd, gain, dres)


def loss_head(name, h, gain, target):
    N, D = h.shape
    tm = _tile(N, 128)

    def body(h_ref, g_ref, t_ref, l_ref, dh_ref, dc_ref, dg_ref):
        x = h_ref[...]
        g = g_ref[...]
        r = lax.rsqrt(jnp.mean(x * x, axis=-1, keepdims=True) + NORM_EPS)
        err = (x * r) * g - t_ref[...]
        dy = err * (1.0 / D)
        dx, dgain = _rms_bwd_math(dy, x, r, g)
        dh_ref[...] = dx
        dc_ref[...] = (0.5 * dx).astype(dc_ref.dtype)

        @pl.when(pl.program_id(0) == 0)
        def _():
            dg_ref[...] = jnp.zeros_like(dg_ref)
            l_ref[...] = jnp.zeros_like(l_ref)

        dg_ref[...] += dgain
        l_ref[...] += jnp.sum(err * err)

    row = pl.BlockSpec((tm, D), lambda i: (i, 0))
    vec = pl.BlockSpec((1, D), lambda i: (0, 0))
    return pl.pallas_call(
        body, name=name, grid=(N // tm,),
        in_specs=[row, vec, row],
        out_specs=[pl.BlockSpec((1, LANES), lambda i: (0, 0)), row, row, vec],
        out_shape=[jax.ShapeDtypeStruct((1, LANES), F32), jax.ShapeDtypeStruct((N, D), F32),
                   jax.ShapeDtypeStruct((N, D), CDT), jax.ShapeDtypeStruct((1, D), F32)],
        compiler_params=_cp(1),
    )(h, gain, target)


def merge_norm_fwd(name, y_pool, y_ssm, g_pool, g_ssm):
    N, PW = y_pool.shape
    SW = y_ssm.shape[1]
    tm = _tile(N, 128)

    def body(yp_ref, ys_ref, gp_ref, gs_ref, m_ref, rp_ref, rs_ref):
        yp = yp_ref[...]
        ys = ys_ref[...]
        rp = lax.rsqrt(jnp.mean(yp * yp, axis=-1, keepdims=True) + NORM_EPS)
        rs = lax.rsqrt(jnp.mean(ys * ys, axis=-1, keepdims=True) + NORM_EPS)
        m_ref[:, :PW] = ((yp * rp) * gp_ref[...]).astype(m_ref.dtype)
        m_ref[:, PW:] = ((ys * rs) * gs_ref[...]).astype(m_ref.dtype)
        rp_ref[...] = rp
        rs_ref[...] = rs

    return pl.pallas_call(
        body, name=name, grid=(N // tm,),
        in_specs=[pl.BlockSpec((tm, PW), lambda i: (i, 0)), pl.BlockSpec((tm, SW), lambda i: (i, 0)),
                  pl.BlockSpec((1, PW), lambda i: (0, 0)), pl.BlockSpec((1, SW), lambda i: (0, 0))],
        out_specs=[pl.BlockSpec((tm, PW + SW), lambda i: (i, 0)), pl.BlockSpec((tm, 1), lambda i: (i, 0)),
                   pl.BlockSpec((tm, 1), lambda i: (i, 0))],
        out_shape=[jax.ShapeDtypeStruct((N, PW + SW), CDT), jax.ShapeDtypeStruct((N, 1), F32),
                   jax.ShapeDtypeStruct((N, 1), F32)],
        compiler_params=_cp(1),
    )(y_pool, y_ssm, g_pool, g_ssm)


def merge_norm_bwd(name, dmerged, y_pool, y_ssm, r_pool, r_ssm, g_pool, g_ssm):
    N, PW = y_pool.shape
    SW = y_ssm.shape[1]
    tm = _tile(N, 128)

    def body(dm_ref, yp_ref, ys_ref, rp_ref, rs_ref, gp_ref, gs_ref, dyp_ref, dys_ref, dgp_ref, dgs_ref):
        dxp, dgp = _rms_bwd_math(dm_ref[:, :PW], yp_ref[...], rp_ref[...], gp_ref[...])
        dxs, dgs = _rms_bwd_math(dm_ref[:, PW:], ys_ref[...], rs_ref[...], gs_ref[...])
        dyp_ref[...] = dxp
        dys_ref[...] = dxs

        @pl.when(pl.program_id(0) == 0)
        def _():
            dgp_ref[...] = jnp.zeros_like(dgp_ref)
            dgs_ref[...] = jnp.zeros_like(dgs_ref)

        dgp_ref[...] += dgp
        dgs_ref[...] += dgs

    col1 = pl.BlockSpec((tm, 1), lambda i: (i, 0))
    return pl.pallas_call(
        body, name=name, grid=(N // tm,),
        in_specs=[pl.BlockSpec((tm, PW + SW), lambda i: (i, 0)), pl.BlockSpec((tm, PW), lambda i: (i, 0)),
                  pl.BlockSpec((tm, SW), lambda i: (i, 0)), col1, col1,
                  pl.BlockSpec((1, PW), lambda i: (0, 0)), pl.BlockSpec((1, SW), lambda i: (0, 0))],
        out_specs=[pl.BlockSpec((tm, PW), lambda i: (i, 0)), pl.BlockSpec((tm, SW), lambda i: (i, 0)),
                   pl.BlockSpec((1, PW), lambda i: (0, 0)), pl.BlockSpec((1, SW), lambda i: (0, 0))],
        out_shape=[jax.ShapeDtypeStruct((N, PW), F32), jax.ShapeDtypeStruct((N, SW), F32),
                   jax.ShapeDtypeStruct((1, PW), F32), jax.ShapeDtypeStruct((1, SW), F32)],
        compiler_params=_cp(1),
    )(dmerged, y_pool, y_ssm, r_pool, r_ssm, g_pool, g_ssm)


def pool_fwd(name, z, w_pool, scale, n_seq, seq):
    G, Cg, _ = w_pool.shape
    N = n_seq * seq
    PW = G * Cg

    def body(z_ref, w_ref, s_ref, d_ref, y_ref, zs_ref):
        g = pl.program_id(0)
        zv = z_ref[...]
        zs_ref[pl.ds(0, POOL_HALO), :] = jnp.zeros((POOL_HALO, Cg), F32)
        zs_ref[pl.ds(POOL_HALO, seq), :] = zv
        t = lax.broadcasted_iota(jnp.int32, (seq, 1), 0)
        for gi, w in enumerate(POOL_WINDOWS):
            @pl.when(g == gi)
            def _(w=w):
                acc = zv
                for j in range(1, w):
                    acc = acc + zs_ref[pl.ds(POOL_HALO - j, seq), :]
                cnt = jnp.minimum(t + 1, w).astype(F32)
                d = (acc / cnt - zv).astype(d_ref.dtype)
                d_ref[...] = d
                out = jnp.dot(d, w_ref[...], preferred_element_type=F32)
                y_ref[...] = out * s_ref[...]

    blk = pl.BlockSpec((seq, Cg), lambda g, b: (b, g))
    return pl.pallas_call(
        body, name=name, grid=(G, n_seq),
        in_specs=[blk, pl.BlockSpec((None, Cg, Cg), lambda g, b: (g, 0, 0)),
                  pl.BlockSpec((1, Cg), lambda g, b: (0, g))],
        out_specs=[blk, blk],
        out_shape=[jax.ShapeDtypeStruct((N, PW), CDT), jax.ShapeDtypeStruct((N, PW), F32)],
        scratch_shapes=[pltpu.VMEM((seq + POOL_HALO, Cg), F32)],
        compiler_params=_cp(2),
    )(z, w_pool, scale)


def pool_bwd(name, dy_pool, d, w_pool, scale, n_seq, seq):
    G, Cg, _ = w_pool.shape
    N = n_seq * seq
    PW = G * Cg

    def body(dy_ref, d_ref, w_ref, s_ref, dz_ref, dw_ref, ds_ref, es_ref):
        g = pl.program_id(0)
        b = pl.program_id(1)
        dv = d_ref[...]
        wv = w_ref[...]
        dy = dy_ref[...]
        out = jnp.dot(dv, wv, preferred_element_type=F32)
        dout = (dy * s_ref[...]).astype(CDT)
        dw = lax.dot_general(dv, dout, (((0,), (0,)), ((), ())), preferred_element_type=F32)
        dd = lax.dot_general(dout, wv, (((1,), (1,)), ((), ())), preferred_element_type=F32)

        @pl.when(b == 0)
        def _():
            dw_ref[...] = jnp.zeros_like(dw_ref)
            ds_ref[...] = jnp.zeros_like(ds_ref)

        dw_ref[...] += dw
        ds_ref[...] += jnp.sum(out * dy, axis=0, keepdims=True)
        t = lax.broadcasted_iota(jnp.int32, (seq, 1), 0)
        es_ref[pl.ds(seq, POOL_HALO), :] = jnp.zeros((POOL_HALO, Cg), F32)
        for gi, w in enumerate(POOL_WINDOWS):
            @pl.when(g == gi)
            def _(w=w):
                cnt = jnp.minimum(t + 1, w).astype(F32)
                e = dd / cnt
                es_ref[pl.ds(0, seq), :] = e
                acc = e
                for j in range(1, w):
                    acc = acc + es_ref[pl.ds(j, seq), :]
                dz_ref[...] = (acc - dd).astype(dz_ref.dtype)

    blk = pl.BlockSpec((seq, Cg), lambda g, b: (b, g))
    return pl.pallas_call(
        body, name=name, grid=(G, n_seq),
        in_specs=[blk, blk, pl.BlockSpec((None, Cg, Cg), lambda g, b: (g, 0, 0)),
                  pl.BlockSpec((1, Cg), lambda g, b: (0, g))],
        out_specs=[blk, pl.BlockSpec((None, Cg, Cg), lambda g, b: (g, 0, 0)),
                   pl.BlockSpec((1, Cg), lambda g, b: (0, g))],
        out_shape=[jax.ShapeDtypeStruct((N, PW), CDT), jax.ShapeDtypeStruct((G, Cg, Cg), F32),
                   jax.ShapeDtypeStruct((1, PW), F32)],
        scratch_shapes=[pltpu.VMEM((seq + POOL_HALO, Cg), F32)],
        compiler_params=_cp(2),
    )(dy_pool, d, w_pool, scale)


def _ssm_param_math(lr, li, ldt, br, bi):
    dt = jnp.exp(ldt)
    mag = jnp.exp(lr * dt)
    ar = mag * jnp.cos(li * dt)
    ai = mag * jnp.sin(li * dt)
    den = lr * lr + li * li
    xr = ar - 1.0
    cr = (xr * lr + ai * li) / den
    ci = (ai * lr - xr * li) / den
    return ar, ai, cr * br - ci * bi, cr * bi + ci * br


def ssm_params_fwd(name, lr, li, ldt, br, bi):
    GP, H = br.shape

    def body(lr_ref, li_ref, ldt_ref, br_ref, bi_ref, ar_ref, ai_ref, bbr_ref, bbi_ref):
        ar, ai, bbr, bbi = _ssm_param_math(lr_ref[...], li_ref[...], ldt_ref[...], br_ref[...], bi_ref[...])
        ar_ref[...] = ar
        ai_ref[...] = ai
        bbr_ref[...] = bbr
        bbi_ref[...] = bbi

    c1 = jax.ShapeDtypeStruct((GP, 1), F32)
    ch = jax.ShapeDtypeStruct((GP, H), F32)
    tr = _tile(GP, 512)
    b1 = pl.BlockSpec((tr, 1), lambda i: (i, 0))
    bh = pl.BlockSpec((tr, H), lambda i: (i, 0))
    return pl.pallas_call(body, name=name, grid=(GP // tr,), in_specs=[b1, b1, b1, bh, bh],
                          out_specs=[b1, b1, bh, bh], out_shape=[c1, c1, ch, ch],
                          compiler_params=_cp(1))(lr, li, ldt, br, bi)


def ssm_params_bwd(name, lr, li, ldt, br, bi, dar, dai, dbbr, dbbi):
    GP, H = br.shape

    def body(lr_ref, li_ref, ldt_ref, br_ref, bi_ref, dar_ref, dai_ref, dbbr_ref, dbbi_ref,
             dlr_ref, dli_ref, dldt_ref, dbr_ref, dbi_ref):
        _, vjp = jax.vjp(_ssm_param_math, lr_ref[...], li_ref[...], ldt_ref[...], br_ref[...], bi_ref[...])
        dlr, dli, dldt, dbr, dbi = vjp((dar_ref[...], dai_ref[...], dbbr_ref[...], dbbi_ref[...]))
        dlr_ref[...] = dlr
        dli_ref[...] = dli
        dldt_ref[...] = dldt
        dbr_ref[...] = dbr
        dbi_ref[...] = dbi

    c1 = jax.ShapeDtypeStruct((GP, 1), F32)
    ch = jax.ShapeDtypeStruct((GP, H), F32)
    tr = _tile(GP, 512)
    b1 = pl.BlockSpec((tr, 1), lambda i: (i, 0))
    bh = pl.BlockSpec((tr, H), lambda i: (i, 0))
    return pl.pallas_call(body, name=name, grid=(GP // tr,), in_specs=[b1, b1, b1, bh, bh, b1, b1, bh, bh],
                          out_specs=[b1, b1, b1, bh, bh], out_shape=[c1, c1, c1, ch, ch],
                          compiler_params=_cp(1))(lr, li, ldt, br, bi, dar, dai, dbbr, dbbi)


def ssm_expand(name, src, col_off, mat, tm):
    N = src.shape[0]
    NC, cw, w2 = mat.shape
    W = w2 // 2
    tm = _tile(N, tm)

    def body(s_ref, m_ref, re_ref, im_ref):
        r = jnp.dot(s_ref[...].astype(CDT), m_ref[...], preferred_element_type=F32)
        re_ref[...] = r[:, :W]
        im_ref[...] = r[:, W:]

    out = pl.BlockSpec((tm, W), lambda i, c: (i, c))
    return pl.pallas_call(
        body, name=name, grid=(N // tm, NC),
        in_specs=[pl.BlockSpec((tm, cw), lambda i, c: (i, c + col_off)),
                  pl.BlockSpec((None, cw, w2), lambda i, c: (c, 0, 0))],
        out_specs=[out, out],
        out_shape=[jax.ShapeDtypeStruct((N, NC * W), F32)] * 2,
        compiler_params=_cp(2),
    )(src, mat)


def ssm_contract(name, re, im, mat, extras, epi, out_dtypes, tm):
    N = re.shape[0]
    NC, w2, cw = mat.shape
    W = w2 // 2
    tm = _tile(N, tm)
    ne = len(extras)

    def body(*refs):
        re_ref, im_ref, m_ref = refs[:3]
        e_refs = refs[3:3 + ne]
        o_refs = refs[3 + ne:]
        acc = jnp.dot(re_ref[...].astype(CDT), m_ref[pl.ds(0, W), :], preferred_element_type=F32)
        acc = acc + jnp.dot(im_ref[...].astype(CDT), m_ref[pl.ds(W, W), :], preferred_element_type=F32)
        outs = epi(acc, [e[...] for e in e_refs])
        for o_ref, o in zip(o_refs, outs):
            o_ref[...] = o.astype(o_ref.dtype)

    e_arrs, e_specs = [], []
    for arr, kind, off in extras:
        e_arrs.append(arr)
        if kind == "mn":
            e_specs.append(pl.BlockSpec((tm, cw), lambda i, c, off=off: (i, c + off)))
        else:
            e_specs.append(pl.BlockSpec((1, cw), lambda i, c, off=off: (0, c + off)))
    blk = pl.BlockSpec((tm, W), lambda i, c: (i, c))
    return pl.pallas_call(
        body, name=name, grid=(N // tm, NC),
        in_specs=[blk, blk, pl.BlockSpec((None, w2, cw), lambda i, c: (c, 0, 0))] + e_specs,
        out_specs=[pl.BlockSpec((tm, cw), lambda i, c: (i, c)) for _ in out_dtypes],
        out_shape=[jax.ShapeDtypeStruct((N, NC * cw), dt) for dt in out_dtypes],
        compiler_params=_cp(2),
    )(re, im, mat, *e_arrs)


def ssm_outer(name, src, col_off, re, im, NC, Wc, tm):
    N = src.shape[0]
    tm = _tile(N, tm)

    def body(s_ref, re_ref, im_ref, o1_ref, o2_ref):
        @pl.when(pl.program_id(1) == 0)
        def _():
            o1_ref[...] = jnp.zeros_like(o1_ref)
            o2_ref[...] = jnp.zeros_like(o2_ref)

        sv = s_ref[...].astype(CDT)
        dn = (((0,), (0,)), ((), ()))
        o1_ref[...] += lax.dot_general(sv, re_ref[...].astype(CDT), dn, preferred_element_type=F32)
        o2_ref[...] += lax.dot_general(sv, im_ref[...].astype(CDT), dn, preferred_element_type=F32)

    blk = pl.BlockSpec((tm, Wc), lambda c, i: (i, c))
    oblk = pl.BlockSpec((None, LANES, Wc), lambda c, i: (c, 0, 0))
    return pl.pallas_call(
        body, name=name, grid=(NC, N // tm),
        in_specs=[pl.BlockSpec((tm, LANES), lambda c, i: (i, c + col_off)), blk, blk],
        out_specs=[oblk, oblk],
        out_shape=[jax.ShapeDtypeStruct((NC, LANES, Wc), F32)] * 2,
        compiler_params=_cp(2),
    )(src, re, im)


def ssm_scan_fwd(name, bur, bui, ar, ai, n_seq, seq, tc):
    N, R, L = bur.shape
    tc = _tile(seq, tc, 1)
    nt = seq // tc

    def body(bur_ref, bui_ref, ar_ref, ai_ref, xr_ref, xi_ref, st_ref):
        @pl.when(pl.program_id(1) == 0)
        def _():
            st_ref[...] = jnp.zeros_like(st_ref)

        a_r = ar_ref[...]
        a_i = ai_ref[...]

        def step(t, carry):
            xr, xi = carry
            nr = a_r * xr - a_i * xi + bur_ref[t]
            ni = a_r * xi + a_i * xr + bui_ref[t]
            xr_ref[t] = nr
            xi_ref[t] = ni
            return nr, ni

        xr, xi = lax.fori_loop(0, tc, step, (st_ref[0], st_ref[1]))
        st_ref[0] = xr
        st_ref[1] = xi

    blk = pl.BlockSpec((tc, R, L), lambda b, j: (b * nt + j, 0, 0))
    par = pl.BlockSpec((R, L), lambda b, j: (0, 0))
    return pl.pallas_call(
        body, name=name, grid=(n_seq, nt),
        in_specs=[blk, blk, par, par], out_specs=[blk, blk],
        out_shape=[jax.ShapeDtypeStruct((N, R, L), F32)] * 2,
        scratch_shapes=[pltpu.VMEM((2, R, L), F32)],
        compiler_params=_cp(2),
    )(bur, bui, ar, ai)


def ssm_scan_bwd(name, gr, gi, xr, xi, ar, ai, n_seq, seq, tc):
    N, R, L = gr.shape
    tc = _tile(seq, tc, 1)
    nt = seq // tc

    def body(gr_ref, gi_ref, xr_ref, xi_ref, ar_ref, ai_ref, lr_ref, li_ref, dar_ref, dai_ref, st_ref):
        b = pl.program_id(0)
        j = pl.program_id(1)

        @pl.when((b == 0) & (j == 0))
        def _():
            dar_ref[...] = jnp.zeros_like(dar_ref)
            dai_ref[...] = jnp.zeros_like(dai_ref)

        @pl.when(j == 0)
        def _():
            st_ref[...] = jnp.zeros_like(st_ref)

        a_r = ar_ref[...]
        a_i = ai_ref[...]

        def step(s, carry):
            t = tc - 1 - s
            lr, li, dr, di = carry
            xrt = xr_ref[t]
            xit = xi_ref[t]
            dr = dr + (lr * xrt + li * xit)
            di = di + (li * xrt - lr * xit)
            nlr = gr_ref[t] + (a_r * lr + a_i * li)
            nli = gi_ref[t] + (a_r * li - a_i * lr)
            lr_ref[t] = nlr
            li_ref[t] = nli
            return nlr, nli, dr, di

        lr, li, dr, di = lax.fori_loop(0, tc, step, (st_ref[0], st_ref[1], dar_ref[...], dai_ref[...]))
        st_ref[0] = lr
        st_ref[1] = li
        dar_ref[...] = dr
        dai_ref[...] = di

    blk = pl.BlockSpec((tc, R, L), lambda b, j: (b * nt + nt - 1 - j, 0, 0))
    par = pl.BlockSpec((R, L), lambda b, j: (0, 0))
    return pl.pallas_call(
        body, name=name, grid=(n_seq, nt),
        in_specs=[blk, blk, blk, blk, par, par], out_specs=[blk, blk, par, par],
        out_shape=[jax.ShapeDtypeStruct((N, R, L), F32)] * 2 + [jax.ShapeDtypeStruct((R, L), F32)] * 2,
        scratch_shapes=[pltpu.VMEM((2, R, L), F32)],
        compiler_params=_cp(2),
    )(gr, gi, xr, xi, ar, ai)


def glu_bwd_pre(name, dy_ssm, y_pre, q):
    N, SW = y_pre.shape
    tm = _tile(N, 128)

    def body(dy_ref, y_ref, q_ref, dq_ref, dyg_ref, db_ref):
        dy = dy_ref[...]
        yg = _gelu(y_ref[...])
        s = jax.nn.sigmoid(q_ref[...])
        dq = dy * yg * (s * (1.0 - s))
        dq_ref[...] = dq.astype(dq_ref.dtype)
        dyg_ref[...] = dy * s

        @pl.when(pl.program_id(0) == 0)
        def _():
            db_ref[...] = jnp.zeros_like(db_ref)

        db_ref[...] += jnp.sum(dq, axis=0, keepdims=True)

    row = pl.BlockSpec((tm, SW), lambda i: (i, 0))
    vec = pl.BlockSpec((1, SW), lambda i: (0, 0))
    return pl.pallas_call(
        body, name=name, grid=(N // tm,), in_specs=[row, row, row], out_specs=[row, row, vec],
        out_shape=[jax.ShapeDtypeStruct((N, SW), CDT), jax.ShapeDtypeStruct((N, SW), F32),
                   jax.ShapeDtypeStruct((1, SW), F32)],
        compiler_params=_cp(1),
    )(dy_ssm, y_pre, q)


def colsum_prod(name, a, b, b_col_off, width):
    N = a.shape[0]
    tm = _tile(N, 128)

    def body(a_ref, b_ref, o_ref):
        @pl.when(pl.program_id(0) == 0)
        def _():
            o_ref[...] = jnp.zeros_like(o_ref)

        o_ref[...] += jnp.sum(a_ref[...] * b_ref[...], axis=0, keepdims=True)

    return pl.pallas_call(
        body, name=name, grid=(N // tm,),
        in_specs=[pl.BlockSpec((tm, width), lambda i: (i, 0)),
                  pl.BlockSpec((tm, width), lambda i: (i, b_col_off))],
        out_specs=pl.BlockSpec((1, width), lambda i: (0, 0)),
        out_shape=jax.ShapeDtypeStruct((1, width), F32),
        compiler_params=_cp(1),
    )(a, b)


def adamw(name, w, g, m, v):
    R, C = w.shape
    tr = _tile(R, max(8, (1 << 18) // C))

    def body(w_ref, g_ref, m_ref, v_ref, d_ref, nm_ref, nv_ref):
        gv = g_ref[...]
        nm = ADAM_B1 * m_ref[...] + (1.0 - ADAM_B1) * gv
        nv = ADAM_B2 * v_ref[...] + (1.0 - ADAM_B2) * jnp.square(gv)
        m_hat = nm / (1.0 - ADAM_B1 ** ADAM_STEP)
        v_hat = nv / (1.0 - ADAM_B2 ** ADAM_STEP)
        d_ref[...] = -ADAM_LR * (m_hat / (jnp.sqrt(v_hat) + ADAM_EPS) + ADAM_WD * w_ref[...])
        nm_ref[...] = nm
        nv_ref[...] = nv

    blk = pl.BlockSpec((tr, C), lambda i: (i, 0))
    sh = jax.ShapeDtypeStruct((R, C), F32)
    return pl.pallas_call(body, name=name, grid=(R // tr,), in_specs=[blk] * 4, out_specs=[blk] * 3,
                          out_shape=[sh] * 3, compiler_params=_cp(1))(w, g, m, v)


def add_halves(name, part, recv, c_idx):
    S4, R, C = part.shape
    h = R // 2
    tr = _tile(h, max(16, (1 << 19) // C), 16)
    nb = h // tr

    def body(c_ref, p_ref, r_ref, o_ref):
        o_ref[...] = (p_ref[...].astype(F32) + r_ref[...].astype(F32)).astype(o_ref.dtype)

    grid_spec = pltpu.PrefetchScalarGridSpec(
        num_scalar_prefetch=1, grid=(S4, nb),
        in_specs=[pl.BlockSpec((None, tr, C), lambda s, i, c_ref: (s, c_ref[0] * nb + i, 0)),
                  pl.BlockSpec((None, tr, C), lambda s, i, c_ref: (s, i, 0))],
        out_specs=pl.BlockSpec((None, tr, C), lambda s, i, c_ref: (s, i, 0)),
    )
    return pl.pallas_call(body, name=name, grid_spec=grid_spec,
                          out_shape=jax.ShapeDtypeStruct((S4, h, C), CDT),
                          compiler_params=_cp(2))(c_idx, part, recv)


def sum_chips(name, u):
    S4, h, C = u.shape
    tr = _tile(h, max(16, (1 << 18) // C), 16)

    def body(u_ref, o_ref):
        acc = u_ref[0].astype(F32)
        for q in range(1, S4):
            acc = acc + u_ref[q].astype(F32)
        o_ref[...] = acc

    return pl.pallas_call(
        body, name=name, grid=(h // tr,),
        in_specs=[pl.BlockSpec((S4, tr, C), lambda i: (0, i, 0))],
        out_specs=pl.BlockSpec((tr, C), lambda i: (i, 0)),
        out_shape=jax.ShapeDtypeStruct((h, C), F32), compiler_params=_cp(1))(u)


def _place():
    x, y, c = lax.axis_index("x"), lax.axis_index("y"), lax.axis_index("c")
    chips = [(1 - x, y), (x, 1 - y), (1 - x, 1 - y)]
    return x, y, c, chips


_ANY = pl.BlockSpec(memory_space=pl.ANY)


def gather_shards(name, shards):
    n = len(shards)

    def body(*refs):
        ins = refs[:n]
        outs = refs[n:2 * n]
        send_sems, recv_sems, loc_sems = refs[2 * n:]
        x, y, c, chips = _place()
        me = 2 * x + y
        sib = (x, y, 1 - c)
        sends = []
        locs = []
        for k in range(n):
            h = ins[k].shape[0] // 2
            cp = pltpu.make_async_copy(ins[k], outs[k].at[me], loc_sems.at[k])
            cp.start()
            locs.append(cp)
            for r, (qx, qy) in enumerate(chips):
                cp = pltpu.make_async_remote_copy(
                    src_ref=ins[k].at[pl.ds(c * h, h)], dst_ref=outs[k].at[me, pl.ds(c * h, h)],
                    send_sem=send_sems.at[k, r], recv_sem=recv_sems.at[k, r],
                    device_id=(qx, qy, c), device_id_type=MESH)
                cp.start()
                sends.append(cp)
        for k in range(n):
            h = ins[k].shape[0] // 2
            for r, (qx, qy) in enumerate(chips):
                q = 2 * qx + qy
                region = outs[k].at[q, pl.ds(c * h, h)]
                pltpu.make_async_remote_copy(
                    src_ref=region, dst_ref=region, send_sem=send_sems.at[k, r], recv_sem=recv_sems.at[k, r],
                    device_id=(qx, qy, c), device_id_type=MESH).wait_recv()
                cp = pltpu.make_async_remote_copy(
                    src_ref=region, dst_ref=region, send_sem=send_sems.at[k, 3 + r],
                    recv_sem=recv_sems.at[k, 3 + r], device_id=sib, device_id_type=MESH)
                cp.start()
                sends.append(cp)
        for k in range(n):
            h = ins[k].shape[0] // 2
            for r, (qx, qy) in enumerate(chips):
                q = 2 * qx + qy
                region = outs[k].at[q, pl.ds((1 - c) * h, h)]
                pltpu.make_async_remote_copy(
                    src_ref=region, dst_ref=region, send_sem=send_sems.at[k, 3 + r],
                    recv_sem=recv_sems.at[k, 3 + r], device_id=sib, device_id_type=MESH).wait_recv()
        for cp in sends:
            cp.wait_send()
        for cp in locs:
            cp.wait()

    return pl.pallas_call(
        body, name=name,
        in_specs=[_ANY] * n, out_specs=[_ANY] * n,
        out_shape=[jax.ShapeDtypeStruct((NCHIP,) + s.shape, s.dtype) for s in shards],
        scratch_shapes=[pltpu.SemaphoreType.DMA((n, 6)), pltpu.SemaphoreType.DMA((n, 6)),
                        pltpu.SemaphoreType.DMA((n,))],
    )(*shards)


def exchange_halves(name, parts):
    n = len(parts)

    def body(*refs):
        ins = refs[:n]
        outs = refs[n:2 * n]
        send_sems, recv_sems = refs[2 * n:]
        x, y, c, _ = _place()
        sib = (x, y, 1 - c)
        cps = []
        for k in range(n):
            h = ins[k].shape[1] // 2
            cp = pltpu.make_async_remote_copy(
                src_ref=ins[k].at[:, pl.ds((1 - c) * h, h)], dst_ref=outs[k],
                send_sem=send_sems.at[k], recv_sem=recv_sems.at[k], device_id=sib, device_id_type=MESH)
            cp.start()
            cps.append(cp)
        for cp in cps:
            cp.wait()

    return pl.pallas_call(
        body, name=name, in_specs=[_ANY] * n, out_specs=[_ANY] * n,
        out_shape=[jax.ShapeDtypeStruct((p.shape[0], p.shape[1] // 2) + p.shape[2:], p.dtype) for p in parts],
        scratch_shapes=[pltpu.SemaphoreType.DMA((n,)), pltpu.SemaphoreType.DMA((n,))],
    )(*parts)


def exchange_chips(name, sums):
    n = len(sums)

    def body(*refs):
        ins = refs[:n]
        outs = refs[n:2 * n]
        send_sems, recv_sems, loc_sems = refs[2 * n:]
        x, y, c, chips = _place()
        me = 2 * x + y
        cps = []
        locs = []
        for k in range(n):
            cp = pltpu.make_async_copy(ins[k].at[me], outs[k].at[me], loc_sems.at[k])
            cp.start()
            locs.append(cp)
            for r, (qx, qy) in enumerate(chips):
                q = 2 * qx + qy
                cp = pltpu.make_async_remote_copy(
                    src_ref=ins[k].at[q], dst_ref=outs[k].at[me],
                    send_sem=send_sems.at[k, r], recv_sem=recv_sems.at[k, r],
                    device_id=(qx, qy, c), device_id_type=MESH)
                cp.start()
                cps.append((cp, k, r, q))
        for cp, k, r, q in cps:
            cp.wait_send()
            pltpu.make_async_remote_copy(
                src_ref=outs[k].at[q], dst_ref=outs[k].at[q], send_sem=send_sems.at[k, r],
                recv_sem=recv_sems.at[k, r], device_id=(x, y, c), device_id_type=MESH).wait_recv()
        for cp in locs:
            cp.wait()

    return pl.pallas_call(
        body, name=name, in_specs=[_ANY] * n, out_specs=[_ANY] * n,
        out_shape=[jax.ShapeDtypeStruct(s.shape, s.dtype) for s in sums],
        scratch_shapes=[pltpu.SemaphoreType.DMA((n, 3)), pltpu.SemaphoreType.DMA((n, 3)),
                        pltpu.SemaphoreType.DMA((n,))],
    )(*sums)


def join_halves(name, halves):
    n = len(halves)

    def body(*refs):
        ins = refs[:n]
        outs = refs[n:2 * n]
        send_sems, recv_sems, loc_sems = refs[2 * n:]
        x, y, c, _ = _place()
        sib = (x, y, 1 - c)
        cps = []
        locs = []
        for k in range(n):
            h = ins[k].shape[0]
            cp = pltpu.make_async_copy(ins[k], outs[k].at[pl.ds(c * h, h)], loc_sems.at[k])
            cp.start()
            locs.append(cp)
            cp = pltpu.make_async_remote_copy(
                src_ref=ins[k], dst_ref=outs[k].at[pl.ds(c * h, h)],
                send_sem=send_sems.at[k], recv_sem=recv_sems.at[k], device_id=sib, device_id_type=MESH)
            cp.start()
            cps.append((cp, k, h))
        for cp, k, h in cps:
            cp.wait_send()
            region = outs[k].at[pl.ds((1 - c) * h, h)]
            pltpu.make_async_remote_copy(
                src_ref=region, dst_ref=region, send_sem=send_sems.at[k], recv_sem=recv_sems.at[k],
                device_id=sib, device_id_type=MESH).wait_recv()
        for cp in locs:
            cp.wait()

    return pl.pallas_call(
        body, name=name, in_specs=[_ANY] * n, out_specs=[_ANY] * n,
        out_shape=[jax.ShapeDtypeStruct((2 * s.shape[0],) + s.shape[1:], s.dtype) for s in halves],
        scratch_shapes=[pltpu.SemaphoreType.DMA((n,)), pltpu.SemaphoreType.DMA((n,)),
                        pltpu.SemaphoreType.DMA((n,))],
    )(*halves)


def all_reduce_small(name, buf):
    R, L = buf.shape
    h = R // 2

    def body(x_ref, o_ref, sib_ref, chip_ref, send_sems, recv_sems):
        x, y, c, chips = _place()
        me = 2 * x + y
        sib = (x, y, 1 - c)
        first = pltpu.make_async_remote_copy(src_ref=x_ref, dst_ref=sib_ref, send_sem=send_sems.at[0],
                                             recv_sem=recv_sems.at[0], device_id=sib, device_id_type=MESH)
        first.start()
        first.wait()
        mine = pl.ds(pl.multiple_of(c * h, 8), h)
        other = pl.ds(pl.multiple_of((1 - c) * h, 8), h)
        chip_ref[me] = x_ref[mine, :] + sib_ref[mine, :]
        cps = []
        for r, (qx, qy) in enumerate(chips):
            cp = pltpu.make_async_remote_copy(
                src_ref=chip_ref.at[me], dst_ref=chip_ref.at[me], send_sem=send_sems.at[1 + r],
                recv_sem=recv_sems.at[1 + r], device_id=(qx, qy, c), device_id_type=MESH)
            cp.start()
            cps.append(cp)
        for r, (qx, qy) in enumerate(chips):
            q = 2 * qx + qy
            pltpu.make_async_remote_copy(
                src_ref=chip_ref.at[q], dst_ref=chip_ref.at[q], send_sem=send_sems.at[1 + r],
                recv_sem=recv_sems.at[1 + r], device_id=(qx, qy, c), device_id_type=MESH).wait_recv()
        for cp in cps:
            cp.wait_send()
        o_ref[mine, :] = ((chip_ref[0] + chip_ref[1]) + chip_ref[2]) + chip_ref[3]
        last = pltpu.make_async_remote_copy(src_ref=o_ref.at[mine], dst_ref=o_ref.at[mine],
                                            send_sem=send_sems.at[4], recv_sem=recv_sems.at[4],
                                            device_id=sib, device_id_type=MESH)
        last.start()
        last.wait_send()
        pltpu.make_async_remote_copy(src_ref=o_ref.at[other], dst_ref=o_ref.at[other],
                                     send_sem=send_sems.at[4], recv_sem=recv_sems.at[4],
                                     device_id=sib, device_id_type=MESH).wait_recv()

    vm = pl.BlockSpec(memory_space=pltpu.VMEM)
    return pl.pallas_call(
        body, name=name, in_specs=[vm], out_specs=vm,
        out_shape=jax.ShapeDtypeStruct((R, L), F32),
        scratch_shapes=[pltpu.VMEM((R, L), F32), pltpu.VMEM((NCHIP, h, L), F32),
                        pltpu.SemaphoreType.DMA((5,)), pltpu.SemaphoreType.DMA((5,))],
        compiler_params=pltpu.CompilerParams(vmem_limit_bytes=VMEM_LIMIT),
    )(buf)


def reduce_scatter(tag, parts, c_idx):
    shapes = [p.shape for p in parts]
    parts2 = [p.reshape(p.shape[0], -1, p.shape[-1]) for p in parts]
    recv = exchange_halves(tag + "_xh", parts2)
    sums = [add_halves(f"{tag}_add{k}", p, r, c_idx) for k, (p, r) in enumerate(zip(parts2, recv))]
    slabs = exchange_chips(tag + "_xc", sums)
    halves = [sum_chips(f"{tag}_sum{k}", u) for k, u in enumerate(slabs)]
    full = join_halves(tag + "_jh", halves)
    return [f.reshape(s[1:]) for f, s in zip(full, shapes)]


def _ffn_fwd(tag, h, gain, wg3, wu3, wd):
    n, rstd = rms_fwd(tag + "_norm", h, gain)

    def epi(accs, extras):
        g, u = accs
        return g, u, _silu(g) * u

    g, u, a = mm_nn(tag + "_up", [n], [wg3, wu3], tm=1024, tn=256, b3=True, out_dtypes=(CDT, CDT, CDT),
                    epi=epi, pair_to_acc=[0, 1])

    def epi_down(accs, extras):
        return (extras[0] + 0.5 * accs[0],)

    (h_out,) = mm_nn(tag + "_down", [a], [wd], tm=1024, tn=512, tk=1024, extras=[(h, "mn", 0)], epi=epi_down)
    return h_out, (h, n, rstd, g, u, a)


def _ffn_bwd(tag, saved, gain, wg3, wu3, wd, dh_out, dfb, cast_scale):
    h, n, rstd, g, u, a = saved

    def epi_act(accs, extras):
        da = accs[0]
        gv = extras[0].astype(F32)
        uv = extras[1].astype(F32)
        s = jax.nn.sigmoid(gv)
        return da * uv * (s * (1.0 + gv * (1.0 - s))), da * (gv * s)

    dg, du = mm_nt(tag + "_bact", [dfb], [wd], tm=1024, tn=256, extras=[(g, "mn", 0), (u, "mn", 0)],
                   out_dtypes=(CDT, CDT), epi=epi_act)
    (dwd,) = mm_tn(tag + "_dwd", [a], [dfb], ti=512, tj=512)
    dwg, dwu = mm_tn(tag + "_dwgu", [n], [dg, du], ti=512, tj=256, o3=NCHIP)
    (dn,) = mm_nt(tag + "_dn", [dg, du], [wg3, wu3], tm=1024, tn=512, tk=1024, b3=True)
    dh_in, dcast, dgain = rms_bwd(tag + "_bnorm", dn, h, rstd, gain, dh_out, cast_scale)
    return dh_in, dcast, dgain, dwg, dwu, dwd


def _blockdiag(m, gpc):
    G, a, b = m.shape
    nc = G // gpc
    mask = jnp.eye(gpc, dtype=m.dtype)[None, :, None, :, None]
    out = m.reshape(nc, gpc, a, 1, b) * mask
    return out.reshape(nc, gpc * a, gpc * b)


def _diag_blocks(o, gpc, a, b):
    nc = o.shape[0]
    o5 = o.reshape(nc, gpc, a, gpc, b)
    mask = jnp.eye(gpc, dtype=o.dtype)[None, :, None, :, None]
    return jnp.sum(o5 * mask, axis=3).reshape(nc * gpc, a, b)


def kernel(x, ffn1_norm, ffn1_gate, ffn1_up, ffn1_down, mix_norm, w_in, w_pool, pool_scale, lam_re, lam_im, log_dt, b_re, b_im, c_re, c_im, d_skip, w_glu, b_glu, pool_out_norm, ssm_out_norm, w_out, ffn2_norm, ffn2_gate, ffn2_up, ffn2_down, final_norm, loss_target, m_ffn1_norm, m_ffn1_gate, m_ffn1_up, m_ffn1_down, m_mix_norm, m_w_in, m_w_pool, m_pool_scale, m_lam_re, m_lam_im, m_log_dt, m_b_re, m_b_im, m_c_re, m_c_im, m_d_skip, m_w_glu, m_b_glu, m_pool_out_norm, m_ssm_out_norm, m_w_out, m_ffn2_norm, m_ffn2_gate, m_ffn2_up, m_ffn2_down, m_final_norm, v_ffn1_norm, v_ffn1_gate, v_ffn1_up, v_ffn1_down, v_mix_norm, v_w_in, v_w_pool, v_pool_scale, v_lam_re, v_lam_im, v_log_dt, v_b_re, v_b_im, v_c_re, v_c_im, v_d_skip, v_w_glu, v_b_glu, v_pool_out_norm, v_ssm_out_norm, v_w_out, v_ffn2_norm, v_ffn2_gate, v_ffn2_up, v_ffn2_down, v_final_norm):
    weights = dict(ffn1_norm=ffn1_norm, ffn1_gate=ffn1_gate, ffn1_up=ffn1_up, ffn1_down=ffn1_down, mix_norm=mix_norm, w_in=w_in, w_pool=w_pool, pool_scale=pool_scale, lam_re=lam_re, lam_im=lam_im, log_dt=log_dt, b_re=b_re, b_im=b_im, c_re=c_re, c_im=c_im, d_skip=d_skip, w_glu=w_glu, b_glu=b_glu, pool_out_norm=pool_out_norm, ssm_out_norm=ssm_out_norm, w_out=w_out, ffn2_norm=ffn2_norm, ffn2_gate=ffn2_gate, ffn2_up=ffn2_up, ffn2_down=ffn2_down, final_norm=final_norm)
    moms = dict(ffn1_norm=(m_ffn1_norm, v_ffn1_norm), ffn1_gate=(m_ffn1_gate, v_ffn1_gate), ffn1_up=(m_ffn1_up, v_ffn1_up), ffn1_down=(m_ffn1_down, v_ffn1_down), mix_norm=(m_mix_norm, v_mix_norm), w_in=(m_w_in, v_w_in), w_pool=(m_w_pool, v_w_pool), pool_scale=(m_pool_scale, v_pool_scale), lam_re=(m_lam_re, v_lam_re), lam_im=(m_lam_im, v_lam_im), log_dt=(m_log_dt, v_log_dt), b_re=(m_b_re, v_b_re), b_im=(m_b_im, v_b_im), c_re=(m_c_re, v_c_re), c_im=(m_c_im, v_c_im), d_skip=(m_d_skip, v_d_skip), w_glu=(m_w_glu, v_w_glu), b_glu=(m_b_glu, v_b_glu), pool_out_norm=(m_pool_out_norm, v_pool_out_norm), ssm_out_norm=(m_ssm_out_norm, v_ssm_out_norm), w_out=(m_w_out, v_w_out), ffn2_norm=(m_ffn2_norm, v_ffn2_norm), ffn2_gate=(m_ffn2_gate, v_ffn2_gate), ffn2_up=(m_ffn2_up, v_ffn2_up), ffn2_down=(m_ffn2_down, v_ffn2_down), final_norm=(m_final_norm, v_final_norm))
    names = list(weights)

    n_seq, seq, D = x.shape
    N = n_seq * seq
    Fs = ffn1_gate.shape[1]
    Fps = -(-Fs // FF_ALIGN) * FF_ALIGN
    G, _, Cg = w_pool.shape
    PW = G * Cg
    SW = d_skip.shape[0]
    SG, P = lam_re.shape
    H = SSM_H
    gpc = LANES // H
    NC = SG // gpc
    W = gpc * P
    GP = SG * P
    c_idx = lax.axis_index("c").astype(jnp.int32).reshape(1)

    row = lambda v: v.reshape(1, -1)
    xf = x.reshape(N, D)
    tgt = loss_target.reshape(N, D)

    pad_c = lambda w: jnp.pad(w.astype(CDT), ((0, 0), (0, Fps - Fs)))
    pad_r = lambda w: jnp.pad(w.astype(CDT), ((0, Fps - Fs), (0, 0)))
    wg1, wu1, wd1 = gather_shards("gather_ffn1", [pad_c(ffn1_gate), pad_c(ffn1_up), pad_r(ffn1_down)])
    wd1 = wd1.reshape(NCHIP * Fps, D)
    w_in_f, w_out_f, w_glu_f, w_pool_f = gather_shards(
        "gather_mix", [w_in.astype(CDT), w_out.astype(CDT), w_glu.astype(CDT), w_pool.astype(CDT)])
    w_in_f = w_in_f.reshape(D, PW + SW)
    w_out_f = w_out_f.reshape(PW + SW, D)
    w_glu_f = w_glu_f.reshape(SW, SW)
    w_pool_f = jnp.swapaxes(w_pool_f, 0, 1).reshape(G, Cg, Cg)
    wg2, wu2, wd2 = gather_shards("gather_ffn2", [pad_c(ffn2_gate), pad_c(ffn2_up), pad_r(ffn2_down)])
    wd2 = wd2.reshape(NCHIP * Fps, D)

    h1, saved1 = _ffn_fwd("ffn1", xf, row(ffn1_norm), wg1, wu1, wd1)
    n2, rstd2 = rms_fwd("mix_norm", h1, row(mix_norm))
    (z,) = mm_nn("mix_in", [n2], [w_in_f], tm=1024, tn=256)

    d_pool, y_pool = pool_fwd("pool_fwd", z, w_pool_f, row(pool_scale), n_seq, seq)

    col = lambda v: v.reshape(GP, 1)
    lr_c, li_c = col(lam_re), col(lam_im)
    ldt_c = col(jnp.broadcast_to(log_dt[:, None], (SG, P)))
    br_c, bi_c = b_re.reshape(GP, H), b_im.reshape(GP, H)
    ar, ai, bbr, bbi = ssm_params_fwd("ssm_params", lr_c, li_c, ldt_c, br_c, bi_c)
    ar2, ai2 = ar.reshape(GP // LANES, LANES), ai.reshape(GP // LANES, LANES)
    bbr_t = jnp.swapaxes(bbr.reshape(SG, P, H), 1, 2)
    bbi_t = jnp.swapaxes(bbi.reshape(SG, P, H), 1, 2)
    b_cat = jnp.concatenate([_blockdiag(bbr_t, gpc), _blockdiag(bbi_t, gpc)], axis=2).astype(CDT)
    b_cat_t = jnp.swapaxes(b_cat, 1, 2)
    c_cat_t = jnp.concatenate([_blockdiag(c_re, gpc), _blockdiag(-c_im, gpc)], axis=2).astype(CDT)
    c_cat = jnp.swapaxes(c_cat_t, 1, 2)

    u_off = PW // LANES
    bur, bui = ssm_expand("ssm_bu", z, u_off, b_cat, 512)
    v3 = lambda a: a.reshape(N, GP // LANES, LANES)
    xs_r, xs_i = ssm_scan_fwd("ssm_scan", v3(bur), v3(bui), ar2, ai2, n_seq, seq, 64)
    xs_r, xs_i = xs_r.reshape(N, GP), xs_i.reshape(N, GP)

    def epi_y(acc, extras):
        y = acc + extras[1] * extras[0]
        return y, _gelu(y)

    y_pre, yg = ssm_contract("ssm_y", xs_r, xs_i, c_cat, [(z, "mn", u_off), (row(d_skip), "n", 0)], epi_y,
                             (F32, CDT), 512)

    def epi_glu(accs, extras):
        q = accs[0] + extras[1]
        return q, _gelu(extras[0]) * jax.nn.sigmoid(q)

    q_glu, y_ssm = mm_nn("glu", [yg], [w_glu_f], tm=1024, tn=256, extras=[(y_pre, "mn", 0), (row(b_glu), "n", 0)],
                         out_dtypes=(F32, F32), epi=epi_glu)
    merged, rstd_p, rstd_s = merge_norm_fwd("merge_norm", y_pool, y_ssm, row(pool_out_norm), row(ssm_out_norm))

    def epi_res(accs, extras):
        return (extras[0] + accs[0],)

    (h2,) = mm_nn("mix_out", [merged], [w_out_f], tm=1024, tn=256, extras=[(h1, "mn", 0)], epi=epi_res)
    h3, saved2 = _ffn_fwd("ffn2", h2, row(ffn2_norm), wg2, wu2, wd2)

    loss_acc, dh3, dfb3, g_final = loss_head("loss_head", h3, row(final_norm), tgt)
    dh2, dh2c, g_ffn2n, dwg2, dwu2, dwd2 = _ffn_bwd("ffn2", saved2, row(ffn2_norm), wg2, wu2, wd2, dh3, dfb3, 1.0)

    (dmerged,) = mm_nt("mix_out_bx", [dh2c], [w_out_f], tm=1024, tn=256)
    (dw_out,) = mm_tn("mix_out_bw", [merged], [dh2c], ti=512, tj=512)
    dy_pool, dy_ssm, g_pon, g_son = merge_norm_bwd("merge_norm_b", dmerged, y_pool, y_ssm, rstd_p, rstd_s,
                                                   row(pool_out_norm), row(ssm_out_norm))
    dq, dyg1, g_bglu = glu_bwd_pre("glu_b_pre", dy_ssm, y_pre, q_glu)

    def epi_dyg(accs, extras):
        return ((accs[0] + extras[0]) * _gelu_grad(extras[1]),)

    (dy_pre,) = mm_nt("glu_bx", [dq], [w_glu_f], tm=1024, tn=256, extras=[(dyg1, "mn", 0), (y_pre, "mn", 0)],
                      epi=epi_dyg)
    (dw_glu,) = mm_tn("glu_bw", [yg], [dq], ti=512, tj=512)
    g_dskip = colsum_prod("dskip", dy_pre, z, PW // SW, SW)
    gxr, gxi = ssm_expand("ssm_by", dy_pre, 0, c_cat_t, 512)
    dc_r, dc_i = ssm_outer("ssm_dc", dy_pre, 0, xs_r, xs_i, NC, W, 512)
    lam_r, lam_i, dar, dai = ssm_scan_bwd("ssm_scan_b", v3(gxr), v3(gxi), v3(xs_r), v3(xs_i), ar2, ai2,
                                          n_seq, seq, 64)
    lam_r, lam_i = lam_r.reshape(N, GP), lam_i.reshape(N, GP)
    db_r, db_i = ssm_outer("ssm_db", z, u_off, lam_r, lam_i, NC, W, 512)

    def epi_du(acc, extras):
        return (acc + extras[0] * extras[1],)

    (du_ssm,) = ssm_contract("ssm_bu_b", lam_r, lam_i, b_cat_t, [(dy_pre, "mn", 0), (row(d_skip), "n", 0)],
                             epi_du, (CDT,), 512)
    g_c_re = _diag_blocks(dc_r, gpc, H, P)
    g_c_im = -_diag_blocks(dc_i, gpc, H, P)
    dbbr = jnp.swapaxes(_diag_blocks(db_r, gpc, H, P), 1, 2).reshape(GP, H)
    dbbi = jnp.swapaxes(_diag_blocks(db_i, gpc, H, P), 1, 2).reshape(GP, H)
    dlr, dli, dldt, dbr, dbi = ssm_params_bwd("ssm_params_b", lr_c, li_c, ldt_c, br_c, bi_c,
                                              dar.reshape(GP, 1), dai.reshape(GP, 1), dbbr, dbbi)
    g_lam_re, g_lam_im = dlr.reshape(SG, P), dli.reshape(SG, P)
    g_log_dt = jnp.sum(dldt.reshape(SG, P), axis=1)
    g_b_re, g_b_im = dbr.reshape(SG, P, H), dbi.reshape(SG, P, H)
    dz_pool, dw_pool, g_pscale = pool_bwd("pool_bwd", dy_pool, d_pool, w_pool_f, row(pool_scale), n_seq, seq)
    dz = jnp.concatenate([dz_pool, du_ssm], axis=1)
    (dn2,) = mm_nt("mix_in_bx", [dz], [w_in_f], tm=1024, tn=256)
    (dw_in,) = mm_tn("mix_in_bw", [n2], [dz], ti=512, tj=512)
    dh1, dh1c, g_mixn = rms_bwd("mix_norm_b", dn2, h1, rstd2, row(mix_norm), dh2, 0.5)
    grad_x, _, g_ffn1n, dwg1, dwu1, dwd1 = _ffn_bwd("ffn1", saved1, row(ffn1_norm), wg1, wu1, wd1, dh1, dh1c, 1.0)

    dwp4 = jnp.swapaxes(dw_pool.astype(CDT).reshape(G, NCHIP, Cg // NCHIP, Cg), 0, 1)
    big = reduce_scatter("rs", [
        dwg1, dwu1, dwd1.reshape(NCHIP, Fps, D), dw_in.reshape(NCHIP, D // NCHIP, PW + SW), dwp4,
        dw_glu.reshape(NCHIP, SW // NCHIP, SW), dw_out.reshape(NCHIP, (PW + SW) // NCHIP, D),
        dwg2, dwu2, dwd2.reshape(NCHIP, Fps, D)], c_idx)
    grads = dict(ffn1_gate=big[0][:, :Fs], ffn1_up=big[1][:, :Fs], ffn1_down=big[2][:Fs], w_in=big[3],
                 w_pool=big[4], w_glu=big[5], w_out=big[6], ffn2_gate=big[7][:, :Fs], ffn2_up=big[8][:, :Fs],
                 ffn2_down=big[9][:Fs])

    small = dict(ffn1_norm=g_ffn1n, mix_norm=g_mixn, pool_scale=g_pscale, lam_re=g_lam_re, lam_im=g_lam_im,
                 log_dt=g_log_dt, b_re=g_b_re, b_im=g_b_im, c_re=g_c_re, c_im=g_c_im, d_skip=g_dskip,
                 b_glu=g_bglu, pool_out_norm=g_pon, ssm_out_norm=g_son, ffn2_norm=g_ffn2n, final_norm=g_final)
    pieces = [jnp.pad(small[k].reshape(-1), (0, (-small[k].size) % LANES)) for k in small]
    pieces.append(loss_acc.reshape(-1))
    flat = jnp.concatenate(pieces)
    rows = -(-flat.size // (16 * LANES)) * 16
    flat = jnp.pad(flat, (0, rows * LANES - flat.size)).reshape(rows, LANES)
    red = all_reduce_small("all_reduce_small", flat).reshape(-1)
    off = 0
    for k in small:
        size = small[k].size
        grads[k] = red[off:off + size].reshape(weights[k].shape)
        off += size + (-size) % LANES
    loss = (0.5 / D) * red[off]

    deltas, new_m, new_v = {}, {}, {}
    for k in names:
        w = weights[k]
        if w.ndim >= 2 and w.shape[-1] >= LANES:
            shape2 = (w.size // w.shape[-1], w.shape[-1])
        elif w.size % LANES == 0:
            shape2 = (w.size // LANES, LANES)
        else:
            shape2 = (1, w.size)
        m, v = moms[k]
        d_, m_, v_ = adamw("adamw_" + k, w.reshape(shape2), grads[k].reshape(shape2), m.reshape(shape2),
                           v.reshape(shape2))
        deltas[k], new_m[k], new_v[k] = d_.reshape(w.shape), m_.reshape(w.shape), v_.reshape(w.shape)

    return (loss, grad_x.reshape(x.shape), *[grads[k] for k in names], *[deltas[k] for k in names],
            *[new_m[k] for k in names], *[new_v[k] for k in names])
```

```python
import functools
import math

import jax
import jax.numpy as jnp
from jax import lax
from jax.experimental import pallas as pl
from jax.experimental.pallas import tpu as pltpu

F32 = jnp.float32
CDT = jnp.bfloat16
NORM_EPS = 1e-6
POOL_WINDOWS = (2, 4, 8, 16)
POOL_HALO = 16
SSM_H = 16
SSM_ROWS = 2048
LANES = 128
FF_ALIGN = 256
NCHIP = 4
VMEM_LIMIT = 48 * 1024 * 1024
ADAM_LR = 0.001
ADAM_B1 = 0.9
ADAM_B2 = 0.999
ADAM_EPS = 1e-08
ADAM_WD = 0.01
ADAM_STEP = 10
MESH = pl.DeviceIdType.MESH


def _cp(n_grid):
    return pltpu.CompilerParams(dimension_semantics=("arbitrary",) * n_grid, vmem_limit_bytes=VMEM_LIMIT)


def _tile(n, pref, align=8):
    if n <= pref:
        return n
    t = (pref // align) * align
    while t >= align:
        if n % t == 0:
            return t
        t -= align
    return n


def _silu(x):
    return x * jax.nn.sigmoid(x)


_GELU_C = math.sqrt(2.0 / math.pi)


def _gelu(x):
    return x * (0.5 * (1.0 + jnp.tanh(_GELU_C * (x + 0.044715 * (x * x * x)))))


def _gelu_grad(x):
    t = jnp.tanh(_GELU_C * (x + 0.044715 * (x * x * x)))
    return 0.5 * (1.0 + t) + 0.5 * x * (1.0 - t * t) * (_GELU_C * (1.0 + 3.0 * 0.044715 * x * x))


def _mm(name, dn, grid, a_list, a_specs, b_list, b_specs, e_list, e_specs, out_shapes, out_specs,
        acc_shape, pair_to_acc, epi):
    na, nb, ne, no = len(a_list), len(b_list), len(e_list), len(out_shapes)
    assert na in (1, nb)
    n_acc = max(pair_to_acc) + 1
    nk = grid[2]

    def body(*refs):
        a_refs = refs[:na]
        b_refs = refs[na:na + nb]
        e_refs = refs[na + nb:na + nb + ne]
        o_refs = refs[na + nb + ne:na + nb + ne + no]
        acc_refs = refs[na + nb + ne + no:]
        prods = [None] * n_acc
        a_vals = [a_ref[...].astype(CDT) for a_ref in a_refs]
        for p in range(nb):
            d = lax.dot_general(a_vals[p if na > 1 else 0], b_refs[p][...].astype(CDT), dn,
                                preferred_element_type=F32)
            q = pair_to_acc[p]
            prods[q] = d if prods[q] is None else prods[q] + d

        def finish(accs):
            outs = epi(accs, [e[...] for e in e_refs])
            for o_ref, o in zip(o_refs, outs):
                o_ref[...] = o.astype(o_ref.dtype)

        if nk == 1:
            finish(prods)
        else:
            k = pl.program_id(2)

            @pl.when(k == 0)
            def _():
                for acc, p in zip(acc_refs, prods):
                    acc[...] = p

            @pl.when(k > 0)
            def _():
                for acc, p in zip(acc_refs, prods):
                    acc[...] += p

            @pl.when(k == nk - 1)
            def _():
                finish([acc[...] for acc in acc_refs])

    scratch = [pltpu.VMEM(acc_shape, F32) for _ in range(n_acc)] if nk > 1 else []
    outs = pl.pallas_call(
        body, name=name, grid=grid,
        in_specs=list(a_specs) + list(b_specs) + list(e_specs),
        out_specs=list(out_specs), out_shape=list(out_shapes),
        scratch_shapes=scratch, compiler_params=_cp(3),
    )(*a_list, *b_list, *e_list)
    return outs


def _extra_specs(extras, tm, tn):
    arrs, specs = [], []
    for arr, kind, off in extras:
        arrs.append(arr)
        if kind == "mn":
            specs.append(pl.BlockSpec((tm, tn), lambda i, j, k, off=off: (i, j + off)))
        elif kind == "n":
            specs.append(pl.BlockSpec((1, tn), lambda i, j, k, off=off: (0, j + off)))
        else:
            raise ValueError(kind)
    return arrs, specs


def _first(accs, extras):
    return (accs[0],)


def mm_nn(name, a_list, b_list, *, tm, tn, tk=None, b3=False, extras=(), out_dtypes=(F32,), epi=_first,
          pair_to_acc=None):
    M, K = a_list[0].shape
    if b3:
        s4, _, ns = b_list[0].shape
        N = s4 * ns
        tn = _tile(ns, tn, LANES)
        nps = ns // tn
    else:
        N = b_list[0].shape[1]
        tn = _tile(N, tn, LANES)
    tm = _tile(M, tm, 16)
    tk = _tile(K, tk or K, LANES)
    grid = (M // tm, N // tn, K // tk)
    a_specs = [pl.BlockSpec((tm, tk), lambda i, j, k: (i, k)) for _ in a_list]
    if b3:
        b_specs = [pl.BlockSpec((None, tk, tn), lambda i, j, k: (j // nps, k, j % nps)) for _ in b_list]
    else:
        b_specs = [pl.BlockSpec((tk, tn), lambda i, j, k: (k, j)) for _ in b_list]
    e_list, e_specs = _extra_specs(extras, tm, tn)
    out_shapes = [jax.ShapeDtypeStruct((M, N), dt) for dt in out_dtypes]
    out_specs = [pl.BlockSpec((tm, tn), lambda i, j, k: (i, j)) for _ in out_dtypes]
    pair_to_acc = pair_to_acc or [0] * len(b_list)
    dn = (((1,), (0,)), ((), ()))
    return _mm(name, dn, grid, a_list, a_specs, b_list, b_specs, e_list, e_specs, out_shapes, out_specs,
               (tm, tn), pair_to_acc, epi)


def mm_nt(name, a_list, b_list, *, tm, tn, tk=None, b3=False, extras=(), out_dtypes=(F32,), epi=_first,
          pair_to_acc=None):
    M, K = a_list[0].shape
    if b3:
        s4, N, ks = b_list[0].shape
        tk = _tile(ks, tk or ks, LANES)
        kps = ks // tk
    else:
        N = b_list[0].shape[0]
        tk = _tile(K, tk or K, LANES)
    tm = _tile(M, tm, 16)
    tn = _tile(N, tn, LANES)
    grid = (M // tm, N // tn, K // tk)
    a_specs = [pl.BlockSpec((tm, tk), lambda i, j, k: (i, k)) for _ in a_list]
    if b3:
        b_specs = [pl.BlockSpec((None, tn, tk), lambda i, j, k: (k // kps, j, k % kps)) for _ in b_list]
    else:
        b_specs = [pl.BlockSpec((tn, tk), lambda i, j, k: (j, k)) for _ in b_list]
    e_list, e_specs = _extra_specs(extras, tm, tn)
    out_shapes = [jax.ShapeDtypeStruct((M, N), dt) for dt in out_dtypes]
    out_specs = [pl.BlockSpec((tm, tn), lambda i, j, k: (i, j)) for _ in out_dtypes]
    pair_to_acc = pair_to_acc or [0] * len(b_list)
    dn = (((1,), (1,)), ((), ()))
    return _mm(name, dn, grid, a_list, a_specs, b_list, b_specs, e_list, e_specs, out_shapes, out_specs,
               (tm, tn), pair_to_acc, epi)


def mm_tn(name, a_list, b_list, *, ti, tj, o3=0, out_dtype=CDT):
    T, I = a_list[0].shape
    J = b_list[0].shape[1]
    ti = _tile(I, ti, LANES)
    if o3:
        js = J // o3
        tj = _tile(js, tj, LANES)
        jps = js // tj
        out_shapes = [jax.ShapeDtypeStruct((o3, I, js), out_dtype) for _ in b_list]
        out_specs = [pl.BlockSpec((None, ti, tj), lambda i, j, k: (j // jps, i, j % jps)) for _ in b_list]
    else:
        tj = _tile(J, tj, LANES)
        out_shapes = [jax.ShapeDtypeStruct((I, J), out_dtype) for _ in b_list]
        out_specs = [pl.BlockSpec((ti, tj), lambda i, j, k: (i, j)) for _ in b_list]
    grid = (I // ti, J // tj, 1)
    a_specs = [pl.BlockSpec((T, ti), lambda i, j, k: (0, i)) for _ in a_list]
    b_specs = [pl.BlockSpec((T, tj), lambda i, j, k: (0, j)) for _ in b_list]
    dn = (((0,), (0,)), ((), ()))
    return _mm(name, dn, grid, a_list, a_specs, b_list, b_specs, [], [], out_shapes, out_specs,
               (ti, tj), list(range(len(b_list))), lambda accs, extras: tuple(accs))


def rms_fwd(name, x, gain):
    N, D = x.shape
    tm = _tile(N, 128)

    def body(x_ref, g_ref, n_ref, r_ref):
        xf = x_ref[...]
        r = lax.rsqrt(jnp.mean(xf * xf, axis=-1, keepdims=True) + NORM_EPS)
        n_ref[...] = ((xf * r) * g_ref[...]).astype(n_ref.dtype)
        r_ref[...] = r

    return pl.pallas_call(
        body, name=name, grid=(N // tm,),
        in_specs=[pl.BlockSpec((tm, D), lambda i: (i, 0)), pl.BlockSpec((1, D), lambda i: (0, 0))],
        out_specs=[pl.BlockSpec((tm, D), lambda i: (i, 0)), pl.BlockSpec((tm, 1), lambda i: (i, 0))],
        out_shape=[jax.ShapeDtypeStruct((N, D), CDT), jax.ShapeDtypeStruct((N, 1), F32)],
        compiler_params=_cp(1),
    )(x, gain)


def _rms_bwd_math(dn, x, r, g):
    xhat = x * r
    dgain = jnp.sum(dn * xhat, axis=0, keepdims=True)
    dxhat = dn * g
    dx = r * (dxhat - xhat * jnp.mean(dxhat * xhat, axis=-1, keepdims=True))
    return dx, dgain


def rms_bwd(name, dn, x, rstd, gain, dres, cast_scale):
    N, D = x.shape
    tm = _tile(N, 128)

    def body(dn_ref, x_ref, r_ref, g_ref, dres_ref, dx_ref, dc_ref, dg_ref):
        dx, dgain = _rms_bwd_math(dn_ref[...], x_ref[...], r_ref[...], g_ref[...])
        tot = dres_ref[...] + dx
        dx_ref[...] = tot
        dc_ref[...] = (cast_scale * tot).astype(dc_ref.dtype)

        @pl.when(pl.program_id(0) == 0)
        def _():
            dg_ref[...] = jnp.zeros_like(dg_ref)

        dg_ref[...] += dgain

    row = pl.BlockSpec((tm, D), lambda i: (i, 0))
    vec = pl.BlockSpec((1, D), lambda i: (0, 0))
    return pl.pallas_call(
        body, name=name, grid=(N // tm,),
        in_specs=[row, row, pl.BlockSpec((tm, 1), lambda i: (i, 0)), vec, row],
        out_specs=[row, row, vec],
        out_shape=[jax.ShapeDtypeStruct((N, D), F32), jax.ShapeDtypeStruct((N, D), CDT),
                   jax.ShapeDtypeStruct((1, D), F32)],
        compiler_params=_cp(1),
    )(dn, x, rstd, gain, dres)


def loss_head(name, h, gain, target):
    N, D = h.shape
    tm = _tile(N, 128)

    def body(h_ref, g_ref, t_ref, l_ref, dh_ref, dc_ref, dg_ref):
        x = h_ref[...]
        g = g_ref[...]
        r = lax.rsqrt(jnp.mean(x * x, axis=-1, keepdims=True) + NORM_EPS)
        err = (x * r) * g - t_ref[...]
        dy = err * (1.0 / D)
        dx, dgain = _rms_bwd_math(dy, x, r, g)
        dh_ref[...] = dx
        dc_ref[...] = (0.5 * dx).astype(dc_ref.dtype)

        @pl.when(pl.program_id(0) == 0)
        def _():
            dg_ref[...] = jnp.zeros_like(dg_ref)
            l_ref[...] = jnp.zeros_like(l_ref)

        dg_ref[...] += dgain
        l_ref[...] += jnp.sum(err * err)

    row = pl.BlockSpec((tm, D), lambda i: (i, 0))
    vec = pl.BlockSpec((1, D), lambda i: (0, 0))
    return pl.pallas_call(
        body, name=name, grid=(N // tm,),
        in_specs=[row, vec, row],
        out_specs=[pl.BlockSpec((1, LANES), lambda i: (0, 0)), row, row, vec],
        out_shape=[jax.ShapeDtypeStruct((1, LANES), F32), jax.ShapeDtypeStruct((N, D), F32),
                   jax.ShapeDtypeStruct((N, D), CDT), jax.ShapeDtypeStruct((1, D), F32)],
        compiler_params=_cp(1),
    )(h, gain, target)


def merge_norm_fwd(name, y_pool, y_ssm, g_pool, g_ssm):
    N, PW = y_pool.shape
    SW = y_ssm.shape[1]
    tm = _tile(N, 128)

    def body(yp_ref, ys_ref, gp_ref, gs_ref, m_ref, rp_ref, rs_ref):
        yp = yp_ref[...]
        ys = ys_ref[...]
        rp = lax.rsqrt(jnp.mean(yp * yp, axis=-1, keepdims=True) + NORM_EPS)
        rs = lax.rsqrt(jnp.mean(ys * ys, axis=-1, keepdims=True) + NORM_EPS)
        m_ref[:, :PW] = ((yp * rp) * gp_ref[...]).astype(m_ref.dtype)
        m_ref[:, PW:] = ((ys * rs) * gs_ref[...]).astype(m_ref.dtype)
        rp_ref[...] = rp
        rs_ref[...] = rs

    return pl.pallas_call(
        body, name=name, grid=(N // tm,),
        in_specs=[pl.BlockSpec((tm, PW), lambda i: (i, 0)), pl.BlockSpec((tm, SW), lambda i: (i, 0)),
                  pl.BlockSpec((1, PW), lambda i: (0, 0)), pl.BlockSpec((1, SW), lambda i: (0, 0))],
        out_specs=[pl.BlockSpec((tm, PW + SW), lambda i: (i, 0)), pl.BlockSpec((tm, 1), lambda i: (i, 0)),
                   pl.BlockSpec((tm, 1), lambda i: (i, 0))],
        out_shape=[jax.ShapeDtypeStruct((N, PW + SW), CDT), jax.ShapeDtypeStruct((N, 1), F32),
                   jax.ShapeDtypeStruct((N, 1), F32)],
        compiler_params=_cp(1),
    )(y_pool, y_ssm, g_pool, g_ssm)


def merge_norm_bwd(name, dmerged, y_pool, y_ssm, r_pool, r_ssm, g_pool, g_ssm):
    N, PW = y_pool.shape
    SW = y_ssm.shape[1]
    tm = _tile(N, 128)

    def body(dm_ref, yp_ref, ys_ref, rp_ref, rs_ref, gp_ref, gs_ref, dyp_ref, dys_ref, dgp_ref, dgs_ref):
        dxp, dgp = _rms_bwd_math(dm_ref[:, :PW], yp_ref[...], rp_ref[...], gp_ref[...])
        dxs, dgs = _rms_bwd_math(dm_ref[:, PW:], ys_ref[...], rs_ref[...], gs_ref[...])
        dyp_ref[...] = dxp
        dys_ref[...] = dxs

        @pl.when(pl.program_id(0) == 0)
        def _():
            dgp_ref[...] = jnp.zeros_like(dgp_ref)
            dgs_ref[...] = jnp.zeros_like(dgs_ref)

        dgp_ref[...] += dgp
        dgs_ref[...] += dgs

    col1 = pl.BlockSpec((tm, 1), lambda i: (i, 0))
    return pl.pallas_call(
        body, name=name, grid=(N // tm,),
        in_specs=[pl.BlockSpec((tm, PW + SW), lambda i: (i, 0)), pl.BlockSpec((tm, PW), lambda i: (i, 0)),
                  pl.BlockSpec((tm, SW), lambda i: (i, 0)), col1, col1,
                  pl.BlockSpec((1, PW), lambda i: (0, 0)), pl.BlockSpec((1, SW), lambda i: (0, 0))],
        out_specs=[pl.BlockSpec((tm, PW), lambda i: (i, 0)), pl.BlockSpec((tm, SW), lambda i: (i, 0)),
                   pl.BlockSpec((1, PW), lambda i: (0, 0)), pl.BlockSpec((1, SW), lambda i: (0, 0))],
        out_shape=[jax.ShapeDtypeStruct((N, PW), F32), jax.ShapeDtypeStruct((N, SW), F32),
                   jax.ShapeDtypeStruct((1, PW), F32), jax.ShapeDtypeStruct((1, SW), F32)],
        compiler_params=_cp(1),
    )(dmerged, y_pool, y_ssm, r_pool, r_ssm, g_pool, g_ssm)


def pool_fwd(name, z, w_pool, scale, n_seq, seq):
    G, Cg, _ = w_pool.shape
    N = n_seq * seq
    PW = G * Cg

    def body(z_ref, w_ref, s_ref, d_ref, y_ref, zs_ref):
        g = pl.program_id(0)
        zv = z_ref[...]
        zs_ref[pl.ds(0, POOL_HALO), :] = jnp.zeros((POOL_HALO, Cg), F32)
        zs_ref[pl.ds(POOL_HALO, seq), :] = zv
        t = lax.broadcasted_iota(jnp.int32, (seq, 1), 0)
        for gi, w in enumerate(POOL_WINDOWS):
            @pl.when(g == gi)
            def _(w=w):
                acc = zv
                for j in range(1, w):
                    acc = acc + zs_ref[pl.ds(POOL_HALO - j, seq), :]
                cnt = jnp.minimum(t + 1, w).astype(F32)
                d = (acc / cnt - zv).astype(d_ref.dtype)
                d_ref[...] = d
                out = jnp.dot(d, w_ref[...], preferred_element_type=F32)
                y_ref[...] = out * s_ref[...]

    blk = pl.BlockSpec((seq, Cg), lambda g, b: (b, g))
    return pl.pallas_call(
        body, name=name, grid=(G, n_seq),
        in_specs=[blk, pl.BlockSpec((None, Cg, Cg), lambda g, b: (g, 0, 0)),
                  pl.BlockSpec((1, Cg), lambda g, b: (0, g))],
        out_specs=[blk, blk],
        out_shape=[jax.ShapeDtypeStruct((N, PW), CDT), jax.ShapeDtypeStruct((N, PW), F32)],
        scratch_shapes=[pltpu.VMEM((seq + POOL_HALO, Cg), F32)],
        compiler_params=_cp(2),
    )(z, w_pool, scale)


def pool_bwd(name, dy_pool, d, w_pool, scale, n_seq, seq):
    G, Cg, _ = w_pool.shape
    N = n_seq * seq
    PW = G * Cg

    def body(dy_ref, d_ref, w_ref, s_ref, dz_ref, dw_ref, ds_ref, es_ref):
        g = pl.program_id(0)
        b = pl.program_id(1)
        dv = d_ref[...]
        wv = w_ref[...]
        dy = dy_ref[...]
        out = jnp.dot(dv, wv, preferred_element_type=F32)
        dout = (dy * s_ref[...]).astype(CDT)
        dw = lax.dot_general(dv, dout, (((0,), (0,)), ((), ())), preferred_element_type=F32)
        dd = lax.dot_general(dout, wv, (((1,), (1,)), ((), ())), preferred_element_type=F32)

        @pl.when(b == 0)
        def _():
            dw_ref[...] = jnp.zeros_like(dw_ref)
            ds_ref[...] = jnp.zeros_like(ds_ref)

        dw_ref[...] += dw
        ds_ref[...] += jnp.sum(out * dy, axis=0, keepdims=True)
        t = lax.broadcasted_iota(jnp.int32, (seq, 1), 0)
        es_ref[pl.ds(seq, POOL_HALO), :] = jnp.zeros((POOL_HALO, Cg), F32)
        for gi, w in enumerate(POOL_WINDOWS):
            @pl.when(g == gi)
            def _(w=w):
                cnt = jnp.minimum(t + 1, w).astype(F32)
                e = dd / cnt
                es_ref[pl.ds(0, seq), :] = e
                acc = e
                for j in range(1, w):
                    acc = acc + es_ref[pl.ds(j, seq), :]
                dz_ref[...] = (acc - dd).astype(dz_ref.dtype)

    blk = pl.BlockSpec((seq, Cg), lambda g, b: (b, g))
    return pl.pallas_call(
        body, name=name, grid=(G, n_seq),
        in_specs=[blk, blk, pl.BlockSpec((None, Cg, Cg), lambda g, b: (g, 0, 0)),
                  pl.BlockSpec((1, Cg), lambda g, b: (0, g))],
        out_specs=[blk, pl.BlockSpec((None, Cg, Cg), lambda g, b: (g, 0, 0)),
                   pl.BlockSpec((1, Cg), lambda g, b: (0, g))],
        out_shape=[jax.ShapeDtypeStruct((N, PW), CDT), jax.ShapeDtypeStruct((G, Cg, Cg), F32),
                   jax.ShapeDtypeStruct((1, PW), F32)],
        scratch_shapes=[pltpu.VMEM((seq + POOL_HALO, Cg), F32)],
        compiler_params=_cp(2),
    )(dy_pool, d, w_pool, scale)


def _ssm_param_math(lr, li, ldt, br, bi):
    dt = jnp.exp(ldt)
    mag = jnp.exp(lr * dt)
    ar = mag * jnp.cos(li * dt)
    ai = mag * jnp.sin(li * dt)
    den = lr * lr + li * li
    xr = ar - 1.0
    cr = (xr * lr + ai * li) / den
    ci = (ai * lr - xr * li) / den
    return ar, ai, cr * br - ci * bi, cr * bi + ci * br


def ssm_params_fwd(name, lr, li, ldt, br, bi):
    GP, H = br.shape

    def body(lr_ref, li_ref, ldt_ref, br_ref, bi_ref, ar_ref, ai_ref, bbr_ref, bbi_ref):
        ar, ai, bbr, bbi = _ssm_param_math(lr_ref[...], li_ref[...], ldt_ref[...], br_ref[...], bi_ref[...])
        ar_ref[...] = ar
        ai_ref[...] = ai
        bbr_ref[...] = bbr
        bbi_ref[...] = bbi

    c1 = jax.ShapeDtypeStruct((GP, 1), F32)
    ch = jax.ShapeDtypeStruct((GP, H), F32)
    tr = _tile(GP, 512)
    b1 = pl.BlockSpec((tr, 1), lambda i: (i, 0))
    bh = pl.BlockSpec((tr, H), lambda i: (i, 0))
    return pl.pallas_call(body, name=name, grid=(GP // tr,), in_specs=[b1, b1, b1, bh, bh],
                          out_specs=[b1, b1, bh, bh], out_shape=[c1, c1, ch, ch],
                          compiler_params=_cp(1))(lr, li, ldt, br, bi)


def ssm_params_bwd(name, lr, li, ldt, br, bi, dar, dai, dbbr, dbbi):
    GP, H = br.shape

    def body(lr_ref, li_ref, ldt_ref, br_ref, bi_ref, dar_ref, dai_ref, dbbr_ref, dbbi_ref,
             dlr_ref, dli_ref, dldt_ref, dbr_ref, dbi_ref):
        _, vjp = jax.vjp(_ssm_param_math, lr_ref[...], li_ref[...], ldt_ref[...], br_ref[...], bi_ref[...])
        dlr, dli, dldt, dbr, dbi = vjp((dar_ref[...], dai_ref[...], dbbr_ref[...], dbbi_ref[...]))
        dlr_ref[...] = dlr
        dli_ref[...] = dli
        dldt_ref[...] = dldt
        dbr_ref[...] = dbr
        dbi_ref[...] = dbi

    c1 = jax.ShapeDtypeStruct((GP, 1), F32)
    ch = jax.ShapeDtypeStruct((GP, H), F32)
    tr = _tile(GP, 512)
    b1 = pl.BlockSpec((tr, 1), lambda i: (i, 0))
    bh = pl.BlockSpec((tr, H), lambda i: (i, 0))
    return pl.pallas_call(body, name=name, grid=(GP // tr,), in_specs=[b1, b1, b1, bh, bh, b1, b1, bh, bh],
                          out_specs=[b1, b1, b1, bh, bh], out_shape=[c1, c1, c1, ch, ch],
                          compiler_params=_cp(1))(lr, li, ldt, br, bi, dar, dai, dbbr, dbbi)


def ssm_expand(name, src, col_off, mat, tm):
    N = src.shape[0]
    NC, cw, w2 = mat.shape
    W = w2 // 2
    tm = _tile(N, tm)

    def body(s_ref, m_ref, re_ref, im_ref):
        r = jnp.dot(s_ref[...].astype(CDT), m_ref[...], preferred_element_type=F32)
        re_ref[...] = r[:, :W]
        im_ref[...] = r[:, W:]

    out = pl.BlockSpec((tm, W), lambda i, c: (i, c))
    return pl.pallas_call(
        body, name=name, grid=(N // tm, NC),
        in_specs=[pl.BlockSpec((tm, cw), lambda i, c: (i, c + col_off)),
                  pl.BlockSpec((None, cw, w2), lambda i, c: (c, 0, 0))],
        out_specs=[out, out],
        out_shape=[jax.ShapeDtypeStruct((N, NC * W), F32)] * 2,
        compiler_params=_cp(2),
    )(src, mat)


def ssm_contract(name, re, im, mat, extras, epi, out_dtypes, tm):
    N = re.shape[0]
    NC, w2, cw = mat.shape
    W = w2 // 2
    tm = _tile(N, tm)
    ne = len(extras)

    def body(*refs):
        re_ref, im_ref, m_ref = refs[:3]
        e_refs = refs[3:3 + ne]
        o_refs = refs[3 + ne:]
        acc = jnp.dot(re_ref[...].astype(CDT), m_ref[pl.ds(0, W), :], preferred_element_type=F32)
        acc = acc + jnp.dot(im_ref[...].astype(CDT), m_ref[pl.ds(W, W), :], preferred_element_type=F32)
        outs = epi(acc, [e[...] for e in e_refs])
        for o_ref, o in zip(o_refs, outs):
            o_ref[...] = o.astype(o_ref.dtype)

    e_arrs, e_specs = [], []
    for arr, kind, off in extras:
        e_arrs.append(arr)
        if kind == "mn":
            e_specs.append(pl.BlockSpec((tm, cw), lambda i, c, off=off: (i, c + off)))
        else:
            e_specs.append(pl.BlockSpec((1, cw), lambda i, c, off=off: (0, c + off)))
    blk = pl.BlockSpec((tm, W), lambda i, c: (i, c))
    return pl.pallas_call(
        body, name=name, grid=(N // tm, NC),
        in_specs=[blk, blk, pl.BlockSpec((None, w2, cw), lambda i, c: (c, 0, 0))] + e_specs,
        out_specs=[pl.BlockSpec((tm, cw), lambda i, c: (i, c)) for _ in out_dtypes],
        out_shape=[jax.ShapeDtypeStruct((N, NC * cw), dt) for dt in out_dtypes],
        compiler_params=_cp(2),
    )(re, im, mat, *e_arrs)


def ssm_outer(name, src, col_off, re, im, NC, Wc, tm):
    N = src.shape[0]
    tm = _tile(N, tm)

    def body(s_ref, re_ref, im_ref, o1_ref, o2_ref):
        @pl.when(pl.program_id(1) == 0)
        def _():
            o1_ref[...] = jnp.zeros_like(o1_ref)
            o2_ref[...] = jnp.zeros_like(o2_ref)

        sv = s_ref[...].astype(CDT)
        dn = (((0,), (0,)), ((), ()))
        o1_ref[...] += lax.dot_general(sv, re_ref[...].astype(CDT), dn, preferred_element_type=F32)
        o2_ref[...] += lax.dot_general(sv, im_ref[...].astype(CDT), dn, preferred_element_type=F32)

    blk = pl.BlockSpec((tm, Wc), lambda c, i: (i, c))
    oblk = pl.BlockSpec((None, LANES, Wc), lambda c, i: (c, 0, 0))
    return pl.pallas_call(
        body, name=name, grid=(NC, N // tm),
        in_specs=[pl.BlockSpec((tm, LANES), lambda c, i: (i, c + col_off)), blk, blk],
        out_specs=[oblk, oblk],
        out_shape=[jax.ShapeDtypeStruct((NC, LANES, Wc), F32)] * 2,
        compiler_params=_cp(2),
    )(src, re, im)


def ssm_scan_fwd(name, bur, bui, ar, ai, n_seq, seq, tc):
    N, R, L = bur.shape
    tc = _tile(seq, tc, 1)
    nt = seq // tc

    def body(bur_ref, bui_ref, ar_ref, ai_ref, xr_ref, xi_ref, st_ref):
        @pl.when(pl.program_id(1) == 0)
        def _():
            st_ref[...] = jnp.zeros_like(st_ref)

        a_r = ar_ref[...]
        a_i = ai_ref[...]

        def step(t, carry):
            xr, xi = carry
            nr = a_r * xr - a_i * xi + bur_ref[t]
            ni = a_r * xi + a_i * xr + bui_ref[t]
            xr_ref[t] = nr
            xi_ref[t] = ni
            return nr, ni

        xr, xi = lax.fori_loop(0, tc, step, (st_ref[0], st_ref[1]))
        st_ref[0] = xr
        st_ref[1] = xi

    blk = pl.BlockSpec((tc, R, L), lambda b, j: (b * nt + j, 0, 0))
    par = pl.BlockSpec((R, L), lambda b, j: (0, 0))
    return pl.pallas_call(
        body, name=name, grid=(n_seq, nt),
        in_specs=[blk, blk, par, par], out_specs=[blk, blk],
        out_shape=[jax.ShapeDtypeStruct((N, R, L), F32)] * 2,
        scratch_shapes=[pltpu.VMEM((2, R, L), F32)],
        compiler_params=_cp(2),
    )(bur, bui, ar, ai)


def ssm_scan_bwd(name, gr, gi, xr, xi, ar, ai, n_seq, seq, tc):
    N, R, L = gr.shape
    tc = _tile(seq, tc, 1)
    nt = seq // tc

    def body(gr_ref, gi_ref, xr_ref, xi_ref, ar_ref, ai_ref, lr_ref, li_ref, dar_ref, dai_ref, st_ref):
        b = pl.program_id(0)
        j = pl.program_id(1)

        @pl.when((b == 0) & (j == 0))
        def _():
            dar_ref[...] = jnp.zeros_like(dar_ref)
            dai_ref[...] = jnp.zeros_like(dai_ref)

        @pl.when(j == 0)
        def _():
            st_ref[...] = jnp.zeros_like(st_ref)

        a_r = ar_ref[...]
        a_i = ai_ref[...]

        def step(s, carry):
            t = tc - 1 - s
            lr, li, dr, di = carry
            xrt = xr_ref[t]
            xit = xi_ref[t]
            dr = dr + (lr * xrt + li * xit)
            di = di + (li * xrt - lr * xit)
            nlr = gr_ref[t] + (a_r * lr + a_i * li)
            nli = gi_ref[t] + (a_r * li - a_i * lr)
            lr_ref[t] = nlr
            li_ref[t] = nli
            return nlr, nli, dr, di

        lr, li, dr, di = lax.fori_loop(0, tc, step, (st_ref[0], st_ref[1], dar_ref[...], dai_ref[...]))
        st_ref[0] = lr
        st_ref[1] = li
        dar_ref[...] = dr
        dai_ref[...] = di

    blk = pl.BlockSpec((tc, R, L), lambda b, j: (b * nt + nt - 1 - j, 0, 0))
    par = pl.BlockSpec((R, L), lambda b, j: (0, 0))
    return pl.pallas_call(
        body, name=name, grid=(n_seq, nt),
        in_specs=[blk, blk, blk, blk, par, par], out_specs=[blk, blk, par, par],
        out_shape=[jax.ShapeDtypeStruct((N, R, L), F32)] * 2 + [jax.ShapeDtypeStruct((R, L), F32)] * 2,
        scratch_shapes=[pltpu.VMEM((2, R, L), F32)],
        compiler_params=_cp(2),
    )(gr, gi, xr, xi, ar, ai)


def glu_bwd_pre(name, dy_ssm, y_pre, q):
    N, SW = y_pre.shape
    tm = _tile(N, 128)

    def body(dy_ref, y_ref, q_ref, dq_ref, dyg_ref, db_ref):
        dy = dy_ref[...]
        yg = _gelu(y_ref[...])
        s = jax.nn.sigmoid(q_ref[...])
        dq = dy * yg * (s * (1.0 - s))
        dq_ref[...] = dq.astype(dq_ref.dtype)
        dyg_ref[...] = dy * s

        @pl.when(pl.program_id(0) == 0)
        def _():
            db_ref[...] = jnp.zeros_like(db_ref)

        db_ref[...] += jnp.sum(dq, axis=0, keepdims=True)

    row = pl.BlockSpec((tm, SW), lambda i: (i, 0))
    vec = pl.BlockSpec((1, SW), lambda i: (0, 0))
    return pl.pallas_call(
        body, name=name, grid=(N // tm,), in_specs=[row, row, row], out_specs=[row, row, vec],
        out_shape=[jax.ShapeDtypeStruct((N, SW), CDT), jax.ShapeDtypeStruct((N, SW), F32),
                   jax.ShapeDtypeStruct((1, SW), F32)],
        compiler_params=_cp(1),
    )(dy_ssm, y_pre, q)


def colsum_prod(name, a, b, b_col_off, width):
    N = a.shape[0]
    tm = _tile(N, 128)

    def body(a_ref, b_ref, o_ref):
        @pl.when(pl.program_id(0) == 0)
        def _():
            o_ref[...] = jnp.zeros_like(o_ref)

        o_ref[...] += jnp.sum(a_ref[...] * b_ref[...], axis=0, keepdims=True)

    return pl.pallas_call(
        body, name=name, grid=(N // tm,),
        in_specs=[pl.BlockSpec((tm, width), lambda i: (i, 0)),
                  pl.BlockSpec((tm, width), lambda i: (i, b_col_off))],
        out_specs=pl.BlockSpec((1, width), lambda i: (0, 0)),
        out_shape=jax.ShapeDtypeStruct((1, width), F32),
        compiler_params=_cp(1),
    )(a, b)


def adamw(name, w, g, m, v):
    R, C = w.shape
    tr = _tile(R, max(8, (1 << 18) // C))

    def body(w_ref, g_ref, m_ref, v_ref, d_ref, nm_ref, nv_ref):
        gv = g_ref[...]
        nm = ADAM_B1 * m_ref[...] + (1.0 - ADAM_B1) * gv
        nv = ADAM_B2 * v_ref[...] + (1.0 - ADAM_B2) * jnp.square(gv)
        m_hat = nm / (1.0 - ADAM_B1 ** ADAM_STEP)
        v_hat = nv / (1.0 - ADAM_B2 ** ADAM_STEP)
        d_ref[...] = -ADAM_LR * (m_hat / (jnp.sqrt(v_hat) + ADAM_EPS) + ADAM_WD * w_ref[...])
        nm_ref[...] = nm
        nv_ref[...] = nv

    blk = pl.BlockSpec((tr, C), lambda i: (i, 0))
    sh = jax.ShapeDtypeStruct((R, C), F32)
    return pl.pallas_call(body, name=name, grid=(R // tr,), in_specs=[blk] * 4, out_specs=[blk] * 3,
                          out_shape=[sh] * 3, compiler_params=_cp(1))(w, g, m, v)


def add_halves(name, part, recv, c_idx):
    S4, R, C = part.shape
    h = R // 2
    tr = _tile(h, max(16, (1 << 19) // C), 16)
    nb = h // tr

    def body(c_ref, p_ref, r_ref, o_ref):
        o_ref[...] = (p_ref[...].astype(F32) + r_ref[...].astype(F32)).astype(o_ref.dtype)

    grid_spec = pltpu.PrefetchScalarGridSpec(
        num_scalar_prefetch=1, grid=(S4, nb),
        in_specs=[pl.BlockSpec((None, tr, C), lambda s, i, c_ref: (s, c_ref[0] * nb + i, 0)),
                  pl.BlockSpec((None, tr, C), lambda s, i, c_ref: (s, i, 0))],
        out_specs=pl.BlockSpec((None, tr, C), lambda s, i, c_ref: (s, i, 0)),
    )
    return pl.pallas_call(body, name=name, grid_spec=grid_spec,
                          out_shape=jax.ShapeDtypeStruct((S4, h, C), CDT),
                          compiler_params=_cp(2))(c_idx, part, recv)


def sum_chips(name, t, u, idx):
    _, h, C = t.shape
    tr = _tile(h, max(16, (1 << 18) // C), 16)
    nb = h // tr

    def body(idx_ref, t_ref, u_ref, o_ref):
        acc = t_ref[...].astype(F32)
        for r in range(NCHIP - 1):
            acc = acc + u_ref[r].astype(F32)
        o_ref[...] = acc

    grid_spec = pltpu.PrefetchScalarGridSpec(
        num_scalar_prefetch=1, grid=(nb,),
        in_specs=[pl.BlockSpec((None, tr, C), lambda i, idx_ref: (idx_ref[0], i, 0)),
                  pl.BlockSpec((NCHIP - 1, tr, C), lambda i, idx_ref: (0, i, 0))],
        out_specs=pl.BlockSpec((tr, C), lambda i, idx_ref: (idx_ref[1] * nb + i, 0)),
    )
    return pl.pallas_call(body, name=name, grid_spec=grid_spec,
                          out_shape=jax.ShapeDtypeStruct((2 * h, C), F32), compiler_params=_cp(1))(idx, t, u)


def _place():
    x, y, c = lax.axis_index("x"), lax.axis_index("y"), lax.axis_index("c")
    chips = [(1 - x, y), (x, 1 - y), (1 - x, 1 - y)]
    return x, y, c, chips


_ANY = pl.BlockSpec(memory_space=pl.ANY)


def gather_shards(name, shards):
    n = len(shards)

    def body(*refs):
        ins = refs[:n]
        outs = refs[n:2 * n]
        send_sems, recv_sems = refs[2 * n:]
        x, y, c, chips = _place()
        me = 2 * x + y
        sib = (x, y, 1 - c)
        sends = []
        for k in range(n):
            h = ins[k].shape[0] // 2
            cp = pltpu.make_async_remote_copy(
                src_ref=ins[k], dst_ref=outs[k].at[me], send_sem=send_sems.at[k, 6], recv_sem=recv_sems.at[k, 6],
                device_id=sib, device_id_type=MESH)
            cp.start()
            sends.append(cp)
            for r, (qx, qy) in enumerate(chips):
                cp = pltpu.make_async_remote_copy(
                    src_ref=ins[k].at[pl.ds(c * h, h)], dst_ref=outs[k].at[me, pl.ds(c * h, h)],
                    send_sem=send_sems.at[k, r], recv_sem=recv_sems.at[k, r],
                    device_id=(qx, qy, c), device_id_type=MESH)
                cp.start()
                sends.append(cp)
        for k in range(n):
            h = ins[k].shape[0] // 2
            for r, (qx, qy) in enumerate(chips):
                q = 2 * qx + qy
                region = outs[k].at[q, pl.ds(c * h, h)]
                pltpu.make_async_remote_copy(
                    src_ref=region, dst_ref=region, send_sem=send_sems.at[k, r], recv_sem=recv_sems.at[k, r],
                    device_id=(qx, qy, c), device_id_type=MESH).wait_recv()
                cp = pltpu.make_async_remote_copy(
                    src_ref=region, dst_ref=region, send_sem=send_sems.at[k, 3 + r],
                    recv_sem=recv_sems.at[k, 3 + r], device_id=sib, device_id_type=MESH)
                cp.start()
                sends.append(cp)
        for k in range(n):
            h = ins[k].shape[0] // 2
            for r, (qx, qy) in enumerate(chips):
                q = 2 * qx + qy
                region = outs[k].at[q, pl.ds((1 - c) * h, h)]
                pltpu.make_async_remote_copy(
                    src_ref=region, dst_ref=region, send_sem=send_sems.at[k, 3 + r],
                    recv_sem=recv_sems.at[k, 3 + r], device_id=sib, device_id_type=MESH).wait_recv()
            own = outs[k].at[me]
            pltpu.make_async_remote_copy(
                src_ref=own, dst_ref=own, send_sem=send_sems.at[k, 6], recv_sem=recv_sems.at[k, 6],
                device_id=sib, device_id_type=MESH).wait_recv()
        for cp in sends:
            cp.wait_send()

    return pl.pallas_call(
        body, name=name,
        in_specs=[_ANY] * n, out_specs=[_ANY] * n,
        out_shape=[jax.ShapeDtypeStruct((NCHIP,) + s.shape, s.dtype) for s in shards],
        scratch_shapes=[pltpu.SemaphoreType.DMA((n, 7)), pltpu.SemaphoreType.DMA((n, 7))],
    )(*shards)


def exchange_halves(name, parts):
    n = len(parts)

    def body(*refs):
        ins = refs[:n]
        outs = refs[n:2 * n]
        send_sems, recv_sems = refs[2 * n:]
        x, y, c, _ = _place()
        sib = (x, y, 1 - c)
        cps = []
        for k in range(n):
            h = ins[k].shape[1] // 2
            cp = pltpu.make_async_remote_copy(
                src_ref=ins[k].at[:, pl.ds((1 - c) * h, h)], dst_ref=outs[k],
                send_sem=send_sems.at[k], recv_sem=recv_sems.at[k], device_id=sib, device_id_type=MESH)
            cp.start()
            cps.append(cp)
        for cp in cps:
            cp.wait()

    return pl.pallas_call(
        body, name=name, in_specs=[_ANY] * n, out_specs=[_ANY] * n,
        out_shape=[jax.ShapeDtypeStruct((p.shape[0], p.shape[1] // 2) + p.shape[2:], p.dtype) for p in parts],
        scratch_shapes=[pltpu.SemaphoreType.DMA((n,)), pltpu.SemaphoreType.DMA((n,))],
    )(*parts)


def exchange_chips(name, sums):
    n = len(sums)

    def body(*refs):
        ins = refs[:n]
        outs = refs[n:2 * n]
        send_sems, recv_sems = refs[2 * n:]
        x, y, c, chips = _place()
        cps = []
        for k in range(n):
            for r, (qx, qy) in enumerate(chips):
                cp = pltpu.make_async_remote_copy(
                    src_ref=ins[k].at[2 * qx + qy], dst_ref=outs[k].at[r],
                    send_sem=send_sems.at[k, r], recv_sem=recv_sems.at[k, r],
                    device_id=(qx, qy, c), device_id_type=MESH)
                cp.start()
                cps.append(cp)
        for cp in cps:
            cp.wait()

    return pl.pallas_call(
        body, name=name, in_specs=[_ANY] * n, out_specs=[_ANY] * n,
        out_shape=[jax.ShapeDtypeStruct((NCHIP - 1,) + s.shape[1:], s.dtype) for s in sums],
        scratch_shapes=[pltpu.SemaphoreType.DMA((n, 3)), pltpu.SemaphoreType.DMA((n, 3))],
    )(*sums)


def join_halves(name, fulls):
    n = len(fulls)

    def body(*refs):
        outs = refs[n:2 * n]
        send_sems, recv_sems = refs[2 * n:]
        x, y, c, _ = _place()
        sib = (x, y, 1 - c)
        cps = []
        for k in range(n):
            h = outs[k].shape[0] // 2
            mine = outs[k].at[pl.ds(c * h, h)]
            cp = pltpu.make_async_remote_copy(
                src_ref=mine, dst_ref=mine, send_sem=send_sems.at[k], recv_sem=recv_sems.at[k],
                device_id=sib, device_id_type=MESH)
            cp.start()
            cps.append((cp, k, h))
        for cp, k, h in cps:
            cp.wait_send()
            other = outs[k].at[pl.ds((1 - c) * h, h)]
            pltpu.make_async_remote_copy(
                src_ref=other, dst_ref=other, send_sem=send_sems.at[k], recv_sem=recv_sems.at[k],
                device_id=sib, device_id_type=MESH).wait_recv()

    return pl.pallas_call(
        body, name=name, in_specs=[_ANY] * n, out_specs=[_ANY] * n,
        out_shape=[jax.ShapeDtypeStruct(s.shape, s.dtype) for s in fulls],
        input_output_aliases={k: k for k in range(n)},
        scratch_shapes=[pltpu.SemaphoreType.DMA((n,)), pltpu.SemaphoreType.DMA((n,))],
    )(*fulls)


def all_reduce_small(name, buf):
    R, L = buf.shape
    h = R // 2

    def body(x_ref, o_ref, sib_ref, chip_ref, send_sems, recv_sems):
        x, y, c, chips = _place()
        me = 2 * x + y
        sib = (x, y, 1 - c)
        first = pltpu.make_async_remote_copy(src_ref=x_ref, dst_ref=sib_ref, send_sem=send_sems.at[0],
                                             recv_sem=recv_sems.at[0], device_id=sib, device_id_type=MESH)
        first.start()
        first.wait()
        mine = pl.ds(pl.multiple_of(c * h, 8), h)
        other = pl.ds(pl.multiple_of((1 - c) * h, 8), h)
        chip_ref[me] = x_ref[mine, :] + sib_ref[mine, :]
        cps = []
        for r, (qx, qy) in enumerate(chips):
            cp = pltpu.make_async_remote_copy(
                src_ref=chip_ref.at[me], dst_ref=chip_ref.at[me], send_sem=send_sems.at[1 + r],
                recv_sem=recv_sems.at[1 + r], device_id=(qx, qy, c), device_id_type=MESH)
            cp.start()
            cps.append(cp)
        for r, (qx, qy) in enumerate(chips):
            q = 2 * qx + qy
            pltpu.make_async_remote_copy(
                src_ref=chip_ref.at[q], dst_ref=chip_ref.at[q], send_sem=send_sems.at[1 + r],
                recv_sem=recv_sems.at[1 + r], device_id=(qx, qy, c), device_id_type=MESH).wait_recv()
        for cp in cps:
            cp.wait_send()
        o_ref[mine, :] = ((chip_ref[0] + chip_ref[1]) + chip_ref[2]) + chip_ref[3]
        last = pltpu.make_async_remote_copy(src_ref=o_ref.at[mine], dst_ref=o_ref.at[mine],
                                            send_sem=send_sems.at[4], recv_sem=recv_sems.at[4],
                                            device_id=sib, device_id_type=MESH)
        last.start()
        last.wait_send()
        pltpu.make_async_remote_copy(src_ref=o_ref.at[other], dst_ref=o_ref.at[other],
                                     send_sem=send_sems.at[4], recv_sem=recv_sems.at[4],
                                     device_id=sib, device_id_type=MESH).wait_recv()

    vm = pl.BlockSpec(memory_space=pltpu.VMEM)
    return pl.pallas_call(
        body, name=name, in_specs=[vm], out_specs=vm,
        out_shape=jax.ShapeDtypeStruct((R, L), F32),
        scratch_shapes=[pltpu.VMEM((R, L), F32), pltpu.VMEM((NCHIP, h, L), F32),
                        pltpu.SemaphoreType.DMA((5,)), pltpu.SemaphoreType.DMA((5,))],
        compiler_params=pltpu.CompilerParams(vmem_limit_bytes=VMEM_LIMIT),
    )(buf)


def reduce_scatter(tag, parts, place_idx):
    shapes = [p.shape for p in parts]
    parts2 = [p.reshape(p.shape[0], -1, p.shape[-1]) for p in parts]
    recv = exchange_halves(tag + "_xh", parts2)
    sums = [add_halves(f"{tag}_add{k}", p, r, place_idx[1:]) for k, (p, r) in enumerate(zip(parts2, recv))]
    slabs = exchange_chips(tag + "_xc", sums)
    fulls = [sum_chips(f"{tag}_sum{k}", t, u, place_idx) for k, (t, u) in enumerate(zip(sums, slabs))]
    full = join_halves(tag + "_jh", fulls)
    return [f.reshape(s[1:]) for f, s in zip(full, shapes)]


def _ffn_fwd(tag, h, gain, wg3, wu3, wd):
    n, rstd = rms_fwd(tag + "_norm", h, gain)

    def epi(accs, extras):
        g, u = accs
        return g, u, _silu(g) * u

    g, u, a = mm_nn(tag + "_up", [n], [wg3, wu3], tm=1024, tn=256, b3=True, out_dtypes=(CDT, CDT, CDT),
                    epi=epi, pair_to_acc=[0, 1])

    def epi_down(accs, extras):
        return (extras[0] + 0.5 * accs[0],)

    (h_out,) = mm_nn(tag + "_down", [a], [wd], tm=1024, tn=512, tk=wd.shape[0] // NCHIP, extras=[(h, "mn", 0)],
                     epi=epi_down)
    return h_out, (h, n, rstd, g, u, a)


def _ffn_bwd(tag, saved, gain, wg3, wu3, wd, dh_out, dfb, cast_scale):
    h, n, rstd, g, u, a = saved

    def epi_act(accs, extras):
        da = accs[0]
        gv = extras[0].astype(F32)
        uv = extras[1].astype(F32)
        s = jax.nn.sigmoid(gv)
        return da * uv * (s * (1.0 + gv * (1.0 - s))), da * (gv * s)

    dg, du = mm_nt(tag + "_bact", [dfb], [wd], tm=1024, tn=256, extras=[(g, "mn", 0), (u, "mn", 0)],
                   out_dtypes=(CDT, CDT), epi=epi_act)
    (dwd,) = mm_tn(tag + "_dwd", [a], [dfb], ti=512, tj=512)
    dwg, dwu = mm_tn(tag + "_dwgu", [n], [dg, du], ti=512, tj=256, o3=NCHIP)
    (dn,) = mm_nt(tag + "_dn", [dg, du], [wg3, wu3], tm=1024, tn=512, tk=wg3.shape[2] // 2, b3=True)
    dh_in, dcast, dgain = rms_bwd(tag + "_bnorm", dn, h, rstd, gain, dh_out, cast_scale)
    return dh_in, dcast, dgain, dwg, dwu, dwd


def _blockdiag(m, gpc):
    G, a, b = m.shape
    nc = G // gpc
    mask = jnp.eye(gpc, dtype=m.dtype)[None, :, None, :, None]
    out = m.reshape(nc, gpc, a, 1, b) * mask
    return out.reshape(nc, gpc * a, gpc * b)


def _diag_blocks(o, gpc, a, b):
    nc = o.shape[0]
    o5 = o.reshape(nc, gpc, a, gpc, b)
    mask = jnp.eye(gpc, dtype=o.dtype)[None, :, None, :, None]
    return jnp.sum(o5 * mask, axis=3).reshape(nc * gpc, a, b)


def kernel(x, ffn1_norm, ffn1_gate, ffn1_up, ffn1_down, mix_norm, w_in, w_pool, pool_scale, lam_re, lam_im, log_dt, b_re, b_im, c_re, c_im, d_skip, w_glu, b_glu, pool_out_norm, ssm_out_norm, w_out, ffn2_norm, ffn2_gate, ffn2_up, ffn2_down, final_norm, loss_target, m_ffn1_norm, m_ffn1_gate, m_ffn1_up, m_ffn1_down, m_mix_norm, m_w_in, m_w_pool, m_pool_scale, m_lam_re, m_lam_im, m_log_dt, m_b_re, m_b_im, m_c_re, m_c_im, m_d_skip, m_w_glu, m_b_glu, m_pool_out_norm, m_ssm_out_norm, m_w_out, m_ffn2_norm, m_ffn2_gate, m_ffn2_up, m_ffn2_down, m_final_norm, v_ffn1_norm, v_ffn1_gate, v_ffn1_up, v_ffn1_down, v_mix_norm, v_w_in, v_w_pool, v_pool_scale, v_lam_re, v_lam_im, v_log_dt, v_b_re, v_b_im, v_c_re, v_c_im, v_d_skip, v_w_glu, v_b_glu, v_pool_out_norm, v_ssm_out_norm, v_w_out, v_ffn2_norm, v_ffn2_gate, v_ffn2_up, v_ffn2_down, v_final_norm):
    weights = dict(ffn1_norm=ffn1_norm, ffn1_gate=ffn1_gate, ffn1_up=ffn1_up, ffn1_down=ffn1_down, mix_norm=mix_norm, w_in=w_in, w_pool=w_pool, pool_scale=pool_scale, lam_re=lam_re, lam_im=lam_im, log_dt=log_dt, b_re=b_re, b_im=b_im, c_re=c_re, c_im=c_im, d_skip=d_skip, w_glu=w_glu, b_glu=b_glu, pool_out_norm=pool_out_norm, ssm_out_norm=ssm_out_norm, w_out=w_out, ffn2_norm=ffn2_norm, ffn2_gate=ffn2_gate, ffn2_up=ffn2_up, ffn2_down=ffn2_down, final_norm=final_norm)
    moms = dict(ffn1_norm=(m_ffn1_norm, v_ffn1_norm), ffn1_gate=(m_ffn1_gate, v_ffn1_gate), ffn1_up=(m_ffn1_up, v_ffn1_up), ffn1_down=(m_ffn1_down, v_ffn1_down), mix_norm=(m_mix_norm, v_mix_norm), w_in=(m_w_in, v_w_in), w_pool=(m_w_pool, v_w_pool), pool_scale=(m_pool_scale, v_pool_scale), lam_re=(m_lam_re, v_lam_re), lam_im=(m_lam_im, v_lam_im), log_dt=(m_log_dt, v_log_dt), b_re=(m_b_re, v_b_re), b_im=(m_b_im, v_b_im), c_re=(m_c_re, v_c_re), c_im=(m_c_im, v_c_im), d_skip=(m_d_skip, v_d_skip), w_glu=(m_w_glu, v_w_glu), b_glu=(m_b_glu, v_b_glu), pool_out_norm=(m_pool_out_norm, v_pool_out_norm), ssm_out_norm=(m_ssm_out_norm, v_ssm_out_norm), w_out=(m_w_out, v_w_out), ffn2_norm=(m_ffn2_norm, v_ffn2_norm), ffn2_gate=(m_ffn2_gate, v_ffn2_gate), ffn2_up=(m_ffn2_up, v_ffn2_up), ffn2_down=(m_ffn2_down, v_ffn2_down), final_norm=(m_final_norm, v_final_norm))
    names = list(weights)

    n_seq, seq, D = x.shape
    N = n_seq * seq
    Fs = ffn1_gate.shape[1]
    Fps = -(-Fs // FF_ALIGN) * FF_ALIGN
    G, _, Cg = w_pool.shape
    PW = G * Cg
    SW = d_skip.shape[0]
    SG, P = lam_re.shape
    H = SSM_H
    gpc = LANES // H
    NC = SG // gpc
    W = gpc * P
    GP = SG * P
    place_idx = jnp.stack([2 * lax.axis_index("x") + lax.axis_index("y"), lax.axis_index("c")]).astype(jnp.int32)

    row = lambda v: v.reshape(1, -1)
    xf = x.reshape(N, D)
    tgt = loss_target.reshape(N, D)

    pad_c = lambda w: jnp.pad(w.astype(CDT), ((0, 0), (0, Fps - Fs)))
    pad_r = lambda w: jnp.pad(w.astype(CDT), ((0, Fps - Fs), (0, 0)))
    wg1, wu1, wd1 = gather_shards("gather_ffn1", [pad_c(ffn1_gate), pad_c(ffn1_up), pad_r(ffn1_down)])
    wd1 = wd1.reshape(NCHIP * Fps, D)
    w_in_f, w_out_f, w_glu_f, w_pool_f = gather_shards(
        "gather_mix", [w_in.astype(CDT), w_out.astype(CDT), w_glu.astype(CDT), w_pool.astype(CDT)])
    w_in_f = w_in_f.reshape(D, PW + SW)
    w_out_f = w_out_f.reshape(PW + SW, D)
    w_glu_f = w_glu_f.reshape(SW, SW)
    w_pool_f = jnp.swapaxes(w_pool_f, 0, 1).reshape(G, Cg, Cg)
    wg2, wu2, wd2 = gather_shards("gather_ffn2", [pad_c(ffn2_gate), pad_c(ffn2_up), pad_r(ffn2_down)])
    wd2 = wd2.reshape(NCHIP * Fps, D)

    h1, saved1 = _ffn_fwd("ffn1", xf, row(ffn1_norm), wg1, wu1, wd1)
    n2, rstd2 = rms_fwd("mix_norm", h1, row(mix_norm))
    (z,) = mm_nn("mix_in", [n2], [w_in_f], tm=1024, tn=256)

    d_pool, y_pool = pool_fwd("pool_fwd", z, w_pool_f, row(pool_scale), n_seq, seq)

    col = lambda v: v.reshape(GP, 1)
    lr_c, li_c = col(lam_re), col(lam_im)
    ldt_c = col(jnp.broadcast_to(log_dt[:, None], (SG, P)))
    br_c, bi_c = b_re.reshape(GP, H), b_im.reshape(GP, H)
    ar, ai, bbr, bbi = ssm_params_fwd("ssm_params", lr_c, li_c, ldt_c, br_c, bi_c)
    ar2, ai2 = ar.reshape(GP // LANES, LANES), ai.reshape(GP // LANES, LANES)
    bbr_t = jnp.swapaxes(bbr.reshape(SG, P, H), 1, 2)
    bbi_t = jnp.swapaxes(bbi.reshape(SG, P, H), 1, 2)
    b_cat = jnp.concatenate([_blockdiag(bbr_t, gpc), _blockdiag(bbi_t, gpc)], axis=2).astype(CDT)
    b_cat_t = jnp.swapaxes(b_cat, 1, 2)
    c_cat_t = jnp.concatenate([_blockdiag(c_re, gpc), _blockdiag(-c_im, gpc)], axis=2).astype(CDT)
    c_cat = jnp.swapaxes(c_cat_t, 1, 2)

    u_off = PW // LANES
    bur, bui = ssm_expand("ssm_bu", z, u_off, b_cat, SSM_ROWS)
    v3 = lambda a: a.reshape(N, GP // LANES, LANES)
    xs_r, xs_i = ssm_scan_fwd("ssm_scan", v3(bur), v3(bui), ar2, ai2, n_seq, seq, 64)
    xs_r, xs_i = xs_r.reshape(N, GP), xs_i.reshape(N, GP)

    def epi_y(acc, extras):
        y = acc + extras[1] * extras[0]
        return y, _gelu(y)

    y_pre, yg = ssm_contract("ssm_y", xs_r, xs_i, c_cat, [(z, "mn", u_off), (row(d_skip), "n", 0)], epi_y,
                             (F32, CDT), SSM_ROWS)

    def epi_glu(accs, extras):
        q = accs[0] + extras[1]
        return q, _gelu(extras[0]) * jax.nn.sigmoid(q)

    q_glu, y_ssm = mm_nn("glu", [yg], [w_glu_f], tm=1024, tn=256, extras=[(y_pre, "mn", 0), (row(b_glu), "n", 0)],
                         out_dtypes=(F32, F32), epi=epi_glu)
    merged, rstd_p, rstd_s = merge_norm_fwd("merge_norm", y_pool, y_ssm, row(pool_out_norm), row(ssm_out_norm))

    def epi_res(accs, extras):
        return (extras[0] + accs[0],)

    (h2,) = mm_nn("mix_out", [merged], [w_out_f], tm=1024, tn=256, extras=[(h1, "mn", 0)], epi=epi_res)
    h3, saved2 = _ffn_fwd("ffn2", h2, row(ffn2_norm), wg2, wu2, wd2)

    loss_acc, dh3, dfb3, g_final = loss_head("loss_head", h3, row(final_norm), tgt)
    dh2, dh2c, g_ffn2n, dwg2, dwu2, dwd2 = _ffn_bwd("ffn2", saved2, row(ffn2_norm), wg2, wu2, wd2, dh3, dfb3, 1.0)

    (dmerged,) = mm_nt("mix_out_bx", [dh2c], [w_out_f], tm=1024, tn=256)
    (dw_out,) = mm_tn("mix_out_bw", [merged], [dh2c], ti=512, tj=512)
    dy_pool, dy_ssm, g_pon, g_son = merge_norm_bwd("merge_norm_b", dmerged, y_pool, y_ssm, rstd_p, rstd_s,
                                                   row(pool_out_norm), row(ssm_out_norm))
    dq, dyg1, g_bglu = glu_bwd_pre("glu_b_pre", dy_ssm, y_pre, q_glu)

    def epi_dyg(accs, extras):
        return ((accs[0] + extras[0]) * _gelu_grad(extras[1]),)

    (dy_pre,) = mm_nt("glu_bx", [dq], [w_glu_f], tm=1024, tn=256, extras=[(dyg1, "mn", 0), (y_pre, "mn", 0)],
                      epi=epi_dyg)
    (dw_glu,) = mm_tn("glu_bw", [yg], [dq], ti=512, tj=512)
    g_dskip = colsum_prod("dskip", dy_pre, z, PW // SW, SW)
    gxr, gxi = ssm_expand("ssm_by", dy_pre, 0, c_cat_t, SSM_ROWS)
    dc_r, dc_i = ssm_outer("ssm_dc", dy_pre, 0, xs_r, xs_i, NC, W, SSM_ROWS)
    lam_r, lam_i, dar, dai = ssm_scan_bwd("ssm_scan_b", v3(gxr), v3(gxi), v3(xs_r), v3(xs_i), ar2, ai2,
                                          n_seq, seq, 64)
    lam_r, lam_i = lam_r.reshape(N, GP), lam_i.reshape(N, GP)
    db_r, db_i = ssm_outer("ssm_db", z, u_off, lam_r, lam_i, NC, W, SSM_ROWS)

    def epi_du(acc, extras):
        return (acc + extras[0] * extras[1],)

    (du_ssm,) = ssm_contract("ssm_bu_b", lam_r, lam_i, b_cat_t, [(dy_pre, "mn", 0), (row(d_skip), "n", 0)],
                             epi_du, (CDT,), SSM_ROWS)
    g_c_re = _diag_blocks(dc_r, gpc, H, P)
    g_c_im = -_diag_blocks(dc_i, gpc, H, P)
    dbbr = jnp.swapaxes(_diag_blocks(db_r, gpc, H, P), 1, 2).reshape(GP, H)
    dbbi = jnp.swapaxes(_diag_blocks(db_i, gpc, H, P), 1, 2).reshape(GP, H)
    dlr, dli, dldt, dbr, dbi = ssm_params_bwd("ssm_params_b", lr_c, li_c, ldt_c, br_c, bi_c,
                                              dar.reshape(GP, 1), dai.reshape(GP, 1), dbbr, dbbi)
    g_lam_re, g_lam_im = dlr.reshape(SG, P), dli.reshape(SG, P)
    g_log_dt = jnp.sum(dldt.reshape(SG, P), axis=1)
    g_b_re, g_b_im = dbr.reshape(SG, P, H), dbi.reshape(SG, P, H)
    dz_pool, dw_pool, g_pscale = pool_bwd("pool_bwd", dy_pool, d_pool, w_pool_f, row(pool_scale), n_seq, seq)
    dz = jnp.concatenate([dz_pool, du_ssm], axis=1)
    (dn2,) = mm_nt("mix_in_bx", [dz], [w_in_f], tm=1024, tn=256)
    (dw_in,) = mm_tn("mix_in_bw", [n2], [dz], ti=512, tj=512)
    dh1, dh1c, g_mixn = rms_bwd("mix_norm_b", dn2, h1, rstd2, row(mix_norm), dh2, 0.5)
    grad_x, _, g_ffn1n, dwg1, dwu1, dwd1 = _ffn_bwd("ffn1", saved1, row(ffn1_norm), wg1, wu1, wd1, dh1, dh1c, 1.0)

    dwp4 = jnp.swapaxes(dw_pool.astype(CDT).reshape(G, NCHIP, Cg // NCHIP, Cg), 0, 1)
    big = reduce_scatter("rs", [
        dwg1, dwu1, dwd1.reshape(NCHIP, Fps, D), dw_in.reshape(NCHIP, D // NCHIP, PW + SW), dwp4,
        dw_glu.reshape(NCHIP, SW // NCHIP, SW), dw_out.reshape(NCHIP, (PW + SW) // NCHIP, D),
        dwg2, dwu2, dwd2.reshape(NCHIP, Fps, D)], place_idx)
    grads = dict(ffn1_gate=big[0][:, :Fs], ffn1_up=big[1][:, :Fs], ffn1_down=big[2][:Fs], w_in=big[3],
                 w_pool=big[4], w_glu=big[5], w_out=big[6], ffn2_gate=big[7][:, :Fs], ffn2_up=big[8][:, :Fs],
                 ffn2_down=big[9][:Fs])

    small = dict(ffn1_norm=g_ffn1n, mix_norm=g_mixn, pool_scale=g_pscale, lam_re=g_lam_re, lam_im=g_lam_im,
                 log_dt=g_log_dt, b_re=g_b_re, b_im=g_b_im, c_re=g_c_re, c_im=g_c_im, d_skip=g_dskip,
                 b_glu=g_bglu, pool_out_norm=g_pon, ssm_out_norm=g_son, ffn2_norm=g_ffn2n, final_norm=g_final)
    pieces = [jnp.pad(small[k].reshape(-1), (0, (-small[k].size) % LANES)) for k in small]
    pieces.append(loss_acc.reshape(-1))
    flat = jnp.concatenate(pieces)
    rows = -(-flat.size // (16 * LANES)) * 16
    flat = jnp.pad(flat, (0, rows * LANES - flat.size)).reshape(rows, LANES)
    red = all_reduce_small("all_reduce_small", flat).reshape(-1)
    off = 0
    for k in small:
        size = small[k].size
        grads[k] = red[off:off + size].reshape(weights[k].shape)
        off += size + (-size) % LANES
    loss = (0.5 / D) * red[off]

    deltas, new_m, new_v = {}, {}, {}
    for k in names:
        w = weights[k]
        if w.ndim >= 2 and w.shape[-1] >= LANES:
            shape2 = (w.size // w.shape[-1], w.shape[-1])
        elif w.size % LANES == 0:
            shape2 = (w.size // LANES, LANES)
        else:
            shape2 = (1, w.size)
        m, v = moms[k]
        d_, m_, v_ = adamw("adamw_" + k, w.reshape(shape2), grads[k].reshape(shape2), m.reshape(shape2),
                           v.reshape(shape2))
        deltas[k], new_m[k], new_v[k] = d_.reshape(w.shape), m_.reshape(w.shape), v_.reshape(w.shape)

    return (loss, grad_x.reshape(x.shape), *[grads[k] for k in names], *[deltas[k] for k in names],
            *[new_m[k] for k in names], *[new_v[k] for k in names])
```

```python
import functools
import math

import jax
import jax.numpy as jnp
from jax import lax
from jax.experimental import pallas as pl
from jax.experimental.pallas import tpu as pltpu

F32 = jnp.float32
CDT = jnp.bfloat16
NORM_EPS = 1e-6
POOL_WINDOWS = (2, 4, 8, 16)
POOL_HALO = 16
SSM_H = 16
SSM_ROWS = 2048
LANES = 128
FF_ALIGN = 256
NCHIP = 4
VMEM_LIMIT = 48 * 1024 * 1024
ADAM_LR = 0.001
ADAM_B1 = 0.9
ADAM_B2 = 0.999
ADAM_EPS = 1e-08
ADAM_WD = 0.01
ADAM_STEP = 10
MESH = pl.DeviceIdType.MESH


def _cp(n_grid):
    return pltpu.CompilerParams(dimension_semantics=("arbitrary",) * n_grid, vmem_limit_bytes=VMEM_LIMIT)


def _tile(n, pref, align=8):
    if n <= pref:
        return n
    t = (pref // align) * align
    while t >= align:
        if n % t == 0:
            return t
        t -= align
    return n


def _silu(x):
    return x * jax.nn.sigmoid(x)


_GELU_C = math.sqrt(2.0 / math.pi)


def _gelu(x):
    return x * (0.5 * (1.0 + jnp.tanh(_GELU_C * (x + 0.044715 * (x * x * x)))))


def _gelu_grad(x):
    t = jnp.tanh(_GELU_C * (x + 0.044715 * (x * x * x)))
    return 0.5 * (1.0 + t) + 0.5 * x * (1.0 - t * t) * (_GELU_C * (1.0 + 3.0 * 0.044715 * x * x))


def _comm_plumb(specs):
    arrays, out_shapes, aliases, scratch = [], [], {}, []
    for sp in specs:
        for i, j in sp["aliases"].items():
            aliases[len(arrays) + i] = len(out_shapes) + j
        arrays += sp["arrays"]
        out_shapes += sp["out_shapes"]
        scratch += [pltpu.SemaphoreType.DMA(sp["sems"]), pltpu.SemaphoreType.DMA(sp["sems"])]

    def make(in_refs, out_refs, sem_refs):
        starts, waits = [], []
        i0 = o0 = 0
        for n, sp in enumerate(specs):
            ni, no = len(sp["arrays"]), len(sp["out_shapes"])
            s, w = sp["make"](in_refs[i0:i0 + ni], out_refs[o0:o0 + no], sem_refs[2 * n], sem_refs[2 * n + 1])
            starts += s
            waits += w
            i0 += ni
            o0 += no
        return starts, waits

    return arrays, out_shapes, aliases, scratch, make


def comm_call(name, specs):
    arrays, out_shapes, aliases, scratch, make = _comm_plumb(specs)
    ni, no = len(arrays), len(out_shapes)

    def body(*refs):
        starts, waits = make(refs[:ni], refs[ni:ni + no], refs[ni + no:])
        for s in starts:
            s()
        for w in waits:
            w()

    return pl.pallas_call(body, name=name, in_specs=[_ANY] * ni, out_specs=[_ANY] * no, out_shape=out_shapes,
                          input_output_aliases=aliases, scratch_shapes=scratch)(*arrays)


def _mm(name, dn, grid, a_list, a_specs, b_list, b_specs, e_list, e_specs, out_shapes, out_specs,
        acc_shape, pair_to_acc, epi, comm=()):
    na, nb, ne, no = len(a_list), len(b_list), len(e_list), len(out_shapes)
    assert na in (1, nb)
    n_acc = max(pair_to_acc) + 1
    nk = grid[2]
    c_arrays, c_out_shapes, c_aliases, c_scratch, c_make = _comm_plumb(comm)
    nci, nco = len(c_arrays), len(c_out_shapes)
    n_in = na + nb + ne
    n_scr = n_acc if nk > 1 else 0

    def body(*refs):
        a_refs = refs[:na]
        b_refs = refs[na:na + nb]
        e_refs = refs[na + nb:n_in]
        o_refs = refs[n_in + nci:n_in + nci + no]
        acc_refs = refs[n_in + nci + no + nco:n_in + nci + no + nco + n_scr]
        if comm:
            starts, waits = c_make(refs[n_in:n_in + nci], refs[n_in + nci + no:n_in + nci + no + nco],
                                   refs[n_in + nci + no + nco + n_scr:])
            gi, gj, gk = pl.program_id(0), pl.program_id(1), pl.program_id(2)

            @pl.when((gi == 0) & (gj == 0) & (gk == 0))
            def _():
                for s in starts:
                    s()

        prods = [None] * n_acc
        a_vals = [a_ref[...].astype(CDT) for a_ref in a_refs]
        for p in range(nb):
            d = lax.dot_general(a_vals[p if na > 1 else 0], b_refs[p][...].astype(CDT), dn,
                                preferred_element_type=F32)
            q = pair_to_acc[p]
            prods[q] = d if prods[q] is None else prods[q] + d

        def finish(accs):
            outs = epi(accs, [e[...] for e in e_refs])
            for o_ref, o in zip(o_refs, outs):
                o_ref[...] = o.astype(o_ref.dtype)

        if nk == 1:
            finish(prods)
        else:
            k = pl.program_id(2)

            @pl.when(k == 0)
            def _():
                for acc, p in zip(acc_refs, prods):
                    acc[...] = p

            @pl.when(k > 0)
            def _():
                for acc, p in zip(acc_refs, prods):
                    acc[...] += p

            @pl.when(k == nk - 1)
            def _():
                finish([acc[...] for acc in acc_refs])

        if comm:
            @pl.when((gi == grid[0] - 1) & (gj == grid[1] - 1) & (gk == grid[2] - 1))
            def _():
                for w in waits:
                    w()

    scratch = ([pltpu.VMEM(acc_shape, F32) for _ in range(n_acc)] if nk > 1 else []) + c_scratch
    outs = pl.pallas_call(
        body, name=name, grid=grid,
        in_specs=list(a_specs) + list(b_specs) + list(e_specs) + [_ANY] * nci,
        out_specs=list(out_specs) + [_ANY] * nco, out_shape=list(out_shapes) + c_out_shapes,
        input_output_aliases={n_in + i: no + j for i, j in c_aliases.items()},
        scratch_shapes=scratch, compiler_params=_cp(3),
    )(*a_list, *b_list, *e_list, *c_arrays)
    return outs


def _extra_specs(extras, tm, tn):
    arrs, specs = [], []
    for arr, kind, off in extras:
        arrs.append(arr)
        if kind == "mn":
            specs.append(pl.BlockSpec((tm, tn), lambda i, j, k, off=off: (i, j + off)))
        elif kind == "n":
            specs.append(pl.BlockSpec((1, tn), lambda i, j, k, off=off: (0, j + off)))
        else:
            raise ValueError(kind)
    return arrs, specs


def _first(accs, extras):
    return (accs[0],)


def mm_nn(name, a_list, b_list, *, tm, tn, tk=None, b3=False, extras=(), out_dtypes=(F32,), epi=_first,
          pair_to_acc=None, comm=()):
    M, K = a_list[0].shape
    if b3:
        s4, _, ns = b_list[0].shape
        N = s4 * ns
        tn = _tile(ns, tn, LANES)
        nps = ns // tn
    else:
        N = b_list[0].shape[1]
        tn = _tile(N, tn, LANES)
    tm = _tile(M, tm, 16)
    tk = _tile(K, tk or K, LANES)
    grid = (M // tm, N // tn, K // tk)
    a_specs = [pl.BlockSpec((tm, tk), lambda i, j, k: (i, k)) for _ in a_list]
    if b3:
        b_specs = [pl.BlockSpec((None, tk, tn), lambda i, j, k: (j // nps, k, j % nps)) for _ in b_list]
    else:
        b_specs = [pl.BlockSpec((tk, tn), lambda i, j, k: (k, j)) for _ in b_list]
    e_list, e_specs = _extra_specs(extras, tm, tn)
    out_shapes = [jax.ShapeDtypeStruct((M, N), dt) for dt in out_dtypes]
    out_specs = [pl.BlockSpec((tm, tn), lambda i, j, k: (i, j)) for _ in out_dtypes]
    pair_to_acc = pair_to_acc or [0] * len(b_list)
    dn = (((1,), (0,)), ((), ()))
    return _mm(name, dn, grid, a_list, a_specs, b_list, b_specs, e_list, e_specs, out_shapes, out_specs,
               (tm, tn), pair_to_acc, epi, comm)


def mm_nt(name, a_list, b_list, *, tm, tn, tk=None, b3=False, extras=(), out_dtypes=(F32,), epi=_first,
          pair_to_acc=None, comm=()):
    M, K = a_list[0].shape
    if b3:
        s4, N, ks = b_list[0].shape
        tk = _tile(ks, tk or ks, LANES)
        kps = ks // tk
    else:
        N = b_list[0].shape[0]
        tk = _tile(K, tk or K, LANES)
    tm = _tile(M, tm, 16)
    tn = _tile(N, tn, LANES)
    grid = (M // tm, N // tn, K // tk)
    a_specs = [pl.BlockSpec((tm, tk), lambda i, j, k: (i, k)) for _ in a_list]
    if b3:
        b_specs = [pl.BlockSpec((None, tn, tk), lambda i, j, k: (k // kps, j, k % kps)) for _ in b_list]
    else:
        b_specs = [pl.BlockSpec((tn, tk), lambda i, j, k: (j, k)) for _ in b_list]
    e_list, e_specs = _extra_specs(extras, tm, tn)
    out_shapes = [jax.ShapeDtypeStruct((M, N), dt) for dt in out_dtypes]
    out_specs = [pl.BlockSpec((tm, tn), lambda i, j, k: (i, j)) for _ in out_dtypes]
    pair_to_acc = pair_to_acc or [0] * len(b_list)
    dn = (((1,), (1,)), ((), ()))
    return _mm(name, dn, grid, a_list, a_specs, b_list, b_specs, e_list, e_specs, out_shapes, out_specs,
               (tm, tn), pair_to_acc, epi, comm)


def mm_tn(name, a_list, b_list, *, ti, tj, o3=0, out_dtype=CDT, comm=()):
    T, I = a_list[0].shape
    J = b_list[0].shape[1]
    ti = _tile(I, ti, LANES)
    if o3:
        js = J // o3
        tj = _tile(js, tj, LANES)
        jps = js // tj
        out_shapes = [jax.ShapeDtypeStruct((o3, I, js), out_dtype) for _ in b_list]
        out_specs = [pl.BlockSpec((None, ti, tj), lambda i, j, k: (j // jps, i, j % jps)) for _ in b_list]
    else:
        tj = _tile(J, tj, LANES)
        out_shapes = [jax.ShapeDtypeStruct((I, J), out_dtype) for _ in b_list]
        out_specs = [pl.BlockSpec((ti, tj), lambda i, j, k: (i, j)) for _ in b_list]
    grid = (I // ti, J // tj, 1)
    a_specs = [pl.BlockSpec((T, ti), lambda i, j, k: (0, i)) for _ in a_list]
    b_specs = [pl.BlockSpec((T, tj), lambda i, j, k: (0, j)) for _ in b_list]
    dn = (((0,), (0,)), ((), ()))
    return _mm(name, dn, grid, a_list, a_specs, b_list, b_specs, [], [], out_shapes, out_specs,
               (ti, tj), list(range(len(b_list))), lambda accs, extras: tuple(accs), comm)


def rms_fwd(name, x, gain):
    N, D = x.shape
    tm = _tile(N, 128)

    def body(x_ref, g_ref, n_ref, r_ref):
        xf = x_ref[...]
        r = lax.rsqrt(jnp.mean(xf * xf, axis=-1, keepdims=True) + NORM_EPS)
        n_ref[...] = ((xf * r) * g_ref[...]).astype(n_ref.dtype)
        r_ref[...] = r

    return pl.pallas_call(
        body, name=name, grid=(N // tm,),
        in_specs=[pl.BlockSpec((tm, D), lambda i: (i, 0)), pl.BlockSpec((1, D), lambda i: (0, 0))],
        out_specs=[pl.BlockSpec((tm, D), lambda i: (i, 0)), pl.BlockSpec((tm, 1), lambda i: (i, 0))],
        out_shape=[jax.ShapeDtypeStruct((N, D), CDT), jax.ShapeDtypeStruct((N, 1), F32)],
        compiler_params=_cp(1),
    )(x, gain)


def _rms_bwd_math(dn, x, r, g):
    xhat = x * r
    dgain = jnp.sum(dn * xhat, axis=0, keepdims=True)
    dxhat = dn * g
    dx = r * (dxhat - xhat * jnp.mean(dxhat * xhat, axis=-1, keepdims=True))
    return dx, dgain


def rms_bwd(name, dn, x, rstd, gain, dres, cast_scale):
    N, D = x.shape
    tm = _tile(N, 128)

    def body(dn_ref, x_ref, r_ref, g_ref, dres_ref, dx_ref, dc_ref, dg_ref):
        dx, dgain = _rms_bwd_math(dn_ref[...], x_ref[...], r_ref[...], g_ref[...])
        tot = dres_ref[...] + dx
        dx_ref[...] = tot
        dc_ref[...] = (cast_scale * tot).astype(dc_ref.dtype)

        @pl.when(pl.program_id(0) == 0)
        def _():
            dg_ref[...] = jnp.zeros_like(dg_ref)

        dg_ref[...] += dgain

    row = pl.BlockSpec((tm, D), lambda i: (i, 0))
    vec = pl.BlockSpec((1, D), lambda i: (0, 0))
    return pl.pallas_call(
        body, name=name, grid=(N // tm,),
        in_specs=[row, row, pl.BlockSpec((tm, 1), lambda i: (i, 0)), vec, row],
        out_specs=[row, row, vec],
        out_shape=[jax.ShapeDtypeStruct((N, D), F32), jax.ShapeDtypeStruct((N, D), CDT),
                   jax.ShapeDtypeStruct((1, D), F32)],
        compiler_params=_cp(1),
    )(dn, x, rstd, gain, dres)


def loss_head(name, h, gain, target):
    N, D = h.shape
    tm = _tile(N, 128)

    def body(h_ref, g_ref, t_ref, l_ref, dh_ref, dc_ref, dg_ref):
        x = h_ref[...]
        g = g_ref[...]
        r = lax.rsqrt(jnp.mean(x * x, axis=-1, keepdims=True) + NORM_EPS)
        err = (x * r) * g - t_ref[...]
        dy = err * (1.0 / D)
        dx, dgain = _rms_bwd_math(dy, x, r, g)
        dh_ref[...] = dx
        dc_ref[...] = (0.5 * dx).astype(dc_ref.dtype)

        @pl.when(pl.program_id(0) == 0)
        def _():
            dg_ref[...] = jnp.zeros_like(dg_ref)
            l_ref[...] = jnp.zeros_like(l_ref)

        dg_ref[...] += dgain
        l_ref[...] += jnp.sum(err * err)

    row = pl.BlockSpec((tm, D), lambda i: (i, 0))
    vec = pl.BlockSpec((1, D), lambda i: (0, 0))
    return pl.pallas_call(
        body, name=name, grid=(N // tm,),
        in_specs=[row, vec, row],
        out_specs=[pl.BlockSpec((1, LANES), lambda i: (0, 0)), row, row, vec],
        out_shape=[jax.ShapeDtypeStruct((1, LANES), F32), jax.ShapeDtypeStruct((N, D), F32),
                   jax.ShapeDtypeStruct((N, D), CDT), jax.ShapeDtypeStruct((1, D), F32)],
        compiler_params=_cp(1),
    )(h, gain, target)


def merge_norm_fwd(name, y_pool, y_ssm, g_pool, g_ssm):
    N, PW = y_pool.shape
    SW = y_ssm.shape[1]
    tm = _tile(N, 128)

    def body(yp_ref, ys_ref, gp_ref, gs_ref, m_ref, rp_ref, rs_ref):
        yp = yp_ref[...]
        ys = ys_ref[...]
        rp = lax.rsqrt(jnp.mean(yp * yp, axis=-1, keepdims=True) + NORM_EPS)
        rs = lax.rsqrt(jnp.mean(ys * ys, axis=-1, keepdims=True) + NORM_EPS)
        m_ref[:, :PW] = ((yp * rp) * gp_ref[...]).astype(m_ref.dtype)
        m_ref[:, PW:] = ((ys * rs) * gs_ref[...]).astype(m_ref.dtype)
        rp_ref[...] = rp
        rs_ref[...] = rs

    return pl.pallas_call(
        body, name=name, grid=(N // tm,),
        in_specs=[pl.BlockSpec((tm, PW), lambda i: (i, 0)), pl.BlockSpec((tm, SW), lambda i: (i, 0)),
                  pl.BlockSpec((1, PW), lambda i: (0, 0)), pl.BlockSpec((1, SW), lambda i: (0, 0))],
        out_specs=[pl.BlockSpec((tm, PW + SW), lambda i: (i, 0)), pl.BlockSpec((tm, 1), lambda i: (i, 0)),
                   pl.BlockSpec((tm, 1), lambda i: (i, 0))],
        out_shape=[jax.ShapeDtypeStruct((N, PW + SW), CDT), jax.ShapeDtypeStruct((N, 1), F32),
                   jax.ShapeDtypeStruct((N, 1), F32)],
        compiler_params=_cp(1),
    )(y_pool, y_ssm, g_pool, g_ssm)


def merge_norm_bwd(name, dmerged, y_pool, y_ssm, r_pool, r_ssm, g_pool, g_ssm):
    N, PW = y_pool.shape
    SW = y_ssm.shape[1]
    tm = _tile(N, 128)

    def body(dm_ref, yp_ref, ys_ref, rp_ref, rs_ref, gp_ref, gs_ref, dyp_ref, dys_ref, dgp_ref, dgs_ref):
        dxp, dgp = _rms_bwd_math(dm_ref[:, :PW], yp_ref[...], rp_ref[...], gp_ref[...])
        dxs, dgs = _rms_bwd_math(dm_ref[:, PW:], ys_ref[...], rs_ref[...], gs_ref[...])
        dyp_ref[...] = dxp
        dys_ref[...] = dxs

        @pl.when(pl.program_id(0) == 0)
        def _():
            dgp_ref[...] = jnp.zeros_like(dgp_ref)
            dgs_ref[...] = jnp.zeros_like(dgs_ref)

        dgp_ref[...] += dgp
        dgs_ref[...] += dgs

    col1 = pl.BlockSpec((tm, 1), lambda i: (i, 0))
    return pl.pallas_call(
        body, name=name, grid=(N // tm,),
        in_specs=[pl.BlockSpec((tm, PW + SW), lambda i: (i, 0)), pl.BlockSpec((tm, PW), lambda i: (i, 0)),
                  pl.BlockSpec((tm, SW), lambda i: (i, 0)), col1, col1,
                  pl.BlockSpec((1, PW), lambda i: (0, 0)), pl.BlockSpec((1, SW), lambda i: (0, 0))],
        out_specs=[pl.BlockSpec((tm, PW), lambda i: (i, 0)), pl.BlockSpec((tm, SW), lambda i: (i, 0)),
                   pl.BlockSpec((1, PW), lambda i: (0, 0)), pl.BlockSpec((1, SW), lambda i: (0, 0))],
        out_shape=[jax.ShapeDtypeStruct((N, PW), F32), jax.ShapeDtypeStruct((N, SW), F32),
                   jax.ShapeDtypeStruct((1, PW), F32), jax.ShapeDtypeStruct((1, SW), F32)],
        compiler_params=_cp(1),
    )(dmerged, y_pool, y_ssm, r_pool, r_ssm, g_pool, g_ssm)


def pool_fwd(name, z, w_pool, scale, n_seq, seq):
    G, Cg, _ = w_pool.shape
    N = n_seq * seq
    PW = G * Cg

    def body(z_ref, w_ref, s_ref, d_ref, y_ref, zs_ref):
        g = pl.program_id(0)
        zv = z_ref[...]
        zs_ref[pl.ds(0, POOL_HALO), :] = jnp.zeros((POOL_HALO, Cg), F32)
        zs_ref[pl.ds(POOL_HALO, seq), :] = zv
        t = lax.broadcasted_iota(jnp.int32, (seq, 1), 0)
        for gi, w in enumerate(POOL_WINDOWS):
            @pl.when(g == gi)
            def _(w=w):
                acc = zv
                for j in range(1, w):
                    acc = acc + zs_ref[pl.ds(POOL_HALO - j, seq), :]
                cnt = jnp.minimum(t + 1, w).astype(F32)
                d = (acc / cnt - zv).astype(d_ref.dtype)
                d_ref[...] = d
                out = jnp.dot(d, w_ref[...], preferred_element_type=F32)
                y_ref[...] = out * s_ref[...]

    blk = pl.BlockSpec((seq, Cg), lambda g, b: (b, g))
    return pl.pallas_call(
        body, name=name, grid=(G, n_seq),
        in_specs=[blk, pl.BlockSpec((None, Cg, Cg), lambda g, b: (g, 0, 0)),
                  pl.BlockSpec((1, Cg), lambda g, b: (0, g))],
        out_specs=[blk, blk],
        out_shape=[jax.ShapeDtypeStruct((N, PW), CDT), jax.ShapeDtypeStruct((N, PW), F32)],
        scratch_shapes=[pltpu.VMEM((seq + POOL_HALO, Cg), F32)],
        compiler_params=_cp(2),
    )(z, w_pool, scale)


def pool_bwd(name, dy_pool, d, w_pool, scale, n_seq, seq):
    G, Cg, _ = w_pool.shape
    N = n_seq * seq
    PW = G * Cg

    def body(dy_ref, d_ref, w_ref, s_ref, dz_ref, dw_ref, ds_ref, es_ref):
        g = pl.program_id(0)
        b = pl.program_id(1)
        dv = d_ref[...]
        wv = w_ref[...]
        dy = dy_ref[...]
        out = jnp.dot(dv, wv, preferred_element_type=F32)
        dout = (dy * s_ref[...]).astype(CDT)
        dw = lax.dot_general(dv, dout, (((0,), (0,)), ((), ())), preferred_element_type=F32)
        dd = lax.dot_general(dout, wv, (((1,), (1,)), ((), ())), preferred_element_type=F32)

        @pl.when(b == 0)
        def _():
            dw_ref[...] = jnp.zeros_like(dw_ref)
            ds_ref[...] = jnp.zeros_like(ds_ref)

        dw_ref[...] += dw
        ds_ref[...] += jnp.sum(out * dy, axis=0, keepdims=True)
        t = lax.broadcasted_iota(jnp.int32, (seq, 1), 0)
        es_ref[pl.ds(seq, POOL_HALO), :] = jnp.zeros((POOL_HALO, Cg), F32)
        for gi, w in enumerate(POOL_WINDOWS):
            @pl.when(g == gi)
            def _(w=w):
                cnt = jnp.minimum(t + 1, w).astype(F32)
                e = dd / cnt
                es_ref[pl.ds(0, seq), :] = e
                acc = e
                for j in range(1, w):
                    acc = acc + es_ref[pl.ds(j, seq), :]
                dz_ref[...] = (acc - dd).astype(dz_ref.dtype)

    blk = pl.BlockSpec((seq, Cg), lambda g, b: (b, g))
    return pl.pallas_call(
        body, name=name, grid=(G, n_seq),
        in_specs=[blk, blk, pl.BlockSpec((None, Cg, Cg), lambda g, b: (g, 0, 0)),
                  pl.BlockSpec((1, Cg), lambda g, b: (0, g))],
        out_specs=[blk, pl.BlockSpec((None, Cg, Cg), lambda g, b: (g, 0, 0)),
                   pl.BlockSpec((1, Cg), lambda g, b: (0, g))],
        out_shape=[jax.ShapeDtypeStruct((N, PW), CDT), jax.ShapeDtypeStruct((G, Cg, Cg), F32),
                   jax.ShapeDtypeStruct((1, PW), F32)],
        scratch_shapes=[pltpu.VMEM((seq + POOL_HALO, Cg), F32)],
        compiler_params=_cp(2),
    )(dy_pool, d, w_pool, scale)


def _ssm_param_math(lr, li, ldt, br, bi):
    dt = jnp.exp(ldt)
    mag = jnp.exp(lr * dt)
    ar = mag * jnp.cos(li * dt)
    ai = mag * jnp.sin(li * dt)
    den = lr * lr + li * li
    xr = ar - 1.0
    cr = (xr * lr + ai * li) / den
    ci = (ai * lr - xr * li) / den
    return ar, ai, cr * br - ci * bi, cr * bi + ci * br


def ssm_params_fwd(name, lr, li, ldt, br, bi):
    GP, H = br.shape

    def body(lr_ref, li_ref, ldt_ref, br_ref, bi_ref, ar_ref, ai_ref, bbr_ref, bbi_ref):
        ar, ai, bbr, bbi = _ssm_param_math(lr_ref[...], li_ref[...], ldt_ref[...], br_ref[...], bi_ref[...])
        ar_ref[...] = ar
        ai_ref[...] = ai
        bbr_ref[...] = bbr
        bbi_ref[...] = bbi

    c1 = jax.ShapeDtypeStruct((GP, 1), F32)
    ch = jax.ShapeDtypeStruct((GP, H), F32)
    tr = _tile(GP, 512)
    b1 = pl.BlockSpec((tr, 1), lambda i: (i, 0))
    bh = pl.BlockSpec((tr, H), lambda i: (i, 0))
    return pl.pallas_call(body, name=name, grid=(GP // tr,), in_specs=[b1, b1, b1, bh, bh],
                          out_specs=[b1, b1, bh, bh], out_shape=[c1, c1, ch, ch],
                          compiler_params=_cp(1))(lr, li, ldt, br, bi)


def ssm_params_bwd(name, lr, li, ldt, br, bi, dar, dai, dbbr, dbbi):
    GP, H = br.shape

    def body(lr_ref, li_ref, ldt_ref, br_ref, bi_ref, dar_ref, dai_ref, dbbr_ref, dbbi_ref,
             dlr_ref, dli_ref, dldt_ref, dbr_ref, dbi_ref):
        _, vjp = jax.vjp(_ssm_param_math, lr_ref[...], li_ref[...], ldt_ref[...], br_ref[...], bi_ref[...])
        dlr, dli, dldt, dbr, dbi = vjp((dar_ref[...], dai_ref[...], dbbr_ref[...], dbbi_ref[...]))
        dlr_ref[...] = dlr
        dli_ref[...] = dli
        dldt_ref[...] = dldt
        dbr_ref[...] = dbr
        dbi_ref[...] = dbi

    c1 = jax.ShapeDtypeStruct((GP, 1), F32)
    ch = jax.ShapeDtypeStruct((GP, H), F32)
    tr = _tile(GP, 512)
    b1 = pl.BlockSpec((tr, 1), lambda i: (i, 0))
    bh = pl.BlockSpec((tr, H), lambda i: (i, 0))
    return pl.pallas_call(body, name=name, grid=(GP // tr,), in_specs=[b1, b1, b1, bh, bh, b1, b1, bh, bh],
                          out_specs=[b1, b1, b1, bh, bh], out_shape=[c1, c1, c1, ch, ch],
                          compiler_params=_cp(1))(lr, li, ldt, br, bi, dar, dai, dbbr, dbbi)


def ssm_expand(name, src, col_off, mat, tm):
    N = src.shape[0]
    NC, cw, w2 = mat.shape
    W = w2 // 2
    tm = _tile(N, tm)

    def body(s_ref, m_ref, re_ref, im_ref):
        r = jnp.dot(s_ref[...].astype(CDT), m_ref[...], preferred_element_type=F32)
        re_ref[...] = r[:, :W]
        im_ref[...] = r[:, W:]

    out = pl.BlockSpec((tm, W), lambda i, c: (i, c))
    return pl.pallas_call(
        body, name=name, grid=(N // tm, NC),
        in_specs=[pl.BlockSpec((tm, cw), lambda i, c: (i, c + col_off)),
                  pl.BlockSpec((None, cw, w2), lambda i, c: (c, 0, 0))],
        out_specs=[out, out],
        out_shape=[jax.ShapeDtypeStruct((N, NC * W), F32)] * 2,
        compiler_params=_cp(2),
    )(src, mat)


def ssm_contract(name, re, im, mat, extras, epi, out_dtypes, tm):
    N = re.shape[0]
    NC, w2, cw = mat.shape
    W = w2 // 2
    tm = _tile(N, tm)
    ne = len(extras)

    def body(*refs):
        re_ref, im_ref, m_ref = refs[:3]
        e_refs = refs[3:3 + ne]
        o_refs = refs[3 + ne:]
        acc = jnp.dot(re_ref[...].astype(CDT), m_ref[pl.ds(0, W), :], preferred_element_type=F32)
        acc = acc + jnp.dot(im_ref[...].astype(CDT), m_ref[pl.ds(W, W), :], preferred_element_type=F32)
        outs = epi(acc, [e[...] for e in e_refs])
        for o_ref, o in zip(o_refs, outs):
            o_ref[...] = o.astype(o_ref.dtype)

    e_arrs, e_specs = [], []
    for arr, kind, off in extras:
        e_arrs.append(arr)
        if kind == "mn":
            e_specs.append(pl.BlockSpec((tm, cw), lambda i, c, off=off: (i, c + off)))
        else:
            e_specs.append(pl.BlockSpec((1, cw), lambda i, c, off=off: (0, c + off)))
    blk = pl.BlockSpec((tm, W), lambda i, c: (i, c))
    return pl.pallas_call(
        body, name=name, grid=(N // tm, NC),
        in_specs=[blk, blk, pl.BlockSpec((None, w2, cw), lambda i, c: (c, 0, 0))] + e_specs,
        out_specs=[pl.BlockSpec((tm, cw), lambda i, c: (i, c)) for _ in out_dtypes],
        out_shape=[jax.ShapeDtypeStruct((N, NC * cw), dt) for dt in out_dtypes],
        compiler_params=_cp(2),
    )(re, im, mat, *e_arrs)


def ssm_outer(name, src, col_off, re, im, NC, Wc, tm):
    N = src.shape[0]
    tm = _tile(N, tm)

    def body(s_ref, re_ref, im_ref, o1_ref, o2_ref):
        @pl.when(pl.program_id(1) == 0)
        def _():
            o1_ref[...] = jnp.zeros_like(o1_ref)
            o2_ref[...] = jnp.zeros_like(o2_ref)

        sv = s_ref[...].astype(CDT)
        dn = (((0,), (0,)), ((), ()))
        o1_ref[...] += lax.dot_general(sv, re_ref[...].astype(CDT), dn, preferred_element_type=F32)
        o2_ref[...] += lax.dot_general(sv, im_ref[...].astype(CDT), dn, preferred_element_type=F32)

    blk = pl.BlockSpec((tm, Wc), lambda c, i: (i, c))
    oblk = pl.BlockSpec((None, LANES, Wc), lambda c, i: (c, 0, 0))
    return pl.pallas_call(
        body, name=name, grid=(NC, N // tm),
        in_specs=[pl.BlockSpec((tm, LANES), lambda c, i: (i, c + col_off)), blk, blk],
        out_specs=[oblk, oblk],
        out_shape=[jax.ShapeDtypeStruct((NC, LANES, Wc), F32)] * 2,
        compiler_params=_cp(2),
    )(src, re, im)


def ssm_scan_fwd(name, bur, bui, ar, ai, n_seq, seq, tc):
    N, R, L = bur.shape
    tc = _tile(seq, tc, 1)
    nt = seq // tc

    def body(bur_ref, bui_ref, ar_ref, ai_ref, xr_ref, xi_ref, st_ref):
        @pl.when(pl.program_id(1) == 0)
        def _():
            st_ref[...] = jnp.zeros_like(st_ref)

        a_r = ar_ref[...]
        a_i = ai_ref[...]

        def step(t, carry):
            xr, xi = carry
            nr = a_r * xr - a_i * xi + bur_ref[t]
            ni = a_r * xi + a_i * xr + bui_ref[t]
            xr_ref[t] = nr
            xi_ref[t] = ni
            return nr, ni

        xr, xi = lax.fori_loop(0, tc, step, (st_ref[0], st_ref[1]))
        st_ref[0] = xr
        st_ref[1] = xi

    blk = pl.BlockSpec((tc, R, L), lambda b, j: (b * nt + j, 0, 0))
    par = pl.BlockSpec((R, L), lambda b, j: (0, 0))
    return pl.pallas_call(
        body, name=name, grid=(n_seq, nt),
        in_specs=[blk, blk, par, par], out_specs=[blk, blk],
        out_shape=[jax.ShapeDtypeStruct((N, R, L), F32)] * 2,
        scratch_shapes=[pltpu.VMEM((2, R, L), F32)],
        compiler_params=_cp(2),
    )(bur, bui, ar, ai)


def ssm_scan_bwd(name, gr, gi, xr, xi, ar, ai, n_seq, seq, tc):
    N, R, L = gr.shape
    tc = _tile(seq, tc, 1)
    nt = seq // tc

    def body(gr_ref, gi_ref, xr_ref, xi_ref, ar_ref, ai_ref, lr_ref, li_ref, dar_ref, dai_ref, st_ref):
        b = pl.program_id(0)
        j = pl.program_id(1)

        @pl.when((b == 0) & (j == 0))
        def _():
            dar_ref[...] = jnp.zeros_like(dar_ref)
            dai_ref[...] = jnp.zeros_like(dai_ref)

        @pl.when(j == 0)
        def _():
            st_ref[...] = jnp.zeros_like(st_ref)

        a_r = ar_ref[...]
        a_i = ai_ref[...]

        def step(s, carry):
            t = tc - 1 - s
            lr, li, dr, di = carry
            xrt = xr_ref[t]
            xit = xi_ref[t]
            dr = dr + (lr * xrt + li * xit)
            di = di + (li * xrt - lr * xit)
            nlr = gr_ref[t] + (a_r * lr + a_i * li)
            nli = gi_ref[t] + (a_r * li - a_i * lr)
            lr_ref[t] = nlr
            li_ref[t] = nli
            return nlr, nli, dr, di

        lr, li, dr, di = lax.fori_loop(0, tc, step, (st_ref[0], st_ref[1], dar_ref[...], dai_ref[...]))
        st_ref[0] = lr
        st_ref[1] = li
        dar_ref[...] = dr
        dai_ref[...] = di

    blk = pl.BlockSpec((tc, R, L), lambda b, j: (b * nt + nt - 1 - j, 0, 0))
    par = pl.BlockSpec((R, L), lambda b, j: (0, 0))
    return pl.pallas_call(
        body, name=name, grid=(n_seq, nt),
        in_specs=[blk, blk, blk, blk, par, par], out_specs=[blk, blk, par, par],
        out_shape=[jax.ShapeDtypeStruct((N, R, L), F32)] * 2 + [jax.ShapeDtypeStruct((R, L), F32)] * 2,
        scratch_shapes=[pltpu.VMEM((2, R, L), F32)],
        compiler_params=_cp(2),
    )(gr, gi, xr, xi, ar, ai)


def glu_bwd_pre(name, dy_ssm, y_pre, q):
    N, SW = y_pre.shape
    tm = _tile(N, 128)

    def body(dy_ref, y_ref, q_ref, dq_ref, dyg_ref, db_ref):
        dy = dy_ref[...]
        yg = _gelu(y_ref[...])
        s = jax.nn.sigmoid(q_ref[...])
        dq = dy * yg * (s * (1.0 - s))
        dq_ref[...] = dq.astype(dq_ref.dtype)
        dyg_ref[...] = dy * s

        @pl.when(pl.program_id(0) == 0)
        def _():
            db_ref[...] = jnp.zeros_like(db_ref)

        db_ref[...] += jnp.sum(dq, axis=0, keepdims=True)

    row = pl.BlockSpec((tm, SW), lambda i: (i, 0))
    vec = pl.BlockSpec((1, SW), lambda i: (0, 0))
    return pl.pallas_call(
        body, name=name, grid=(N // tm,), in_specs=[row, row, row], out_specs=[row, row, vec],
        out_shape=[jax.ShapeDtypeStruct((N, SW), CDT), jax.ShapeDtypeStruct((N, SW), F32),
                   jax.ShapeDtypeStruct((1, SW), F32)],
        compiler_params=_cp(1),
    )(dy_ssm, y_pre, q)


def colsum_prod(name, a, b, b_col_off, width):
    N = a.shape[0]
    tm = _tile(N, 128)

    def body(a_ref, b_ref, o_ref):
        @pl.when(pl.program_id(0) == 0)
        def _():
            o_ref[...] = jnp.zeros_like(o_ref)

        o_ref[...] += jnp.sum(a_ref[...] * b_ref[...], axis=0, keepdims=True)

    return pl.pallas_call(
        body, name=name, grid=(N // tm,),
        in_specs=[pl.BlockSpec((tm, width), lambda i: (i, 0)),
                  pl.BlockSpec((tm, width), lambda i: (i, b_col_off))],
        out_specs=pl.BlockSpec((1, width), lambda i: (0, 0)),
        out_shape=jax.ShapeDtypeStruct((1, width), F32),
        compiler_params=_cp(1),
    )(a, b)


def adamw(name, w, g, m, v):
    R, C = w.shape
    tr = _tile(R, max(8, (1 << 18) // C))

    def body(w_ref, g_ref, m_ref, v_ref, d_ref, nm_ref, nv_ref):
        gv = g_ref[...]
        nm = ADAM_B1 * m_ref[...] + (1.0 - ADAM_B1) * gv
        nv = ADAM_B2 * v_ref[...] + (1.0 - ADAM_B2) * jnp.square(gv)
        m_hat = nm / (1.0 - ADAM_B1 ** ADAM_STEP)
        v_hat = nv / (1.0 - ADAM_B2 ** ADAM_STEP)
        d_ref[...] = -ADAM_LR * (m_hat / (jnp.sqrt(v_hat) + ADAM_EPS) + ADAM_WD * w_ref[...])
        nm_ref[...] = nm
        nv_ref[...] = nv

    blk = pl.BlockSpec((tr, C), lambda i: (i, 0))
    sh = jax.ShapeDtypeStruct((R, C), F32)
    return pl.pallas_call(body, name=name, grid=(R // tr,), in_specs=[blk] * 4, out_specs=[blk] * 3,
                          out_shape=[sh] * 3, compiler_params=_cp(1))(w, g, m, v)


def add_halves(name, part, recv, c_idx):
    S4, R, C = part.shape
    h = R // 2
    tr = _tile(h, max(16, (1 << 19) // C), 16)
    nb = h // tr

    def body(c_ref, p_ref, r_ref, o_ref):
        o_ref[...] = (p_ref[...].astype(F32) + r_ref[...].astype(F32)).astype(o_ref.dtype)

    grid_spec = pltpu.PrefetchScalarGridSpec(
        num_scalar_prefetch=1, grid=(S4, nb),
        in_specs=[pl.BlockSpec((None, tr, C), lambda s, i, c_ref: (s, c_ref[0] * nb + i, 0)),
                  pl.BlockSpec((None, tr, C), lambda s, i, c_ref: (s, i, 0))],
        out_specs=pl.BlockSpec((None, tr, C), lambda s, i, c_ref: (s, i, 0)),
    )
    return pl.pallas_call(body, name=name, grid_spec=grid_spec,
                          out_shape=jax.ShapeDtypeStruct((S4, h, C), CDT),
                          compiler_params=_cp(2))(c_idx, part, recv)


def sum_chips(name, t, u, idx):
    _, h, C = t.shape
    tr = _tile(h, max(16, (1 << 18) // C), 16)
    nb = h // tr

    def body(idx_ref, t_ref, u_ref, o_ref):
        acc = t_ref[...].astype(F32)
        for r in range(NCHIP - 1):
            acc = acc + u_ref[r].astype(F32)
        o_ref[...] = acc

    grid_spec = pltpu.PrefetchScalarGridSpec(
        num_scalar_prefetch=1, grid=(nb,),
        in_specs=[pl.BlockSpec((None, tr, C), lambda i, idx_ref: (idx_ref[0], i, 0)),
                  pl.BlockSpec((NCHIP - 1, tr, C), lambda i, idx_ref: (0, i, 0))],
        out_specs=pl.BlockSpec((tr, C), lambda i, idx_ref: (idx_ref[1] * nb + i, 0)),
    )
    return pl.pallas_call(body, name=name, grid_spec=grid_spec,
                          out_shape=jax.ShapeDtypeStruct((2 * h, C), F32), compiler_params=_cp(1))(idx, t, u)


def _place():
    x, y, c = lax.axis_index("x"), lax.axis_index("y"), lax.axis_index("c")
    chips = [(1 - x, y), (x, 1 - y), (1 - x, 1 - y)]
    return x, y, c, chips


_ANY = pl.BlockSpec(memory_space=pl.ANY)


def gather_shards(name, shards):
    n = len(shards)

    def body(*refs):
        ins = refs[:n]
        outs = refs[n:2 * n]
        send_sems, recv_sems = refs[2 * n:]
        x, y, c, chips = _place()
        me = 2 * x + y
        sib = (x, y, 1 - c)
        sends = []
        for k in range(n):
            h = ins[k].shape[0] // 2
            cp = pltpu.make_async_remote_copy(
                src_ref=ins[k], dst_ref=outs[k].at[me], send_sem=send_sems.at[k, 6], recv_sem=recv_sems.at[k, 6],
                device_id=sib, device_id_type=MESH)
            cp.start()
            sends.append(cp)
            for r, (qx, qy) in enumerate(chips):
                cp = pltpu.make_async_remote_copy(
                    src_ref=ins[k].at[pl.ds(c * h, h)], dst_ref=outs[k].at[me, pl.ds(c * h, h)],
                    send_sem=send_sems.at[k, r], recv_sem=recv_sems.at[k, r],
                    device_id=(qx, qy, c), device_id_type=MESH)
                cp.start()
                sends.append(cp)
        for k in range(n):
            h = ins[k].shape[0] // 2
            for r, (qx, qy) in enumerate(chips):
                q = 2 * qx + qy
                region = outs[k].at[q, pl.ds(c * h, h)]
                pltpu.make_async_remote_copy(
                    src_ref=region, dst_ref=region, send_sem=send_sems.at[k, r], recv_sem=recv_sems.at[k, r],
                    device_id=(qx, qy, c), device_id_type=MESH).wait_recv()
                cp = pltpu.make_async_remote_copy(
                    src_ref=region, dst_ref=region, send_sem=send_sems.at[k, 3 + r],
                    recv_sem=recv_sems.at[k, 3 + r], device_id=sib, device_id_type=MESH)
                cp.start()
                sends.append(cp)
        for k in range(n):
            h = ins[k].shape[0] // 2
            for r, (qx, qy) in enumerate(chips):
                q = 2 * qx + qy
                region = outs[k].at[q, pl.ds((1 - c) * h, h)]
                pltpu.make_async_remote_copy(
                    src_ref=region, dst_ref=region, send_sem=send_sems.at[k, 3 + r],
                    recv_sem=recv_sems.at[k, 3 + r], device_id=sib, device_id_type=MESH).wait_recv()
            own = outs[k].at[me]
            pltpu.make_async_remote_copy(
                src_ref=own, dst_ref=own, send_sem=send_sems.at[k, 6], recv_sem=recv_sems.at[k, 6],
                device_id=sib, device_id_type=MESH).wait_recv()
        for cp in sends:
            cp.wait_send()

    return pl.pallas_call(
        body, name=name,
        in_specs=[_ANY] * n, out_specs=[_ANY] * n,
        out_shape=[jax.ShapeDtypeStruct((NCHIP,) + s.shape, s.dtype) for s in shards],
        scratch_shapes=[pltpu.SemaphoreType.DMA((n, 7)), pltpu.SemaphoreType.DMA((n, 7))],
    )(*shards)


def _remote(src, dst, ssem, rsem, dev):
    return pltpu.make_async_remote_copy(src_ref=src, dst_ref=dst, send_sem=ssem, recv_sem=rsem,
                                        device_id=dev, device_id_type=MESH)


def _spec(arrays, out_shapes, aliases, sem_cols, make):
    return dict(arrays=list(arrays), out_shapes=list(out_shapes), aliases=dict(aliases),
                sems=(len(arrays), sem_cols), make=make)


def gather_send_spec(shards):
    n = len(shards)

    def make(ins, outs, ss, rs):
        x, y, c, chips = _place()
        me = 2 * x + y
        sib = (x, y, 1 - c)
        starts, waits = [], []
        for k in range(n):
            h = ins[k].shape[0] // 2
            own = outs[k].at[me]
            full = _remote(ins[k], own, ss.at[k, 3], rs.at[k, 3], sib)
            starts.append(full.start)
            waits += [full.wait_send, _remote(own, own, ss.at[k, 3], rs.at[k, 3], sib).wait_recv]
            for r, (qx, qy) in enumerate(chips):
                cp = _remote(ins[k].at[pl.ds(c * h, h)], outs[k].at[me, pl.ds(c * h, h)],
                             ss.at[k, r], rs.at[k, r], (qx, qy, c))
                region = outs[k].at[2 * qx + qy, pl.ds(c * h, h)]
                starts.append(cp.start)
                waits += [cp.wait_send, _remote(region, region, ss.at[k, r], rs.at[k, r], (qx, qy, c)).wait_recv]
        return starts, waits

    return _spec(shards, [jax.ShapeDtypeStruct((NCHIP,) + s.shape, s.dtype) for s in shards], {}, 4, make)


def gather_pass_spec(bufs):
    n = len(bufs)

    def make(ins, outs, ss, rs):
        x, y, c, chips = _place()
        sib = (x, y, 1 - c)
        starts, waits = [], []
        for k in range(n):
            h = outs[k].shape[1] // 2
            for r, (qx, qy) in enumerate(chips):
                q = 2 * qx + qy
                region = outs[k].at[q, pl.ds(c * h, h)]
                other = outs[k].at[q, pl.ds((1 - c) * h, h)]
                cp = _remote(region, region, ss.at[k, r], rs.at[k, r], sib)
                starts.append(cp.start)
                waits += [cp.wait_send, _remote(other, other, ss.at[k, r], rs.at[k, r], sib).wait_recv]
        return starts, waits

    return _spec(bufs, [jax.ShapeDtypeStruct(b.shape, b.dtype) for b in bufs], {k: k for k in range(n)}, 3, make)


def rs_xh_spec(parts):
    n = len(parts)

    def make(ins, outs, ss, rs):
        x, y, c, _ = _place()
        sib = (x, y, 1 - c)
        starts, waits = [], []
        for k in range(n):
            h = ins[k].shape[1] // 2
            cp = _remote(ins[k].at[:, pl.ds((1 - c) * h, h)], outs[k], ss.at[k, 0], rs.at[k, 0], sib)
            starts.append(cp.start)
            waits.append(cp.wait)
        return starts, waits

    shapes = [jax.ShapeDtypeStruct((p.shape[0], p.shape[1] // 2) + p.shape[2:], p.dtype) for p in parts]
    return _spec(parts, shapes, {}, 1, make)


def rs_xc_spec(sums):
    n = len(sums)

    def make(ins, outs, ss, rs):
        x, y, c, chips = _place()
        starts, waits = [], []
        for k in range(n):
            for r, (qx, qy) in enumerate(chips):
                cp = _remote(ins[k].at[2 * qx + qy], outs[k].at[r], ss.at[k, r], rs.at[k, r], (qx, qy, c))
                starts.append(cp.start)
                waits.append(cp.wait)
        return starts, waits

    return _spec(sums, [jax.ShapeDtypeStruct((NCHIP - 1,) + s.shape[1:], s.dtype) for s in sums], {}, 3, make)


def rs_jh_spec(fulls):
    n = len(fulls)

    def make(ins, outs, ss, rs):
        x, y, c, _ = _place()
        sib = (x, y, 1 - c)
        starts, waits = [], []
        for k in range(n):
            h = outs[k].shape[0] // 2
            mine = outs[k].at[pl.ds(c * h, h)]
            other = outs[k].at[pl.ds((1 - c) * h, h)]
            cp = _remote(mine, mine, ss.at[k, 0], rs.at[k, 0], sib)
            starts.append(cp.start)
            waits += [cp.wait_send, _remote(other, other, ss.at[k, 0], rs.at[k, 0], sib).wait_recv]
        return starts, waits

    return _spec(fulls, [jax.ShapeDtypeStruct(f.shape, f.dtype) for f in fulls], {k: k for k in range(n)}, 1, make)


def all_reduce_small(name, buf):
    R, L = buf.shape
    h = R // 2

    def body(x_ref, o_ref, sib_ref, chip_ref, send_sems, recv_sems):
        x, y, c, chips = _place()
        me = 2 * x + y
        sib = (x, y, 1 - c)
        first = pltpu.make_async_remote_copy(src_ref=x_ref, dst_ref=sib_ref, send_sem=send_sems.at[0],
                                             recv_sem=recv_sems.at[0], device_id=sib, device_id_type=MESH)
        first.start()
        first.wait()
        mine = pl.ds(pl.multiple_of(c * h, 8), h)
        other = pl.ds(pl.multiple_of((1 - c) * h, 8), h)
        chip_ref[me] = x_ref[mine, :] + sib_ref[mine, :]
        cps = []
        for r, (qx, qy) in enumerate(chips):
            cp = pltpu.make_async_remote_copy(
                src_ref=chip_ref.at[me], dst_ref=chip_ref.at[me], send_sem=send_sems.at[1 + r],
                recv_sem=recv_sems.at[1 + r], device_id=(qx, qy, c), device_id_type=MESH)
            cp.start()
            cps.append(cp)
        for r, (qx, qy) in enumerate(chips):
            q = 2 * qx + qy
            pltpu.make_async_remote_copy(
                src_ref=chip_ref.at[q], dst_ref=chip_ref.at[q], send_sem=send_sems.at[1 + r],
                recv_sem=recv_sems.at[1 + r], device_id=(qx, qy, c), device_id_type=MESH).wait_recv()
        for cp in cps:
            cp.wait_send()
        o_ref[mine, :] = ((chip_ref[0] + chip_ref[1]) + chip_ref[2]) + chip_ref[3]
        last = pltpu.make_async_remote_copy(src_ref=o_ref.at[mine], dst_ref=o_ref.at[mine],
                                            send_sem=send_sems.at[4], recv_sem=recv_sems.at[4],
                                            device_id=sib, device_id_type=MESH)
        last.start()
        last.wait_send()
        pltpu.make_async_remote_copy(src_ref=o_ref.at[other], dst_ref=o_ref.at[other],
                                     send_sem=send_sems.at[4], recv_sem=recv_sems.at[4],
                                     device_id=sib, device_id_type=MESH).wait_recv()

    vm = pl.BlockSpec(memory_space=pltpu.VMEM)
    return pl.pallas_call(
        body, name=name, in_specs=[vm], out_specs=vm,
        out_shape=jax.ShapeDtypeStruct((R, L), F32),
        scratch_shapes=[pltpu.VMEM((R, L), F32), pltpu.VMEM((NCHIP, h, L), F32),
                        pltpu.SemaphoreType.DMA((5,)), pltpu.SemaphoreType.DMA((5,))],
        compiler_params=pltpu.CompilerParams(vmem_limit_bytes=VMEM_LIMIT),
    )(buf)


def _ffn_fwd(tag, h, gain, wg3, wu3, wd, up_comm=(), down_comm=None):
    n, rstd = rms_fwd(tag + "_norm", h, gain)

    def epi(accs, extras):
        g, u = accs
        return g, u, _silu(g) * u

    g, u, a, *up_res = mm_nn(tag + "_up", [n], [wg3, wu3], tm=1024, tn=256, b3=True, out_dtypes=(CDT, CDT, CDT),
                             epi=epi, pair_to_acc=[0, 1], comm=up_comm)

    def epi_down(accs, extras):
        return (extras[0] + 0.5 * accs[0],)

    h_out, *down_res = mm_nn(tag + "_down", [a], [wd], tm=1024, tn=512, tk=wd.shape[0] // NCHIP,
                             extras=[(h, "mn", 0)], epi=epi_down, comm=down_comm(up_res) if down_comm else ())
    return h_out, (h, n, rstd, g, u, a), up_res, down_res


def _ffn_bact(tag, saved, wd, dfb, comm=()):
    _, _, _, g, u, _ = saved

    def epi_act(accs, extras):
        da = accs[0]
        gv = extras[0].astype(F32)
        uv = extras[1].astype(F32)
        s = jax.nn.sigmoid(gv)
        return da * uv * (s * (1.0 + gv * (1.0 - s))), da * (gv * s)

    return mm_nt(tag + "_bact", [dfb], [wd], tm=1024, tn=256, extras=[(g, "mn", 0), (u, "mn", 0)],
                 out_dtypes=(CDT, CDT), epi=epi_act, comm=comm)


def _ffn_dwd(tag, saved, dfb, comm=()):
    return mm_tn(tag + "_dwd", [saved[5]], [dfb], ti=512, tj=512, comm=comm)


def _ffn_dwgu(tag, saved, dg, du, comm=()):
    return mm_tn(tag + "_dwgu", [saved[1]], [dg, du], ti=512, tj=256, o3=NCHIP, comm=comm)


def _ffn_dn(tag, dg, du, wg3, wu3, comm=()):
    return mm_nt(tag + "_dn", [dg, du], [wg3, wu3], tm=1024, tn=512, tk=wg3.shape[2] // 2, b3=True, comm=comm)


def _rs3(p):
    return p.reshape(p.shape[0], -1, p.shape[-1])


def _blockdiag(m, gpc):
    G, a, b = m.shape
    nc = G // gpc
    mask = jnp.eye(gpc, dtype=m.dtype)[None, :, None, :, None]
    out = m.reshape(nc, gpc, a, 1, b) * mask
    return out.reshape(nc, gpc * a, gpc * b)


def _diag_blocks(o, gpc, a, b):
    nc = o.shape[0]
    o5 = o.reshape(nc, gpc, a, gpc, b)
    mask = jnp.eye(gpc, dtype=o.dtype)[None, :, None, :, None]
    return jnp.sum(o5 * mask, axis=3).reshape(nc * gpc, a, b)


def kernel(x, ffn1_norm, ffn1_gate, ffn1_up, ffn1_down, mix_norm, w_in, w_pool, pool_scale, lam_re, lam_im, log_dt, b_re, b_im, c_re, c_im, d_skip, w_glu, b_glu, pool_out_norm, ssm_out_norm, w_out, ffn2_norm, ffn2_gate, ffn2_up, ffn2_down, final_norm, loss_target, m_ffn1_norm, m_ffn1_gate, m_ffn1_up, m_ffn1_down, m_mix_norm, m_w_in, m_w_pool, m_pool_scale, m_lam_re, m_lam_im, m_log_dt, m_b_re, m_b_im, m_c_re, m_c_im, m_d_skip, m_w_glu, m_b_glu, m_pool_out_norm, m_ssm_out_norm, m_w_out, m_ffn2_norm, m_ffn2_gate, m_ffn2_up, m_ffn2_down, m_final_norm, v_ffn1_norm, v_ffn1_gate, v_ffn1_up, v_ffn1_down, v_mix_norm, v_w_in, v_w_pool, v_pool_scale, v_lam_re, v_lam_im, v_log_dt, v_b_re, v_b_im, v_c_re, v_c_im, v_d_skip, v_w_glu, v_b_glu, v_pool_out_norm, v_ssm_out_norm, v_w_out, v_ffn2_norm, v_ffn2_gate, v_ffn2_up, v_ffn2_down, v_final_norm):
    weights = dict(ffn1_norm=ffn1_norm, ffn1_gate=ffn1_gate, ffn1_up=ffn1_up, ffn1_down=ffn1_down, mix_norm=mix_norm, w_in=w_in, w_pool=w_pool, pool_scale=pool_scale, lam_re=lam_re, lam_im=lam_im, log_dt=log_dt, b_re=b_re, b_im=b_im, c_re=c_re, c_im=c_im, d_skip=d_skip, w_glu=w_glu, b_glu=b_glu, pool_out_norm=pool_out_norm, ssm_out_norm=ssm_out_norm, w_out=w_out, ffn2_norm=ffn2_norm, ffn2_gate=ffn2_gate, ffn2_up=ffn2_up, ffn2_down=ffn2_down, final_norm=final_norm)
    moms = dict(ffn1_norm=(m_ffn1_norm, v_ffn1_norm), ffn1_gate=(m_ffn1_gate, v_ffn1_gate), ffn1_up=(m_ffn1_up, v_ffn1_up), ffn1_down=(m_ffn1_down, v_ffn1_down), mix_norm=(m_mix_norm, v_mix_norm), w_in=(m_w_in, v_w_in), w_pool=(m_w_pool, v_w_pool), pool_scale=(m_pool_scale, v_pool_scale), lam_re=(m_lam_re, v_lam_re), lam_im=(m_lam_im, v_lam_im), log_dt=(m_log_dt, v_log_dt), b_re=(m_b_re, v_b_re), b_im=(m_b_im, v_b_im), c_re=(m_c_re, v_c_re), c_im=(m_c_im, v_c_im), d_skip=(m_d_skip, v_d_skip), w_glu=(m_w_glu, v_w_glu), b_glu=(m_b_glu, v_b_glu), pool_out_norm=(m_pool_out_norm, v_pool_out_norm), ssm_out_norm=(m_ssm_out_norm, v_ssm_out_norm), w_out=(m_w_out, v_w_out), ffn2_norm=(m_ffn2_norm, v_ffn2_norm), ffn2_gate=(m_ffn2_gate, v_ffn2_gate), ffn2_up=(m_ffn2_up, v_ffn2_up), ffn2_down=(m_ffn2_down, v_ffn2_down), final_norm=(m_final_norm, v_final_norm))
    names = list(weights)

    n_seq, seq, D = x.shape
    N = n_seq * seq
    Fs = ffn1_gate.shape[1]
    Fps = -(-Fs // FF_ALIGN) * FF_ALIGN
    G, _, Cg = w_pool.shape
    PW = G * Cg
    SW = d_skip.shape[0]
    SG, P = lam_re.shape
    H = SSM_H
    gpc = LANES // H
    NC = SG // gpc
    W = gpc * P
    GP = SG * P
    place_idx = jnp.stack([2 * lax.axis_index("x") + lax.axis_index("y"), lax.axis_index("c")]).astype(jnp.int32)

    row = lambda v: v.reshape(1, -1)
    xf = x.reshape(N, D)
    tgt = loss_target.reshape(N, D)

    pad_c = lambda w: jnp.pad(w.astype(CDT), ((0, 0), (0, Fps - Fs)))
    pad_r = lambda w: jnp.pad(w.astype(CDT), ((0, Fps - Fs), (0, 0)))
    wg1, wu1, wd1 = gather_shards("gather_ffn1", [pad_c(ffn1_gate), pad_c(ffn1_up), pad_r(ffn1_down)])
    wd1 = wd1.reshape(NCHIP * Fps, D)
    mix_shards = [w_in.astype(CDT), w_out.astype(CDT), w_glu.astype(CDT), w_pool.astype(CDT)]
    f2_shards = [pad_c(ffn2_gate), pad_c(ffn2_up), pad_r(ffn2_down)]

    h1, saved1, _, down_res = _ffn_fwd(
        "ffn1", xf, row(ffn1_norm), wg1, wu1, wd1, up_comm=[gather_send_spec(mix_shards + f2_shards[:1])],
        down_comm=lambda sent: [gather_pass_spec(sent), gather_send_spec(f2_shards[1:2])])
    w_in_f, w_out_f, w_glu_f, w_pool_f, wg2, sent_u2 = down_res
    w_in_f = w_in_f.reshape(D, PW + SW)
    w_out_f = w_out_f.reshape(PW + SW, D)
    w_glu_f = w_glu_f.reshape(SW, SW)
    w_pool_f = jnp.swapaxes(w_pool_f, 0, 1).reshape(G, Cg, Cg)
    n2, rstd2 = rms_fwd("mix_norm", h1, row(mix_norm))
    z, wu2, sent_d2 = mm_nn("mix_in", [n2], [w_in_f], tm=1024, tn=256,
                            comm=[gather_pass_spec([sent_u2]), gather_send_spec(f2_shards[2:])])

    d_pool, y_pool = pool_fwd("pool_fwd", z, w_pool_f, row(pool_scale), n_seq, seq)

    col = lambda v: v.reshape(GP, 1)
    lr_c, li_c = col(lam_re), col(lam_im)
    ldt_c = col(jnp.broadcast_to(log_dt[:, None], (SG, P)))
    br_c, bi_c = b_re.reshape(GP, H), b_im.reshape(GP, H)
    ar, ai, bbr, bbi = ssm_params_fwd("ssm_params", lr_c, li_c, ldt_c, br_c, bi_c)
    ar2, ai2 = ar.reshape(GP // LANES, LANES), ai.reshape(GP // LANES, LANES)
    bbr_t = jnp.swapaxes(bbr.reshape(SG, P, H), 1, 2)
    bbi_t = jnp.swapaxes(bbi.reshape(SG, P, H), 1, 2)
    b_cat = jnp.concatenate([_blockdiag(bbr_t, gpc), _blockdiag(bbi_t, gpc)], axis=2).astype(CDT)
    b_cat_t = jnp.swapaxes(b_cat, 1, 2)
    c_cat_t = jnp.concatenate([_blockdiag(c_re, gpc), _blockdiag(-c_im, gpc)], axis=2).astype(CDT)
    c_cat = jnp.swapaxes(c_cat_t, 1, 2)

    u_off = PW // LANES
    bur, bui = ssm_expand("ssm_bu", z, u_off, b_cat, SSM_ROWS)
    v3 = lambda a: a.reshape(N, GP // LANES, LANES)
    xs_r, xs_i = ssm_scan_fwd("ssm_scan", v3(bur), v3(bui), ar2, ai2, n_seq, seq, 64)
    xs_r, xs_i = xs_r.reshape(N, GP), xs_i.reshape(N, GP)

    def epi_y(acc, extras):
        y = acc + extras[1] * extras[0]
        return y, _gelu(y)

    y_pre, yg = ssm_contract("ssm_y", xs_r, xs_i, c_cat, [(z, "mn", u_off), (row(d_skip), "n", 0)], epi_y,
                             (F32, CDT), SSM_ROWS)

    def epi_glu(accs, extras):
        q = accs[0] + extras[1]
        return q, _gelu(extras[0]) * jax.nn.sigmoid(q)

    q_glu, y_ssm, wd2 = mm_nn("glu", [yg], [w_glu_f], tm=1024, tn=256,
                              extras=[(y_pre, "mn", 0), (row(b_glu), "n", 0)], out_dtypes=(F32, F32), epi=epi_glu,
                              comm=[gather_pass_spec([sent_d2])])
    wd2 = wd2.reshape(NCHIP * Fps, D)
    merged, rstd_p, rstd_s = merge_norm_fwd("merge_norm", y_pool, y_ssm, row(pool_out_norm), row(ssm_out_norm))

    def epi_res(accs, extras):
        return (extras[0] + accs[0],)

    (h2,) = mm_nn("mix_out", [merged], [w_out_f], tm=1024, tn=256, extras=[(h1, "mn", 0)], epi=epi_res)
    h3, saved2, _, _ = _ffn_fwd("ffn2", h2, row(ffn2_norm), wg2, wu2, wd2)

    half = place_idx[1:]
    loss_acc, dh3, dfb3, g_final = loss_head("loss_head", h3, row(final_norm), tgt)
    dg2, du2 = _ffn_bact("ffn2", saved2, wd2, dfb3)
    (dwd2,) = _ffn_dwd("ffn2", saved2, dfb3)
    dwg2, dwu2 = _ffn_dwgu("ffn2", saved2, dg2, du2)
    parts_f2 = [dwg2, dwu2, dwd2.reshape(NCHIP, Fps, D)]
    dn_f2, *recv_f2 = _ffn_dn("ffn2", dg2, du2, wg2, wu2, comm=[rs_xh_spec(parts_f2)])
    dh2, dh2c, g_ffn2n = rms_bwd("ffn2_bnorm", dn_f2, saved2[0], saved2[2], row(ffn2_norm), dh3, 1.0)
    sums_f2 = [add_halves(f"rs_add_f2_{k}", p, r, half) for k, (p, r) in enumerate(zip(parts_f2, recv_f2))]

    (dmerged,) = mm_nt("mix_out_bx", [dh2c], [w_out_f], tm=1024, tn=256)
    (dw_out,) = mm_tn("mix_out_bw", [merged], [dh2c], ti=512, tj=512)
    dy_pool, dy_ssm, g_pon, g_son = merge_norm_bwd("merge_norm_b", dmerged, y_pool, y_ssm, rstd_p, rstd_s,
                                                   row(pool_out_norm), row(ssm_out_norm))
    dq, dyg1, g_bglu = glu_bwd_pre("glu_b_pre", dy_ssm, y_pre, q_glu)

    def epi_dyg(accs, extras):
        return ((accs[0] + extras[0]) * _gelu_grad(extras[1]),)

    (dy_pre,) = mm_nt("glu_bx", [dq], [w_glu_f], tm=1024, tn=256, extras=[(dyg1, "mn", 0), (y_pre, "mn", 0)],
                      epi=epi_dyg)
    (dw_glu,) = mm_tn("glu_bw", [yg], [dq], ti=512, tj=512)
    g_dskip = colsum_prod("dskip", dy_pre, z, PW // SW, SW)
    gxr, gxi = ssm_expand("ssm_by", dy_pre, 0, c_cat_t, SSM_ROWS)
    dc_r, dc_i = ssm_outer("ssm_dc", dy_pre, 0, xs_r, xs_i, NC, W, SSM_ROWS)
    lam_r, lam_i, dar, dai = ssm_scan_bwd("ssm_scan_b", v3(gxr), v3(gxi), v3(xs_r), v3(xs_i), ar2, ai2,
                                          n_seq, seq, 64)
    lam_r, lam_i = lam_r.reshape(N, GP), lam_i.reshape(N, GP)
    db_r, db_i = ssm_outer("ssm_db", z, u_off, lam_r, lam_i, NC, W, SSM_ROWS)

    def epi_du(acc, extras):
        return (acc + extras[0] * extras[1],)

    (du_ssm,) = ssm_contract("ssm_bu_b", lam_r, lam_i, b_cat_t, [(dy_pre, "mn", 0), (row(d_skip), "n", 0)],
                             epi_du, (CDT,), SSM_ROWS)
    g_c_re = _diag_blocks(dc_r, gpc, H, P)
    g_c_im = -_diag_blocks(dc_i, gpc, H, P)
    dbbr = jnp.swapaxes(_diag_blocks(db_r, gpc, H, P), 1, 2).reshape(GP, H)
    dbbi = jnp.swapaxes(_diag_blocks(db_i, gpc, H, P), 1, 2).reshape(GP, H)
    dlr, dli, dldt, dbr, dbi = ssm_params_bwd("ssm_params_b", lr_c, li_c, ldt_c, br_c, bi_c,
                                              dar.reshape(GP, 1), dai.reshape(GP, 1), dbbr, dbbi)
    g_lam_re, g_lam_im = dlr.reshape(SG, P), dli.reshape(SG, P)
    g_log_dt = jnp.sum(dldt.reshape(SG, P), axis=1)
    g_b_re, g_b_im = dbr.reshape(SG, P, H), dbi.reshape(SG, P, H)
    dz_pool, dw_pool, g_pscale = pool_bwd("pool_bwd", dy_pool, d_pool, w_pool_f, row(pool_scale), n_seq, seq)
    dz = jnp.concatenate([dz_pool, du_ssm], axis=1)
    (dn2,) = mm_nt("mix_in_bx", [dz], [w_in_f], tm=1024, tn=256)
    (dw_in,) = mm_tn("mix_in_bw", [n2], [dz], ti=512, tj=512)
    dh1, dh1c, g_mixn = rms_bwd("mix_norm_b", dn2, h1, rstd2, row(mix_norm), dh2, 0.5)
    dwp4 = jnp.swapaxes(dw_pool.astype(CDT).reshape(G, NCHIP, Cg // NCHIP, Cg), 0, 1)
    parts_mx = [_rs3(p) for p in (dw_in.reshape(NCHIP, D // NCHIP, PW + SW), dwp4,
                                  dw_glu.reshape(NCHIP, SW // NCHIP, SW),
                                  dw_out.reshape(NCHIP, (PW + SW) // NCHIP, D))]
    dwd1, slab_g2 = _ffn_dwd("ffn1", saved1, dh1c, comm=[rs_xc_spec(sums_f2[:1])])
    part_d1 = dwd1.reshape(NCHIP, Fps, D)
    dg1, du1, slab_u2, *recv_b = _ffn_bact("ffn1", saved1, wd1, dh1c,
                                           comm=[rs_xc_spec(sums_f2[1:2]), rs_xh_spec(parts_mx + [part_d1])])
    sums_mx = [add_halves(f"rs_add_mx_{k}", p, r, half) for k, (p, r) in enumerate(zip(parts_mx, recv_b[:4]))]
    sum_d1 = add_halves("rs_add_f1_2", part_d1, recv_b[4], half)
    dwg1, dwu1, slab_d2, slab_d1 = _ffn_dwgu("ffn1", saved1, dg1, du1, comm=[rs_xc_spec([sums_f2[2], sum_d1])])
    parts_gu1 = [dwg1, dwu1]
    recv_gu1 = comm_call("rs_xh_f1", [rs_xh_spec(parts_gu1)])
    sums_gu1 = [add_halves(f"rs_add_f1_{k}", p, r, half) for k, (p, r) in enumerate(zip(parts_gu1, recv_gu1))]
    dn_f1, *slabs_c = _ffn_dn("ffn1", dg1, du1, wg1, wu1, comm=[rs_xc_spec(sums_gu1 + sums_mx)])
    grad_x, _, g_ffn1n = rms_bwd("ffn1_bnorm", dn_f1, saved1[0], saved1[2], row(ffn1_norm), dh1, 1.0)

    order = ["ffn1_gate", "ffn1_up", "ffn1_down", "w_in", "w_pool", "w_glu", "w_out", "ffn2_gate", "ffn2_up",
             "ffn2_down"]
    own_sums = sums_gu1 + [sum_d1] + sums_mx + sums_f2
    slabs = slabs_c[:2] + [slab_d1] + slabs_c[2:] + [slab_g2, slab_u2, slab_d2]
    fulls = [sum_chips("rs_sum_" + k, t, u, place_idx) for k, t, u in zip(order, own_sums, slabs)]
    joined = comm_call("rs_join", [rs_jh_spec(fulls)])
    big = dict(zip(order, joined))
    grads = dict(ffn1_gate=big["ffn1_gate"][:, :Fs], ffn1_up=big["ffn1_up"][:, :Fs], ffn1_down=big["ffn1_down"][:Fs],
                 w_in=big["w_in"], w_pool=big["w_pool"].reshape(w_pool.shape), w_glu=big["w_glu"], w_out=big["w_out"],
                 ffn2_gate=big["ffn2_gate"][:, :Fs], ffn2_up=big["ffn2_up"][:, :Fs], ffn2_down=big["ffn2_down"][:Fs])

    small = dict(ffn1_norm=g_ffn1n, mix_norm=g_mixn, pool_scale=g_pscale, lam_re=g_lam_re, lam_im=g_lam_im,
                 log_dt=g_log_dt, b_re=g_b_re, b_im=g_b_im, c_re=g_c_re, c_im=g_c_im, d_skip=g_dskip,
                 b_glu=g_bglu, pool_out_norm=g_pon, ssm_out_norm=g_son, ffn2_norm=g_ffn2n, final_norm=g_final)
    pieces = [jnp.pad(small[k].reshape(-1), (0, (-small[k].size) % LANES)) for k in small]
    pieces.append(loss_acc.reshape(-1))
    flat = jnp.concatenate(pieces)
    rows = -(-flat.size // (16 * LANES)) * 16
    flat = jnp.pad(flat, (0, rows * LANES - flat.size)).reshape(rows, LANES)
    red = all_reduce_small("all_reduce_small", flat).reshape(-1)
    off = 0
    for k in small:
        size = small[k].size
        grads[k] = red[off:off + size].reshape(weights[k].shape)
        off += size + (-size) % LANES
    loss = (0.5 / D) * red[off]

    deltas, new_m, new_v = {}, {}, {}
    for k in names:
        w = weights[k]
        if w.ndim >= 2 and w.shape[-1] >= LANES:
            shape2 = (w.size // w.shape[-1], w.shape[-1])
        elif w.size % LANES == 0:
            shape2 = (w.size // LANES, LANES)
        else:
            shape2 = (1, w.size)
        m, v = moms[k]
        d_, m_, v_ = adamw("adamw_" + k, w.reshape(shape2), grads[k].reshape(shape2), m.reshape(shape2),
                           v.reshape(shape2))
        deltas[k], new_m[k], new_v[k] = d_.reshape(w.shape), m_.reshape(w.shape), v_.reshape(w.shape)

    return (loss, grad_x.reshape(x.shape), *[grads[k] for k in names], *[deltas[k] for k in names],
            *[new_m[k] for k in names], *[new_v[k] for k in names])
```

```python
import functools
import math

import jax
import jax.numpy as jnp
from jax import lax
from jax.experimental import pallas as pl
from jax.experimental.pallas import tpu as pltpu

F32 = jnp.float32
CDT = jnp.bfloat16
NORM_EPS = 1e-6
POOL_WINDOWS = (2, 4, 8, 16)
POOL_HALO = 16
SSM_H = 16
SSM_ROWS = 2048
LANES = 128
FF_ALIGN = 256
NCHIP = 4
VMEM_LIMIT = 48 * 1024 * 1024
ADAM_LR = 0.001
ADAM_B1 = 0.9
ADAM_B2 = 0.999
ADAM_EPS = 1e-08
ADAM_WD = 0.01
ADAM_STEP = 10
MESH = pl.DeviceIdType.MESH


def _cp(n_grid):
    return pltpu.CompilerParams(dimension_semantics=("arbitrary",) * n_grid, vmem_limit_bytes=VMEM_LIMIT)


def _tile(n, pref, align=8):
    if n <= pref:
        return n
    t = (pref // align) * align
    while t >= align:
        if n % t == 0:
            return t
        t -= align
    return n


def _silu(x):
    return x * jax.nn.sigmoid(x)


_GELU_C = math.sqrt(2.0 / math.pi)


def _gelu(x):
    return x * (0.5 * (1.0 + jnp.tanh(_GELU_C * (x + 0.044715 * (x * x * x)))))


def _gelu_grad(x):
    t = jnp.tanh(_GELU_C * (x + 0.044715 * (x * x * x)))
    return 0.5 * (1.0 + t) + 0.5 * x * (1.0 - t * t) * (_GELU_C * (1.0 + 3.0 * 0.044715 * x * x))


def _comm_plumb(specs):
    arrays, out_shapes, aliases, scratch = [], [], {}, []
    for sp in specs:
        for i, j in sp["aliases"].items():
            aliases[len(arrays) + i] = len(out_shapes) + j
        arrays += sp["arrays"]
        out_shapes += sp["out_shapes"]
        scratch += [pltpu.SemaphoreType.DMA(sp["sems"]), pltpu.SemaphoreType.DMA(sp["sems"])]

    def make(in_refs, out_refs, sem_refs):
        starts, waits = [], []
        i0 = o0 = 0
        for n, sp in enumerate(specs):
            ni, no = len(sp["arrays"]), len(sp["out_shapes"])
            s, w = sp["make"](in_refs[i0:i0 + ni], out_refs[o0:o0 + no], sem_refs[2 * n], sem_refs[2 * n + 1])
            starts += s
            waits += w
            i0 += ni
            o0 += no
        return starts, waits

    return arrays, out_shapes, aliases, scratch, make


def comm_call(name, specs):
    arrays, out_shapes, aliases, scratch, make = _comm_plumb(specs)
    ni, no = len(arrays), len(out_shapes)

    def body(*refs):
        starts, waits = make(refs[:ni], refs[ni:ni + no], refs[ni + no:])
        for s in starts:
            s()
        for w in waits:
            w()

    return pl.pallas_call(body, name=name, in_specs=[_ANY] * ni, out_specs=[_ANY] * no, out_shape=out_shapes,
                          input_output_aliases=aliases, scratch_shapes=scratch)(*arrays)


def _mm(name, dn, grid, a_list, a_specs, b_list, b_specs, e_list, e_specs, out_shapes, out_specs,
        acc_shape, pair_to_acc, epi, comm=()):
    na, nb, ne, no = len(a_list), len(b_list), len(e_list), len(out_shapes)
    n_pairs = max(na, nb)
    assert na in (1, n_pairs) and nb in (1, n_pairs) and len(pair_to_acc) == n_pairs
    n_acc = max(pair_to_acc) + 1
    nk = grid[2]
    c_arrays, c_out_shapes, c_aliases, c_scratch, c_make = _comm_plumb(comm)
    nci, nco = len(c_arrays), len(c_out_shapes)
    n_in = na + nb + ne
    n_scr = n_acc if nk > 1 else 0

    def body(*refs):
        a_refs = refs[:na]
        b_refs = refs[na:na + nb]
        e_refs = refs[na + nb:n_in]
        o_refs = refs[n_in + nci:n_in + nci + no]
        acc_refs = refs[n_in + nci + no + nco:n_in + nci + no + nco + n_scr]
        if comm:
            starts, waits = c_make(refs[n_in:n_in + nci], refs[n_in + nci + no:n_in + nci + no + nco],
                                   refs[n_in + nci + no + nco + n_scr:])
            gi, gj, gk = pl.program_id(0), pl.program_id(1), pl.program_id(2)

            @pl.when((gi == 0) & (gj == 0) & (gk == 0))
            def _():
                for s in starts:
                    s()

        prods = [None] * n_acc
        a_vals = [a_ref[...].astype(CDT) for a_ref in a_refs]
        b_vals = [b_ref[...].astype(CDT) for b_ref in b_refs]
        for p in range(n_pairs):
            d = lax.dot_general(a_vals[p if na > 1 else 0], b_vals[p if nb > 1 else 0], dn,
                                preferred_element_type=F32)
            q = pair_to_acc[p]
            prods[q] = d if prods[q] is None else prods[q] + d

        def finish(accs):
            outs = epi(accs, [e[...] for e in e_refs])
            for o_ref, o in zip(o_refs, outs):
                o_ref[...] = o.astype(o_ref.dtype)

        if nk == 1:
            finish(prods)
        else:
            k = pl.program_id(2)

            @pl.when(k == 0)
            def _():
                for acc, p in zip(acc_refs, prods):
                    acc[...] = p

            @pl.when(k > 0)
            def _():
                for acc, p in zip(acc_refs, prods):
                    acc[...] += p

            @pl.when(k == nk - 1)
            def _():
                finish([acc[...] for acc in acc_refs])

        if comm:
            @pl.when((gi == grid[0] - 1) & (gj == grid[1] - 1) & (gk == grid[2] - 1))
            def _():
                for w in waits:
                    w()

    scratch = ([pltpu.VMEM(acc_shape, F32) for _ in range(n_acc)] if nk > 1 else []) + c_scratch
    outs = pl.pallas_call(
        body, name=name, grid=grid,
        in_specs=list(a_specs) + list(b_specs) + list(e_specs) + [_ANY] * nci,
        out_specs=list(out_specs) + [_ANY] * nco, out_shape=list(out_shapes) + c_out_shapes,
        input_output_aliases={n_in + i: no + j for i, j in c_aliases.items()},
        scratch_shapes=scratch, compiler_params=_cp(3),
    )(*a_list, *b_list, *e_list, *c_arrays)
    return outs


def _extra_specs(extras, tm, tn):
    arrs, specs = [], []
    for arr, kind, off in extras:
        arrs.append(arr)
        if kind == "mn":
            specs.append(pl.BlockSpec((tm, tn), lambda i, j, k, off=off: (i, j + off)))
        elif kind == "n":
            specs.append(pl.BlockSpec((1, tn), lambda i, j, k, off=off: (0, j + off)))
        else:
            raise ValueError(kind)
    return arrs, specs


def _first(accs, extras):
    return (accs[0],)


def mm_nn(name, a_list, b_list, *, tm, tn, tk=None, b3=False, extras=(), out_dtypes=(F32,), epi=_first,
          pair_to_acc=None, comm=()):
    M, K = a_list[0].shape
    if b3:
        s4, _, ns = b_list[0].shape
        N = s4 * ns
        tn = _tile(ns, tn, LANES)
        nps = ns // tn
    else:
        N = b_list[0].shape[1]
        tn = _tile(N, tn, LANES)
    tm = _tile(M, tm, 16)
    tk = _tile(K, tk or K, LANES)
    grid = (M // tm, N // tn, K // tk)
    a_specs = [pl.BlockSpec((tm, tk), lambda i, j, k: (i, k)) for _ in a_list]
    if b3:
        b_specs = [pl.BlockSpec((None, tk, tn), lambda i, j, k: (j // nps, k, j % nps)) for _ in b_list]
    else:
        b_specs = [pl.BlockSpec((tk, tn), lambda i, j, k: (k, j)) for _ in b_list]
    e_list, e_specs = _extra_specs(extras, tm, tn)
    out_shapes = [jax.ShapeDtypeStruct((M, N), dt) for dt in out_dtypes]
    out_specs = [pl.BlockSpec((tm, tn), lambda i, j, k: (i, j)) for _ in out_dtypes]
    pair_to_acc = pair_to_acc or [0] * max(len(a_list), len(b_list))
    dn = (((1,), (0,)), ((), ()))
    return _mm(name, dn, grid, a_list, a_specs, b_list, b_specs, e_list, e_specs, out_shapes, out_specs,
               (tm, tn), pair_to_acc, epi, comm)


def mm_nt(name, a_list, b_list, *, tm, tn, tk=None, b3=False, extras=(), out_dtypes=(F32,), epi=_first,
          pair_to_acc=None, comm=()):
    M, K = a_list[0].shape
    if b3:
        s4, N, ks = b_list[0].shape
        tk = _tile(ks, tk or ks, LANES)
        kps = ks // tk
    else:
        N = b_list[0].shape[0]
        tk = _tile(K, tk or K, LANES)
    tm = _tile(M, tm, 16)
    tn = _tile(N, tn, LANES)
    grid = (M // tm, N // tn, K // tk)
    a_specs = [pl.BlockSpec((tm, tk), lambda i, j, k: (i, k)) for _ in a_list]
    if b3:
        b_specs = [pl.BlockSpec((None, tn, tk), lambda i, j, k: (k // kps, j, k % kps)) for _ in b_list]
    else:
        b_specs = [pl.BlockSpec((tn, tk), lambda i, j, k: (j, k)) for _ in b_list]
    e_list, e_specs = _extra_specs(extras, tm, tn)
    out_shapes = [jax.ShapeDtypeStruct((M, N), dt) for dt in out_dtypes]
    out_specs = [pl.BlockSpec((tm, tn), lambda i, j, k: (i, j)) for _ in out_dtypes]
    pair_to_acc = pair_to_acc or [0] * max(len(a_list), len(b_list))
    dn = (((1,), (1,)), ((), ()))
    return _mm(name, dn, grid, a_list, a_specs, b_list, b_specs, e_list, e_specs, out_shapes, out_specs,
               (tm, tn), pair_to_acc, epi, comm)


def mm_tn(name, a_list, b_list, *, ti, tj, out_dtype=CDT, comm=()):
    T, I = a_list[0].shape
    J = b_list[0].shape[1]
    n_pairs = max(len(a_list), len(b_list))
    ti = _tile(I, ti, LANES)
    tj = _tile(J, tj, LANES)
    out_shapes = [jax.ShapeDtypeStruct((I, J), out_dtype) for _ in range(n_pairs)]
    out_specs = [pl.BlockSpec((ti, tj), lambda i, j, k: (i, j)) for _ in range(n_pairs)]
    grid = (I // ti, J // tj, 1)
    a_specs = [pl.BlockSpec((T, ti), lambda i, j, k: (0, i)) for _ in a_list]
    b_specs = [pl.BlockSpec((T, tj), lambda i, j, k: (0, j)) for _ in b_list]
    dn = (((0,), (0,)), ((), ()))
    return _mm(name, dn, grid, a_list, a_specs, b_list, b_specs, [], [], out_shapes, out_specs,
               (ti, tj), list(range(n_pairs)), lambda accs, extras: tuple(accs), comm)


def rms_fwd(name, x, gain):
    N, D = x.shape
    tm = _tile(N, 128)

    def body(x_ref, g_ref, n_ref, r_ref):
        xf = x_ref[...]
        r = lax.rsqrt(jnp.mean(xf * xf, axis=-1, keepdims=True) + NORM_EPS)
        n_ref[...] = ((xf * r) * g_ref[...]).astype(n_ref.dtype)
        r_ref[...] = r

    return pl.pallas_call(
        body, name=name, grid=(N // tm,),
        in_specs=[pl.BlockSpec((tm, D), lambda i: (i, 0)), pl.BlockSpec((1, D), lambda i: (0, 0))],
        out_specs=[pl.BlockSpec((tm, D), lambda i: (i, 0)), pl.BlockSpec((tm, 1), lambda i: (i, 0))],
        out_shape=[jax.ShapeDtypeStruct((N, D), CDT), jax.ShapeDtypeStruct((N, 1), F32)],
        compiler_params=_cp(1),
    )(x, gain)


def _rms_bwd_math(dn, x, r, g):
    xhat = x * r
    dgain = jnp.sum(dn * xhat, axis=0, keepdims=True)
    dxhat = dn * g
    dx = r * (dxhat - xhat * jnp.mean(dxhat * xhat, axis=-1, keepdims=True))
    return dx, dgain


def rms_bwd(name, dn, x, rstd, gain, dres, cast_scale):
    N, D = x.shape
    tm = _tile(N, 128)

    def body(dn_ref, x_ref, r_ref, g_ref, dres_ref, dx_ref, dc_ref, dg_ref):
        dx, dgain = _rms_bwd_math(dn_ref[...], x_ref[...], r_ref[...], g_ref[...])
        tot = dres_ref[...] + dx
        dx_ref[...] = tot
        dc_ref[...] = (cast_scale * tot).astype(dc_ref.dtype)

        @pl.when(pl.program_id(0) == 0)
        def _():
            dg_ref[...] = jnp.zeros_like(dg_ref)

        dg_ref[...] += dgain

    row = pl.BlockSpec((tm, D), lambda i: (i, 0))
    vec = pl.BlockSpec((1, D), lambda i: (0, 0))
    return pl.pallas_call(
        body, name=name, grid=(N // tm,),
        in_specs=[row, row, pl.BlockSpec((tm, 1), lambda i: (i, 0)), vec, row],
        out_specs=[row, row, vec],
        out_shape=[jax.ShapeDtypeStruct((N, D), F32), jax.ShapeDtypeStruct((N, D), CDT),
                   jax.ShapeDtypeStruct((1, D), F32)],
        compiler_params=_cp(1),
    )(dn, x, rstd, gain, dres)


def loss_head(name, h, gain, target):
    N, D = h.shape
    tm = _tile(N, 128)

    def body(h_ref, g_ref, t_ref, l_ref, dh_ref, dc_ref, dg_ref):
        x = h_ref[...]
        g = g_ref[...]
        r = lax.rsqrt(jnp.mean(x * x, axis=-1, keepdims=True) + NORM_EPS)
        err = (x * r) * g - t_ref[...]
        dy = err * (1.0 / D)
        dx, dgain = _rms_bwd_math(dy, x, r, g)
        dh_ref[...] = dx
        dc_ref[...] = (0.5 * dx).astype(dc_ref.dtype)

        @pl.when(pl.program_id(0) == 0)
        def _():
            dg_ref[...] = jnp.zeros_like(dg_ref)
            l_ref[...] = jnp.zeros_like(l_ref)

        dg_ref[...] += dgain
        l_ref[...] += jnp.sum(err * err)

    row = pl.BlockSpec((tm, D), lambda i: (i, 0))
    vec = pl.BlockSpec((1, D), lambda i: (0, 0))
    return pl.pallas_call(
        body, name=name, grid=(N // tm,),
        in_specs=[row, vec, row],
        out_specs=[pl.BlockSpec((1, LANES), lambda i: (0, 0)), row, row, vec],
        out_shape=[jax.ShapeDtypeStruct((1, LANES), F32), jax.ShapeDtypeStruct((N, D), F32),
                   jax.ShapeDtypeStruct((N, D), CDT), jax.ShapeDtypeStruct((1, D), F32)],
        compiler_params=_cp(1),
    )(h, gain, target)


def merge_norm_fwd(name, y_pool, y_ssm, g_pool, g_ssm):
    N, PW = y_pool.shape
    SW = y_ssm.shape[1]
    tm = _tile(N, 128)

    def body(yp_ref, ys_ref, gp_ref, gs_ref, m_ref, rp_ref, rs_ref):
        yp = yp_ref[...]
        ys = ys_ref[...]
        rp = lax.rsqrt(jnp.mean(yp * yp, axis=-1, keepdims=True) + NORM_EPS)
        rs = lax.rsqrt(jnp.mean(ys * ys, axis=-1, keepdims=True) + NORM_EPS)
        m_ref[:, :PW] = ((yp * rp) * gp_ref[...]).astype(m_ref.dtype)
        m_ref[:, PW:] = ((ys * rs) * gs_ref[...]).astype(m_ref.dtype)
        rp_ref[...] = rp
        rs_ref[...] = rs

    return pl.pallas_call(
        body, name=name, grid=(N // tm,),
        in_specs=[pl.BlockSpec((tm, PW), lambda i: (i, 0)), pl.BlockSpec((tm, SW), lambda i: (i, 0)),
                  pl.BlockSpec((1, PW), lambda i: (0, 0)), pl.BlockSpec((1, SW), lambda i: (0, 0))],
        out_specs=[pl.BlockSpec((tm, PW + SW), lambda i: (i, 0)), pl.BlockSpec((tm, 1), lambda i: (i, 0)),
                   pl.BlockSpec((tm, 1), lambda i: (i, 0))],
        out_shape=[jax.ShapeDtypeStruct((N, PW + SW), CDT), jax.ShapeDtypeStruct((N, 1), F32),
                   jax.ShapeDtypeStruct((N, 1), F32)],
        compiler_params=_cp(1),
    )(y_pool, y_ssm, g_pool, g_ssm)


def merge_norm_bwd(name, dmerged, y_pool, y_ssm, r_pool, r_ssm, g_pool, g_ssm):
    N, PW = y_pool.shape
    SW = y_ssm.shape[1]
    tm = _tile(N, 128)

    def body(dm_ref, yp_ref, ys_ref, rp_ref, rs_ref, gp_ref, gs_ref, dyp_ref, dys_ref, dgp_ref, dgs_ref):
        dxp, dgp = _rms_bwd_math(dm_ref[:, :PW], yp_ref[...], rp_ref[...], gp_ref[...])
        dxs, dgs = _rms_bwd_math(dm_ref[:, PW:], ys_ref[...], rs_ref[...], gs_ref[...])
        dyp_ref[...] = dxp
        dys_ref[...] = dxs

        @pl.when(pl.program_id(0) == 0)
        def _():
            dgp_ref[...] = jnp.zeros_like(dgp_ref)
            dgs_ref[...] = jnp.zeros_like(dgs_ref)

        dgp_ref[...] += dgp
        dgs_ref[...] += dgs

    col1 = pl.BlockSpec((tm, 1), lambda i: (i, 0))
    return pl.pallas_call(
        body, name=name, grid=(N // tm,),
        in_specs=[pl.BlockSpec((tm, PW + SW), lambda i: (i, 0)), pl.BlockSpec((tm, PW), lambda i: (i, 0)),
                  pl.BlockSpec((tm, SW), lambda i: (i, 0)), col1, col1,
                  pl.BlockSpec((1, PW), lambda i: (0, 0)), pl.BlockSpec((1, SW), lambda i: (0, 0))],
        out_specs=[pl.BlockSpec((tm, PW), lambda i: (i, 0)), pl.BlockSpec((tm, SW), lambda i: (i, 0)),
                   pl.BlockSpec((1, PW), lambda i: (0, 0)), pl.BlockSpec((1, SW), lambda i: (0, 0))],
        out_shape=[jax.ShapeDtypeStruct((N, PW), F32), jax.ShapeDtypeStruct((N, SW), F32),
                   jax.ShapeDtypeStruct((1, PW), F32), jax.ShapeDtypeStruct((1, SW), F32)],
        compiler_params=_cp(1),
    )(dmerged, y_pool, y_ssm, r_pool, r_ssm, g_pool, g_ssm)


def pool_fwd(name, z, w_pool, scale, n_seq, seq):
    G, Cg, _ = w_pool.shape
    N = n_seq * seq
    PW = G * Cg

    def body(z_ref, w_ref, s_ref, d_ref, y_ref, zs_ref):
        g = pl.program_id(0)
        zv = z_ref[...]
        zs_ref[pl.ds(0, POOL_HALO), :] = jnp.zeros((POOL_HALO, Cg), F32)
        zs_ref[pl.ds(POOL_HALO, seq), :] = zv
        t = lax.broadcasted_iota(jnp.int32, (seq, 1), 0)
        for gi, w in enumerate(POOL_WINDOWS):
            @pl.when(g == gi)
            def _(w=w):
                acc = zv
                for j in range(1, w):
                    acc = acc + zs_ref[pl.ds(POOL_HALO - j, seq), :]
                cnt = jnp.minimum(t + 1, w).astype(F32)
                d = (acc / cnt - zv).astype(d_ref.dtype)
                d_ref[...] = d
                out = jnp.dot(d, w_ref[...], preferred_element_type=F32)
                y_ref[...] = out * s_ref[...]

    blk = pl.BlockSpec((seq, Cg), lambda g, b: (b, g))
    return pl.pallas_call(
        body, name=name, grid=(G, n_seq),
        in_specs=[blk, pl.BlockSpec((None, Cg, Cg), lambda g, b: (g, 0, 0)),
                  pl.BlockSpec((1, Cg), lambda g, b: (0, g))],
        out_specs=[blk, blk],
        out_shape=[jax.ShapeDtypeStruct((N, PW), CDT), jax.ShapeDtypeStruct((N, PW), F32)],
        scratch_shapes=[pltpu.VMEM((seq + POOL_HALO, Cg), F32)],
        compiler_params=_cp(2),
    )(z, w_pool, scale)


def pool_bwd(name, dy_pool, d, w_pool, scale, n_seq, seq):
    G, Cg, _ = w_pool.shape
    N = n_seq * seq
    PW = G * Cg

    def body(dy_ref, d_ref, w_ref, s_ref, dz_ref, dw_ref, ds_ref, es_ref):
        g = pl.program_id(0)
        b = pl.program_id(1)
        dv = d_ref[...]
        wv = w_ref[...]
        dy = dy_ref[...]
        out = jnp.dot(dv, wv, preferred_element_type=F32)
        dout = (dy * s_ref[...]).astype(CDT)
        dw = lax.dot_general(dv, dout, (((0,), (0,)), ((), ())), preferred_element_type=F32)
        dd = lax.dot_general(dout, wv, (((1,), (1,)), ((), ())), preferred_element_type=F32)

        @pl.when(b == 0)
        def _():
            dw_ref[...] = jnp.zeros_like(dw_ref)
            ds_ref[...] = jnp.zeros_like(ds_ref)

        dw_ref[...] += dw
        ds_ref[...] += jnp.sum(out * dy, axis=0, keepdims=True)
        t = lax.broadcasted_iota(jnp.int32, (seq, 1), 0)
        es_ref[pl.ds(seq, POOL_HALO), :] = jnp.zeros((POOL_HALO, Cg), F32)
        for gi, w in enumerate(POOL_WINDOWS):
            @pl.when(g == gi)
            def _(w=w):
                cnt = jnp.minimum(t + 1, w).astype(F32)
                e = dd / cnt
                es_ref[pl.ds(0, seq), :] = e
                acc = e
                for j in range(1, w):
                    acc = acc + es_ref[pl.ds(j, seq), :]
                dz_ref[...] = (acc - dd).astype(dz_ref.dtype)

    blk = pl.BlockSpec((seq, Cg), lambda g, b: (b, g))
    return pl.pallas_call(
        body, name=name, grid=(G, n_seq),
        in_specs=[blk, blk, pl.BlockSpec((None, Cg, Cg), lambda g, b: (g, 0, 0)),
                  pl.BlockSpec((1, Cg), lambda g, b: (0, g))],
        out_specs=[blk, pl.BlockSpec((None, Cg, Cg), lambda g, b: (g, 0, 0)),
                   pl.BlockSpec((1, Cg), lambda g, b: (0, g))],
        out_shape=[jax.ShapeDtypeStruct((N, PW), CDT), jax.ShapeDtypeStruct((G, Cg, Cg), F32),
                   jax.ShapeDtypeStruct((1, PW), F32)],
        scratch_shapes=[pltpu.VMEM((seq + POOL_HALO, Cg), F32)],
        compiler_params=_cp(2),
    )(dy_pool, d, w_pool, scale)


def _ssm_param_math(lr, li, ldt, br, bi):
    dt = jnp.exp(ldt)
    mag = jnp.exp(lr * dt)
    ar = mag * jnp.cos(li * dt)
    ai = mag * jnp.sin(li * dt)
    den = lr * lr + li * li
    xr = ar - 1.0
    cr = (xr * lr + ai * li) / den
    ci = (ai * lr - xr * li) / den
    return ar, ai, cr * br - ci * bi, cr * bi + ci * br


def ssm_params_fwd(name, lr, li, ldt, br, bi):
    GP, H = br.shape

    def body(lr_ref, li_ref, ldt_ref, br_ref, bi_ref, ar_ref, ai_ref, bbr_ref, bbi_ref):
        ar, ai, bbr, bbi = _ssm_param_math(lr_ref[...], li_ref[...], ldt_ref[...], br_ref[...], bi_ref[...])
        ar_ref[...] = ar
        ai_ref[...] = ai
        bbr_ref[...] = bbr
        bbi_ref[...] = bbi

    c1 = jax.ShapeDtypeStruct((GP, 1), F32)
    ch = jax.ShapeDtypeStruct((GP, H), F32)
    tr = _tile(GP, 512)
    b1 = pl.BlockSpec((tr, 1), lambda i: (i, 0))
    bh = pl.BlockSpec((tr, H), lambda i: (i, 0))
    return pl.pallas_call(body, name=name, grid=(GP // tr,), in_specs=[b1, b1, b1, bh, bh],
                          out_specs=[b1, b1, bh, bh], out_shape=[c1, c1, ch, ch],
                          compiler_params=_cp(1))(lr, li, ldt, br, bi)


def ssm_params_bwd(name, lr, li, ldt, br, bi, dar, dai, dbbr, dbbi):
    GP, H = br.shape

    def body(lr_ref, li_ref, ldt_ref, br_ref, bi_ref, dar_ref, dai_ref, dbbr_ref, dbbi_ref,
             dlr_ref, dli_ref, dldt_ref, dbr_ref, dbi_ref):
        _, vjp = jax.vjp(_ssm_param_math, lr_ref[...], li_ref[...], ldt_ref[...], br_ref[...], bi_ref[...])
        dlr, dli, dldt, dbr, dbi = vjp((dar_ref[...], dai_ref[...], dbbr_ref[...], dbbi_ref[...]))
        dlr_ref[...] = dlr
        dli_ref[...] = dli
        dldt_ref[...] = dldt
        dbr_ref[...] = dbr
        dbi_ref[...] = dbi

    c1 = jax.ShapeDtypeStruct((GP, 1), F32)
    ch = jax.ShapeDtypeStruct((GP, H), F32)
    tr = _tile(GP, 512)
    b1 = pl.BlockSpec((tr, 1), lambda i: (i, 0))
    bh = pl.BlockSpec((tr, H), lambda i: (i, 0))
    return pl.pallas_call(body, name=name, grid=(GP // tr,), in_specs=[b1, b1, b1, bh, bh, b1, b1, bh, bh],
                          out_specs=[b1, b1, b1, bh, bh], out_shape=[c1, c1, c1, ch, ch],
                          compiler_params=_cp(1))(lr, li, ldt, br, bi, dar, dai, dbbr, dbbi)


def ssm_expand(name, src, col_off, mat, tm):
    N = src.shape[0]
    NC, cw, w2 = mat.shape
    W = w2 // 2
    tm = _tile(N, tm)

    def body(s_ref, m_ref, re_ref, im_ref):
        r = jnp.dot(s_ref[...].astype(CDT), m_ref[...], preferred_element_type=F32)
        re_ref[...] = r[:, :W]
        im_ref[...] = r[:, W:]

    out = pl.BlockSpec((tm, W), lambda i, c: (i, c))
    return pl.pallas_call(
        body, name=name, grid=(N // tm, NC),
        in_specs=[pl.BlockSpec((tm, cw), lambda i, c: (i, c + col_off)),
                  pl.BlockSpec((None, cw, w2), lambda i, c: (c, 0, 0))],
        out_specs=[out, out],
        out_shape=[jax.ShapeDtypeStruct((N, NC * W), F32)] * 2,
        compiler_params=_cp(2),
    )(src, mat)


def ssm_contract(name, re, im, mat, extras, epi, out_dtypes, tm):
    N = re.shape[0]
    NC, w2, cw = mat.shape
    W = w2 // 2
    tm = _tile(N, tm)
    ne = len(extras)

    def body(*refs):
        re_ref, im_ref, m_ref = refs[:3]
        e_refs = refs[3:3 + ne]
        o_refs = refs[3 + ne:]
        acc = jnp.dot(re_ref[...].astype(CDT), m_ref[pl.ds(0, W), :], preferred_element_type=F32)
        acc = acc + jnp.dot(im_ref[...].astype(CDT), m_ref[pl.ds(W, W), :], preferred_element_type=F32)
        outs = epi(acc, [e[...] for e in e_refs])
        for o_ref, o in zip(o_refs, outs):
            o_ref[...] = o.astype(o_ref.dtype)

    e_arrs, e_specs = [], []
    for arr, kind, off in extras:
        e_arrs.append(arr)
        if kind == "mn":
            e_specs.append(pl.BlockSpec((tm, cw), lambda i, c, off=off: (i, c + off)))
        else:
            e_specs.append(pl.BlockSpec((1, cw), lambda i, c, off=off: (0, c + off)))
    blk = pl.BlockSpec((tm, W), lambda i, c: (i, c))
    return pl.pallas_call(
        body, name=name, grid=(N // tm, NC),
        in_specs=[blk, blk, pl.BlockSpec((None, w2, cw), lambda i, c: (c, 0, 0))] + e_specs,
        out_specs=[pl.BlockSpec((tm, cw), lambda i, c: (i, c)) for _ in out_dtypes],
        out_shape=[jax.ShapeDtypeStruct((N, NC * cw), dt) for dt in out_dtypes],
        compiler_params=_cp(2),
    )(re, im, mat, *e_arrs)


def ssm_outer(name, src, col_off, re, im, NC, Wc, tm):
    N = src.shape[0]
    tm = _tile(N, tm)

    def body(s_ref, re_ref, im_ref, o1_ref, o2_ref):
        @pl.when(pl.program_id(1) == 0)
        def _():
            o1_ref[...] = jnp.zeros_like(o1_ref)
            o2_ref[...] = jnp.zeros_like(o2_ref)

        sv = s_ref[...].astype(CDT)
        dn = (((0,), (0,)), ((), ()))
        o1_ref[...] += lax.dot_general(sv, re_ref[...].astype(CDT), dn, preferred_element_type=F32)
        o2_ref[...] += lax.dot_general(sv, im_ref[...].astype(CDT), dn, preferred_element_type=F32)

    blk = pl.BlockSpec((tm, Wc), lambda c, i: (i, c))
    oblk = pl.BlockSpec((None, LANES, Wc), lambda c, i: (c, 0, 0))
    return pl.pallas_call(
        body, name=name, grid=(NC, N // tm),
        in_specs=[pl.BlockSpec((tm, LANES), lambda c, i: (i, c + col_off)), blk, blk],
        out_specs=[oblk, oblk],
        out_shape=[jax.ShapeDtypeStruct((NC, LANES, Wc), F32)] * 2,
        compiler_params=_cp(2),
    )(src, re, im)


def ssm_scan_fwd(name, bur, bui, ar, ai, n_seq, seq, tc):
    N, R, L = bur.shape
    tc = _tile(seq, tc, 1)
    nt = seq // tc

    def body(bur_ref, bui_ref, ar_ref, ai_ref, xr_ref, xi_ref, st_ref):
        @pl.when(pl.program_id(1) == 0)
        def _():
            st_ref[...] = jnp.zeros_like(st_ref)

        a_r = ar_ref[...]
        a_i = ai_ref[...]

        def step(t, carry):
            xr, xi = carry
            nr = a_r * xr - a_i * xi + bur_ref[t]
            ni = a_r * xi + a_i * xr + bui_ref[t]
            xr_ref[t] = nr
            xi_ref[t] = ni
            return nr, ni

        xr, xi = lax.fori_loop(0, tc, step, (st_ref[0], st_ref[1]))
        st_ref[0] = xr
        st_ref[1] = xi

    blk = pl.BlockSpec((tc, R, L), lambda b, j: (b * nt + j, 0, 0))
    par = pl.BlockSpec((R, L), lambda b, j: (0, 0))
    return pl.pallas_call(
        body, name=name, grid=(n_seq, nt),
        in_specs=[blk, blk, par, par], out_specs=[blk, blk],
        out_shape=[jax.ShapeDtypeStruct((N, R, L), F32)] * 2,
        scratch_shapes=[pltpu.VMEM((2, R, L), F32)],
        compiler_params=_cp(2),
    )(bur, bui, ar, ai)


def ssm_scan_bwd(name, gr, gi, xr, xi, ar, ai, n_seq, seq, tc):
    N, R, L = gr.shape
    tc = _tile(seq, tc, 1)
    nt = seq // tc

    def body(gr_ref, gi_ref, xr_ref, xi_ref, ar_ref, ai_ref, lr_ref, li_ref, dar_ref, dai_ref, st_ref):
        b = pl.program_id(0)
        j = pl.program_id(1)

        @pl.when((b == 0) & (j == 0))
        def _():
            dar_ref[...] = jnp.zeros_like(dar_ref)
            dai_ref[...] = jnp.zeros_like(dai_ref)

        @pl.when(j == 0)
        def _():
            st_ref[...] = jnp.zeros_like(st_ref)

        a_r = ar_ref[...]
        a_i = ai_ref[...]

        def step(s, carry):
            t = tc - 1 - s
            lr, li, dr, di = carry
            xrt = xr_ref[t]
            xit = xi_ref[t]
            dr = dr + (lr * xrt + li * xit)
            di = di + (li * xrt - lr * xit)
            nlr = gr_ref[t] + (a_r * lr + a_i * li)
            nli = gi_ref[t] + (a_r * li - a_i * lr)
            lr_ref[t] = nlr
            li_ref[t] = nli
            return nlr, nli, dr, di

        lr, li, dr, di = lax.fori_loop(0, tc, step, (st_ref[0], st_ref[1], dar_ref[...], dai_ref[...]))
        st_ref[0] = lr
        st_ref[1] = li
        dar_ref[...] = dr
        dai_ref[...] = di

    blk = pl.BlockSpec((tc, R, L), lambda b, j: (b * nt + nt - 1 - j, 0, 0))
    par = pl.BlockSpec((R, L), lambda b, j: (0, 0))
    return pl.pallas_call(
        body, name=name, grid=(n_seq, nt),
        in_specs=[blk, blk, blk, blk, par, par], out_specs=[blk, blk, par, par],
        out_shape=[jax.ShapeDtypeStruct((N, R, L), F32)] * 2 + [jax.ShapeDtypeStruct((R, L), F32)] * 2,
        scratch_shapes=[pltpu.VMEM((2, R, L), F32)],
        compiler_params=_cp(2),
    )(gr, gi, xr, xi, ar, ai)


def glu_bwd_pre(name, dy_ssm, y_pre, q):
    N, SW = y_pre.shape
    tm = _tile(N, 128)

    def body(dy_ref, y_ref, q_ref, dq_ref, dyg_ref, db_ref):
        dy = dy_ref[...]
        yg = _gelu(y_ref[...])
        s = jax.nn.sigmoid(q_ref[...])
        dq = dy * yg * (s * (1.0 - s))
        dq_ref[...] = dq.astype(dq_ref.dtype)
        dyg_ref[...] = dy * s

        @pl.when(pl.program_id(0) == 0)
        def _():
            db_ref[...] = jnp.zeros_like(db_ref)

        db_ref[...] += jnp.sum(dq, axis=0, keepdims=True)

    row = pl.BlockSpec((tm, SW), lambda i: (i, 0))
    vec = pl.BlockSpec((1, SW), lambda i: (0, 0))
    return pl.pallas_call(
        body, name=name, grid=(N // tm,), in_specs=[row, row, row], out_specs=[row, row, vec],
        out_shape=[jax.ShapeDtypeStruct((N, SW), CDT), jax.ShapeDtypeStruct((N, SW), F32),
                   jax.ShapeDtypeStruct((1, SW), F32)],
        compiler_params=_cp(1),
    )(dy_ssm, y_pre, q)


def colsum_prod(name, a, b, b_col_off, width):
    N = a.shape[0]
    tm = _tile(N, 128)

    def body(a_ref, b_ref, o_ref):
        @pl.when(pl.program_id(0) == 0)
        def _():
            o_ref[...] = jnp.zeros_like(o_ref)

        o_ref[...] += jnp.sum(a_ref[...] * b_ref[...], axis=0, keepdims=True)

    return pl.pallas_call(
        body, name=name, grid=(N // tm,),
        in_specs=[pl.BlockSpec((tm, width), lambda i: (i, 0)),
                  pl.BlockSpec((tm, width), lambda i: (i, b_col_off))],
        out_specs=pl.BlockSpec((1, width), lambda i: (0, 0)),
        out_shape=jax.ShapeDtypeStruct((1, width), F32),
        compiler_params=_cp(1),
    )(a, b)


def adamw(name, w, g, m, v):
    R, C = w.shape
    tr = _tile(R, max(8, (1 << 18) // C))

    def body(w_ref, g_ref, m_ref, v_ref, d_ref, nm_ref, nv_ref):
        gv = g_ref[...]
        nm = ADAM_B1 * m_ref[...] + (1.0 - ADAM_B1) * gv
        nv = ADAM_B2 * v_ref[...] + (1.0 - ADAM_B2) * jnp.square(gv)
        m_hat = nm / (1.0 - ADAM_B1 ** ADAM_STEP)
        v_hat = nv / (1.0 - ADAM_B2 ** ADAM_STEP)
        d_ref[...] = -ADAM_LR * (m_hat / (jnp.sqrt(v_hat) + ADAM_EPS) + ADAM_WD * w_ref[...])
        nm_ref[...] = nm
        nv_ref[...] = nv

    blk = pl.BlockSpec((tr, C), lambda i: (i, 0))
    sh = jax.ShapeDtypeStruct((R, C), F32)
    return pl.pallas_call(body, name=name, grid=(R // tr,), in_specs=[blk] * 4, out_specs=[blk] * 3,
                          out_shape=[sh] * 3, compiler_params=_cp(1))(w, g, m, v)


def add_halves(name, part, recv, c_idx):
    S4, R, C = part.shape
    h = R // 2
    tr = _tile(h, max(16, (1 << 19) // C), 16)
    nb = h // tr

    def body(c_ref, p_ref, r_ref, o_ref):
        o_ref[...] = (p_ref[...].astype(F32) + r_ref[...].astype(F32)).astype(o_ref.dtype)

    grid_spec = pltpu.PrefetchScalarGridSpec(
        num_scalar_prefetch=1, grid=(S4, nb),
        in_specs=[pl.BlockSpec((None, tr, C), lambda s, i, c_ref: (s, c_ref[0] * nb + i, 0)),
                  pl.BlockSpec((None, tr, C), lambda s, i, c_ref: (s, i, 0))],
        out_specs=pl.BlockSpec((None, tr, C), lambda s, i, c_ref: (s, i, 0)),
    )
    return pl.pallas_call(body, name=name, grid_spec=grid_spec,
                          out_shape=jax.ShapeDtypeStruct((S4, h, C), CDT),
                          compiler_params=_cp(2))(c_idx, part, recv)


def sum_chips(name, t, u, idx):
    _, h, C = t.shape
    tr = _tile(h, max(16, (1 << 18) // C), 16)
    nb = h // tr

    def body(idx_ref, t_ref, u_ref, o_ref):
        acc = t_ref[...].astype(F32)
        for r in range(NCHIP - 1):
            acc = acc + u_ref[r].astype(F32)
        o_ref[...] = acc

    grid_spec = pltpu.PrefetchScalarGridSpec(
        num_scalar_prefetch=1, grid=(nb,),
        in_specs=[pl.BlockSpec((None, tr, C), lambda i, idx_ref: (idx_ref[0], i, 0)),
                  pl.BlockSpec((NCHIP - 1, tr, C), lambda i, idx_ref: (0, i, 0))],
        out_specs=pl.BlockSpec((tr, C), lambda i, idx_ref: (idx_ref[1] * nb + i, 0)),
    )
    return pl.pallas_call(body, name=name, grid_spec=grid_spec,
                          out_shape=jax.ShapeDtypeStruct((2 * h, C), F32), compiler_params=_cp(1))(idx, t, u)


def _place():
    x, y, c = lax.axis_index("x"), lax.axis_index("y"), lax.axis_index("c")
    chips = [(1 - x, y), (x, 1 - y), (1 - x, 1 - y)]
    return x, y, c, chips


_ANY = pl.BlockSpec(memory_space=pl.ANY)


def gather_shards(name, shards):
    n = len(shards)

    def body(*refs):
        ins = refs[:n]
        outs = refs[n:2 * n]
        send_sems, recv_sems = refs[2 * n:]
        x, y, c, chips = _place()
        me = 2 * x + y
        sib = (x, y, 1 - c)
        sends = []
        for k in range(n):
            h = ins[k].shape[0] // 2
            cp = pltpu.make_async_remote_copy(
                src_ref=ins[k], dst_ref=outs[k].at[me], send_sem=send_sems.at[k, 6], recv_sem=recv_sems.at[k, 6],
                device_id=sib, device_id_type=MESH)
            cp.start()
            sends.append(cp)
            for r, (qx, qy) in enumerate(chips):
                cp = pltpu.make_async_remote_copy(
                    src_ref=ins[k].at[pl.ds(c * h, h)], dst_ref=outs[k].at[me, pl.ds(c * h, h)],
                    send_sem=send_sems.at[k, r], recv_sem=recv_sems.at[k, r],
                    device_id=(qx, qy, c), device_id_type=MESH)
                cp.start()
                sends.append(cp)
        for k in range(n):
            h = ins[k].shape[0] // 2
            for r, (qx, qy) in enumerate(chips):
                q = 2 * qx + qy
                region = outs[k].at[q, pl.ds(c * h, h)]
                pltpu.make_async_remote_copy(
                    src_ref=region, dst_ref=region, send_sem=send_sems.at[k, r], recv_sem=recv_sems.at[k, r],
                    device_id=(qx, qy, c), device_id_type=MESH).wait_recv()
                cp = pltpu.make_async_remote_copy(
                    src_ref=region, dst_ref=region, send_sem=send_sems.at[k, 3 + r],
                    recv_sem=recv_sems.at[k, 3 + r], device_id=sib, device_id_type=MESH)
                cp.start()
                sends.append(cp)
        for k in range(n):
            h = ins[k].shape[0] // 2
            for r, (qx, qy) in enumerate(chips):
                q = 2 * qx + qy
                region = outs[k].at[q, pl.ds((1 - c) * h, h)]
                pltpu.make_async_remote_copy(
                    src_ref=region, dst_ref=region, send_sem=send_sems.at[k, 3 + r],
                    recv_sem=recv_sems.at[k, 3 + r], device_id=sib, device_id_type=MESH).wait_recv()
            own = outs[k].at[me]
            pltpu.make_async_remote_copy(
                src_ref=own, dst_ref=own, send_sem=send_sems.at[k, 6], recv_sem=recv_sems.at[k, 6],
                device_id=sib, device_id_type=MESH).wait_recv()
        for cp in sends:
            cp.wait_send()

    return pl.pallas_call(
        body, name=name,
        in_specs=[_ANY] * n, out_specs=[_ANY] * n,
        out_shape=[jax.ShapeDtypeStruct((NCHIP,) + s.shape, s.dtype) for s in shards],
        scratch_shapes=[pltpu.SemaphoreType.DMA((n, 7)), pltpu.SemaphoreType.DMA((n, 7))],
    )(*shards)


def _remote(src, dst, ssem, rsem, dev):
    return pltpu.make_async_remote_copy(src_ref=src, dst_ref=dst, send_sem=ssem, recv_sem=rsem,
                                        device_id=dev, device_id_type=MESH)


def _spec(arrays, out_shapes, aliases, sem_cols, make):
    return dict(arrays=list(arrays), out_shapes=list(out_shapes), aliases=dict(aliases),
                sems=(len(arrays), sem_cols), make=make)


def gather_send_spec(shards):
    n = len(shards)

    def make(ins, outs, ss, rs):
        x, y, c, chips = _place()
        me = 2 * x + y
        sib = (x, y, 1 - c)
        starts, waits = [], []
        for k in range(n):
            h = ins[k].shape[0] // 2
            own = outs[k].at[me]
            full = _remote(ins[k], own, ss.at[k, 3], rs.at[k, 3], sib)
            starts.append(full.start)
            waits += [full.wait_send, _remote(own, own, ss.at[k, 3], rs.at[k, 3], sib).wait_recv]
            for r, (qx, qy) in enumerate(chips):
                cp = _remote(ins[k].at[pl.ds(c * h, h)], outs[k].at[me, pl.ds(c * h, h)],
                             ss.at[k, r], rs.at[k, r], (qx, qy, c))
                region = outs[k].at[2 * qx + qy, pl.ds(c * h, h)]
                starts.append(cp.start)
                waits += [cp.wait_send, _remote(region, region, ss.at[k, r], rs.at[k, r], (qx, qy, c)).wait_recv]
        return starts, waits

    return _spec(shards, [jax.ShapeDtypeStruct((NCHIP,) + s.shape, s.dtype) for s in shards], {}, 4, make)


def gather_pass_spec(bufs):
    n = len(bufs)

    def make(ins, outs, ss, rs):
        x, y, c, chips = _place()
        sib = (x, y, 1 - c)
        starts, waits = [], []
        for k in range(n):
            h = outs[k].shape[1] // 2
            for r, (qx, qy) in enumerate(chips):
                q = 2 * qx + qy
                region = outs[k].at[q, pl.ds(c * h, h)]
                other = outs[k].at[q, pl.ds((1 - c) * h, h)]
                cp = _remote(region, region, ss.at[k, r], rs.at[k, r], sib)
                starts.append(cp.start)
                waits += [cp.wait_send, _remote(other, other, ss.at[k, r], rs.at[k, r], sib).wait_recv]
        return starts, waits

    return _spec(bufs, [jax.ShapeDtypeStruct(b.shape, b.dtype) for b in bufs], {k: k for k in range(n)}, 3, make)


def rs_xh_spec(parts):
    n = len(parts)

    def make(ins, outs, ss, rs):
        x, y, c, _ = _place()
        sib = (x, y, 1 - c)
        starts, waits = [], []
        for k in range(n):
            h = ins[k].shape[1] // 2
            cp = _remote(ins[k].at[:, pl.ds((1 - c) * h, h)], outs[k], ss.at[k, 0], rs.at[k, 0], sib)
            starts.append(cp.start)
            waits.append(cp.wait)
        return starts, waits

    shapes = [jax.ShapeDtypeStruct((p.shape[0], p.shape[1] // 2) + p.shape[2:], p.dtype) for p in parts]
    return _spec(parts, shapes, {}, 1, make)


def rs_xc_spec(sums):
    n = len(sums)

    def make(ins, outs, ss, rs):
        x, y, c, chips = _place()
        starts, waits = [], []
        for k in range(n):
            for r, (qx, qy) in enumerate(chips):
                cp = _remote(ins[k].at[2 * qx + qy], outs[k].at[r], ss.at[k, r], rs.at[k, r], (qx, qy, c))
                starts.append(cp.start)
                waits.append(cp.wait)
        return starts, waits

    return _spec(sums, [jax.ShapeDtypeStruct((NCHIP - 1,) + s.shape[1:], s.dtype) for s in sums], {}, 3, make)


def rs_jh_spec(fulls):
    n = len(fulls)

    def make(ins, outs, ss, rs):
        x, y, c, _ = _place()
        sib = (x, y, 1 - c)
        starts, waits = [], []
        for k in range(n):
            h = outs[k].shape[0] // 2
            mine = outs[k].at[pl.ds(c * h, h)]
            other = outs[k].at[pl.ds((1 - c) * h, h)]
            cp = _remote(mine, mine, ss.at[k, 0], rs.at[k, 0], sib)
            starts.append(cp.start)
            waits += [cp.wait_send, _remote(other, other, ss.at[k, 0], rs.at[k, 0], sib).wait_recv]
        return starts, waits

    return _spec(fulls, [jax.ShapeDtypeStruct(f.shape, f.dtype) for f in fulls], {k: k for k in range(n)}, 1, make)


def all_reduce_small(name, buf):
    R, L = buf.shape
    h = R // 2

    def body(x_ref, o_ref, sib_ref, chip_ref, send_sems, recv_sems):
        x, y, c, chips = _place()
        me = 2 * x + y
        sib = (x, y, 1 - c)
        first = pltpu.make_async_remote_copy(src_ref=x_ref, dst_ref=sib_ref, send_sem=send_sems.at[0],
                                             recv_sem=recv_sems.at[0], device_id=sib, device_id_type=MESH)
        first.start()
        first.wait()
        mine = pl.ds(pl.multiple_of(c * h, 8), h)
        other = pl.ds(pl.multiple_of((1 - c) * h, 8), h)
        chip_ref[me] = x_ref[mine, :] + sib_ref[mine, :]
        cps = []
        for r, (qx, qy) in enumerate(chips):
            cp = pltpu.make_async_remote_copy(
                src_ref=chip_ref.at[me], dst_ref=chip_ref.at[me], send_sem=send_sems.at[1 + r],
                recv_sem=recv_sems.at[1 + r], device_id=(qx, qy, c), device_id_type=MESH)
            cp.start()
            cps.append(cp)
        for r, (qx, qy) in enumerate(chips):
            q = 2 * qx + qy
            pltpu.make_async_remote_copy(
                src_ref=chip_ref.at[q], dst_ref=chip_ref.at[q], send_sem=send_sems.at[1 + r],
                recv_sem=recv_sems.at[1 + r], device_id=(qx, qy, c), device_id_type=MESH).wait_recv()
        for cp in cps:
            cp.wait_send()
        o_ref[mine, :] = ((chip_ref[0] + chip_ref[1]) + chip_ref[2]) + chip_ref[3]
        last = pltpu.make_async_remote_copy(src_ref=o_ref.at[mine], dst_ref=o_ref.at[mine],
                                            send_sem=send_sems.at[4], recv_sem=recv_sems.at[4],
                                            device_id=sib, device_id_type=MESH)
        last.start()
        last.wait_send()
        pltpu.make_async_remote_copy(src_ref=o_ref.at[other], dst_ref=o_ref.at[other],
                                     send_sem=send_sems.at[4], recv_sem=recv_sems.at[4],
                                     device_id=sib, device_id_type=MESH).wait_recv()

    vm = pl.BlockSpec(memory_space=pltpu.VMEM)
    return pl.pallas_call(
        body, name=name, in_specs=[vm], out_specs=vm,
        out_shape=jax.ShapeDtypeStruct((R, L), F32),
        scratch_shapes=[pltpu.VMEM((R, L), F32), pltpu.VMEM((NCHIP, h, L), F32),
                        pltpu.SemaphoreType.DMA((5,)), pltpu.SemaphoreType.DMA((5,))],
        compiler_params=pltpu.CompilerParams(vmem_limit_bytes=VMEM_LIMIT),
    )(buf)


def _ffn_fwd(tag, h, gain, wg3, wu3, wd, up_comm=(), down_comm=None):
    n, rstd = rms_fwd(tag + "_norm", h, gain)

    def epi(accs, extras):
        g, u = accs
        return g, u, _silu(g) * u

    g, u, a, *up_res = mm_nt(tag + "_up", [n], [wg3, wu3], tm=1024, tn=256, out_dtypes=(CDT, CDT, CDT),
                             epi=epi, pair_to_acc=[0, 1], comm=up_comm)

    def epi_down(accs, extras):
        return (extras[0] + 0.5 * accs[0],)

    h_out, *down_res = mm_nn(tag + "_down", [a], [wd], tm=1024, tn=512, tk=wd.shape[0] // NCHIP,
                             extras=[(h, "mn", 0)], epi=epi_down, comm=down_comm(up_res) if down_comm else ())
    return h_out, (h, n, rstd, g, u, a), up_res, down_res


def _ffn_bact(tag, saved, wd, dfb, comm=()):
    _, _, _, g, u, _ = saved

    def epi_act(accs, extras):
        da = accs[0]
        gv = extras[0].astype(F32)
        uv = extras[1].astype(F32)
        s = jax.nn.sigmoid(gv)
        return da * uv * (s * (1.0 + gv * (1.0 - s))), da * (gv * s)

    return mm_nt(tag + "_bact", [dfb], [wd], tm=1024, tn=256, extras=[(g, "mn", 0), (u, "mn", 0)],
                 out_dtypes=(CDT, CDT), epi=epi_act, comm=comm)


def _ffn_dwd(tag, saved, dfb, comm=()):
    return mm_tn(tag + "_dwd", [saved[5]], [dfb], ti=512, tj=512, comm=comm)


def _ffn_dwgu(tag, saved, dg, du, comm=()):
    return mm_tn(tag + "_dwgu", [dg, du], [saved[1]], ti=512, tj=512, comm=comm)


def _ffn_dn(tag, dg, du, wg3, wu3, comm=()):
    return mm_nn(tag + "_dn", [dg, du], [wg3, wu3], tm=1024, tn=512, tk=wg3.shape[0] // (2 * NCHIP),
                 pair_to_acc=[0, 0], comm=comm)


def _rs3(p):
    return p.reshape(p.shape[0], -1, p.shape[-1])


def _blockdiag(m, gpc):
    G, a, b = m.shape
    nc = G // gpc
    mask = jnp.eye(gpc, dtype=m.dtype)[None, :, None, :, None]
    out = m.reshape(nc, gpc, a, 1, b) * mask
    return out.reshape(nc, gpc * a, gpc * b)


def _diag_blocks(o, gpc, a, b):
    nc = o.shape[0]
    o5 = o.reshape(nc, gpc, a, gpc, b)
    mask = jnp.eye(gpc, dtype=o.dtype)[None, :, None, :, None]
    return jnp.sum(o5 * mask, axis=3).reshape(nc * gpc, a, b)


def kernel(x, ffn1_norm, ffn1_gate, ffn1_up, ffn1_down, mix_norm, w_in, w_pool, pool_scale, lam_re, lam_im, log_dt, b_re, b_im, c_re, c_im, d_skip, w_glu, b_glu, pool_out_norm, ssm_out_norm, w_out, ffn2_norm, ffn2_gate, ffn2_up, ffn2_down, final_norm, loss_target, m_ffn1_norm, m_ffn1_gate, m_ffn1_up, m_ffn1_down, m_mix_norm, m_w_in, m_w_pool, m_pool_scale, m_lam_re, m_lam_im, m_log_dt, m_b_re, m_b_im, m_c_re, m_c_im, m_d_skip, m_w_glu, m_b_glu, m_pool_out_norm, m_ssm_out_norm, m_w_out, m_ffn2_norm, m_ffn2_gate, m_ffn2_up, m_ffn2_down, m_final_norm, v_ffn1_norm, v_ffn1_gate, v_ffn1_up, v_ffn1_down, v_mix_norm, v_w_in, v_w_pool, v_pool_scale, v_lam_re, v_lam_im, v_log_dt, v_b_re, v_b_im, v_c_re, v_c_im, v_d_skip, v_w_glu, v_b_glu, v_pool_out_norm, v_ssm_out_norm, v_w_out, v_ffn2_norm, v_ffn2_gate, v_ffn2_up, v_ffn2_down, v_final_norm):
    weights = dict(ffn1_norm=ffn1_norm, ffn1_gate=ffn1_gate, ffn1_up=ffn1_up, ffn1_down=ffn1_down, mix_norm=mix_norm, w_in=w_in, w_pool=w_pool, pool_scale=pool_scale, lam_re=lam_re, lam_im=lam_im, log_dt=log_dt, b_re=b_re, b_im=b_im, c_re=c_re, c_im=c_im, d_skip=d_skip, w_glu=w_glu, b_glu=b_glu, pool_out_norm=pool_out_norm, ssm_out_norm=ssm_out_norm, w_out=w_out, ffn2_norm=ffn2_norm, ffn2_gate=ffn2_gate, ffn2_up=ffn2_up, ffn2_down=ffn2_down, final_norm=final_norm)
    moms = dict(ffn1_norm=(m_ffn1_norm, v_ffn1_norm), ffn1_gate=(m_ffn1_gate, v_ffn1_gate), ffn1_up=(m_ffn1_up, v_ffn1_up), ffn1_down=(m_ffn1_down, v_ffn1_down), mix_norm=(m_mix_norm, v_mix_norm), w_in=(m_w_in, v_w_in), w_pool=(m_w_pool, v_w_pool), pool_scale=(m_pool_scale, v_pool_scale), lam_re=(m_lam_re, v_lam_re), lam_im=(m_lam_im, v_lam_im), log_dt=(m_log_dt, v_log_dt), b_re=(m_b_re, v_b_re), b_im=(m_b_im, v_b_im), c_re=(m_c_re, v_c_re), c_im=(m_c_im, v_c_im), d_skip=(m_d_skip, v_d_skip), w_glu=(m_w_glu, v_w_glu), b_glu=(m_b_glu, v_b_glu), pool_out_norm=(m_pool_out_norm, v_pool_out_norm), ssm_out_norm=(m_ssm_out_norm, v_ssm_out_norm), w_out=(m_w_out, v_w_out), ffn2_norm=(m_ffn2_norm, v_ffn2_norm), ffn2_gate=(m_ffn2_gate, v_ffn2_gate), ffn2_up=(m_ffn2_up, v_ffn2_up), ffn2_down=(m_ffn2_down, v_ffn2_down), final_norm=(m_final_norm, v_final_norm))
    names = list(weights)

    n_seq, seq, D = x.shape
    N = n_seq * seq
    Fs = ffn1_gate.shape[1]
    Fps = -(-Fs // FF_ALIGN) * FF_ALIGN
    G, _, Cg = w_pool.shape
    PW = G * Cg
    SW = d_skip.shape[0]
    SG, P = lam_re.shape
    H = SSM_H
    gpc = LANES // H
    NC = SG // gpc
    W = gpc * P
    GP = SG * P
    place_idx = jnp.stack([2 * lax.axis_index("x") + lax.axis_index("y"), lax.axis_index("c")]).astype(jnp.int32)

    row = lambda v: v.reshape(1, -1)
    xf = x.reshape(N, D)
    tgt = loss_target.reshape(N, D)

    pad_r = lambda w: jnp.pad(w.astype(CDT), ((0, Fps - Fs), (0, 0)))
    pad_c = lambda w: pad_r(w.T)
    rows4 = lambda w4: w4.reshape(NCHIP * Fps, D)
    wg1, wu1, wd1 = map(rows4, gather_shards("gather_ffn1", [pad_c(ffn1_gate), pad_c(ffn1_up), pad_r(ffn1_down)]))
    mix_shards = [w_in.astype(CDT), w_out.astype(CDT), w_glu.astype(CDT), w_pool.astype(CDT)]
    f2_shards = [pad_c(ffn2_gate), pad_c(ffn2_up), pad_r(ffn2_down)]

    h1, saved1, _, down_res = _ffn_fwd(
        "ffn1", xf, row(ffn1_norm), wg1, wu1, wd1, up_comm=[gather_send_spec(mix_shards + f2_shards[:1])],
        down_comm=lambda sent: [gather_pass_spec(sent), gather_send_spec(f2_shards[1:2])])
    w_in_f, w_out_f, w_glu_f, w_pool_f, wg2, sent_u2 = down_res
    wg2 = rows4(wg2)
    w_in_f = w_in_f.reshape(D, PW + SW)
    w_out_f = w_out_f.reshape(PW + SW, D)
    w_glu_f = w_glu_f.reshape(SW, SW)
    w_pool_f = jnp.swapaxes(w_pool_f, 0, 1).reshape(G, Cg, Cg)
    n2, rstd2 = rms_fwd("mix_norm", h1, row(mix_norm))
    z, wu2, sent_d2 = mm_nn("mix_in", [n2], [w_in_f], tm=1024, tn=256,
                            comm=[gather_pass_spec([sent_u2]), gather_send_spec(f2_shards[2:])])
    wu2 = rows4(wu2)

    d_pool, y_pool = pool_fwd("pool_fwd", z, w_pool_f, row(pool_scale), n_seq, seq)

    col = lambda v: v.reshape(GP, 1)
    lr_c, li_c = col(lam_re), col(lam_im)
    ldt_c = col(jnp.broadcast_to(log_dt[:, None], (SG, P)))
    br_c, bi_c = b_re.reshape(GP, H), b_im.reshape(GP, H)
    ar, ai, bbr, bbi = ssm_params_fwd("ssm_params", lr_c, li_c, ldt_c, br_c, bi_c)
    ar2, ai2 = ar.reshape(GP // LANES, LANES), ai.reshape(GP // LANES, LANES)
    bbr_t = jnp.swapaxes(bbr.reshape(SG, P, H), 1, 2)
    bbi_t = jnp.swapaxes(bbi.reshape(SG, P, H), 1, 2)
    b_cat = jnp.concatenate([_blockdiag(bbr_t, gpc), _blockdiag(bbi_t, gpc)], axis=2).astype(CDT)
    b_cat_t = jnp.swapaxes(b_cat, 1, 2)
    c_cat_t = jnp.concatenate([_blockdiag(c_re, gpc), _blockdiag(-c_im, gpc)], axis=2).astype(CDT)
    c_cat = jnp.swapaxes(c_cat_t, 1, 2)

    u_off = PW // LANES
    bur, bui = ssm_expand("ssm_bu", z, u_off, b_cat, SSM_ROWS)
    v3 = lambda a: a.reshape(N, GP // LANES, LANES)
    xs_r, xs_i = ssm_scan_fwd("ssm_scan", v3(bur), v3(bui), ar2, ai2, n_seq, seq, 64)
    xs_r, xs_i = xs_r.reshape(N, GP), xs_i.reshape(N, GP)

    def epi_y(acc, extras):
        y = acc + extras[1] * extras[0]
        return y, _gelu(y)

    y_pre, yg = ssm_contract("ssm_y", xs_r, xs_i, c_cat, [(z, "mn", u_off), (row(d_skip), "n", 0)], epi_y,
                             (F32, CDT), SSM_ROWS)

    def epi_glu(accs, extras):
        q = accs[0] + extras[1]
        return q, _gelu(extras[0]) * jax.nn.sigmoid(q)

    q_glu, y_ssm, wd2 = mm_nn("glu", [yg], [w_glu_f], tm=1024, tn=256,
                              extras=[(y_pre, "mn", 0), (row(b_glu), "n", 0)], out_dtypes=(F32, F32), epi=epi_glu,
                              comm=[gather_pass_spec([sent_d2])])
    wd2 = rows4(wd2)
    merged, rstd_p, rstd_s = merge_norm_fwd("merge_norm", y_pool, y_ssm, row(pool_out_norm), row(ssm_out_norm))

    def epi_res(accs, extras):
        return (extras[0] + accs[0],)

    (h2,) = mm_nn("mix_out", [merged], [w_out_f], tm=1024, tn=256, extras=[(h1, "mn", 0)], epi=epi_res)
    h3, saved2, _, _ = _ffn_fwd("ffn2", h2, row(ffn2_norm), wg2, wu2, wd2)

    half = place_idx[1:]
    loss_acc, dh3, dfb3, g_final = loss_head("loss_head", h3, row(final_norm), tgt)
    dg2, du2 = _ffn_bact("ffn2", saved2, wd2, dfb3)
    (dwd2,) = _ffn_dwd("ffn2", saved2, dfb3)
    dwg2, dwu2 = _ffn_dwgu("ffn2", saved2, dg2, du2)
    parts_f2 = [p.reshape(NCHIP, Fps, D) for p in (dwg2, dwu2, dwd2)]
    dn_f2, *recv_f2 = _ffn_dn("ffn2", dg2, du2, wg2, wu2, comm=[rs_xh_spec(parts_f2)])
    dh2, dh2c, g_ffn2n = rms_bwd("ffn2_bnorm", dn_f2, saved2[0], saved2[2], row(ffn2_norm), dh3, 1.0)
    sums_f2 = [add_halves(f"rs_add_f2_{k}", p, r, half) for k, (p, r) in enumerate(zip(parts_f2, recv_f2))]

    (dmerged,) = mm_nt("mix_out_bx", [dh2c], [w_out_f], tm=1024, tn=256)
    (dw_out,) = mm_tn("mix_out_bw", [merged], [dh2c], ti=512, tj=512)
    dy_pool, dy_ssm, g_pon, g_son = merge_norm_bwd("merge_norm_b", dmerged, y_pool, y_ssm, rstd_p, rstd_s,
                                                   row(pool_out_norm), row(ssm_out_norm))
    dq, dyg1, g_bglu = glu_bwd_pre("glu_b_pre", dy_ssm, y_pre, q_glu)

    def epi_dyg(accs, extras):
        return ((accs[0] + extras[0]) * _gelu_grad(extras[1]),)

    (dy_pre,) = mm_nt("glu_bx", [dq], [w_glu_f], tm=1024, tn=256, extras=[(dyg1, "mn", 0), (y_pre, "mn", 0)],
                      epi=epi_dyg)
    (dw_glu,) = mm_tn("glu_bw", [yg], [dq], ti=512, tj=512)
    g_dskip = colsum_prod("dskip", dy_pre, z, PW // SW, SW)
    gxr, gxi = ssm_expand("ssm_by", dy_pre, 0, c_cat_t, SSM_ROWS)
    dc_r, dc_i = ssm_outer("ssm_dc", dy_pre, 0, xs_r, xs_i, NC, W, SSM_ROWS)
    lam_r, lam_i, dar, dai = ssm_scan_bwd("ssm_scan_b", v3(gxr), v3(gxi), v3(xs_r), v3(xs_i), ar2, ai2,
                                          n_seq, seq, 64)
    lam_r, lam_i = lam_r.reshape(N, GP), lam_i.reshape(N, GP)
    db_r, db_i = ssm_outer("ssm_db", z, u_off, lam_r, lam_i, NC, W, SSM_ROWS)

    def epi_du(acc, extras):
        return (acc + extras[0] * extras[1],)

    (du_ssm,) = ssm_contract("ssm_bu_b", lam_r, lam_i, b_cat_t, [(dy_pre, "mn", 0), (row(d_skip), "n", 0)],
                             epi_du, (CDT,), SSM_ROWS)
    g_c_re = _diag_blocks(dc_r, gpc, H, P)
    g_c_im = -_diag_blocks(dc_i, gpc, H, P)
    dbbr = jnp.swapaxes(_diag_blocks(db_r, gpc, H, P), 1, 2).reshape(GP, H)
    dbbi = jnp.swapaxes(_diag_blocks(db_i, gpc, H, P), 1, 2).reshape(GP, H)
    dlr, dli, dldt, dbr, dbi = ssm_params_bwd("ssm_params_b", lr_c, li_c, ldt_c, br_c, bi_c,
                                              dar.reshape(GP, 1), dai.reshape(GP, 1), dbbr, dbbi)
    g_lam_re, g_lam_im = dlr.reshape(SG, P), dli.reshape(SG, P)
    g_log_dt = jnp.sum(dldt.reshape(SG, P), axis=1)
    g_b_re, g_b_im = dbr.reshape(SG, P, H), dbi.reshape(SG, P, H)
    dz_pool, dw_pool, g_pscale = pool_bwd("pool_bwd", dy_pool, d_pool, w_pool_f, row(pool_scale), n_seq, seq)
    dz = jnp.concatenate([dz_pool, du_ssm], axis=1)
    (dn2,) = mm_nt("mix_in_bx", [dz], [w_in_f], tm=1024, tn=256)
    (dw_in,) = mm_tn("mix_in_bw", [n2], [dz], ti=512, tj=512)
    dh1, dh1c, g_mixn = rms_bwd("mix_norm_b", dn2, h1, rstd2, row(mix_norm), dh2, 0.5)
    dwp4 = jnp.swapaxes(dw_pool.astype(CDT).reshape(G, NCHIP, Cg // NCHIP, Cg), 0, 1)
    parts_mx = [_rs3(p) for p in (dw_in.reshape(NCHIP, D // NCHIP, PW + SW), dwp4,
                                  dw_glu.reshape(NCHIP, SW // NCHIP, SW),
                                  dw_out.reshape(NCHIP, (PW + SW) // NCHIP, D))]
    dwd1, slab_g2 = _ffn_dwd("ffn1", saved1, dh1c, comm=[rs_xc_spec(sums_f2[:1])])
    part_d1 = dwd1.reshape(NCHIP, Fps, D)
    dg1, du1, slab_u2, *recv_b = _ffn_bact("ffn1", saved1, wd1, dh1c,
                                           comm=[rs_xc_spec(sums_f2[1:2]), rs_xh_spec(parts_mx + [part_d1])])
    sums_mx = [add_halves(f"rs_add_mx_{k}", p, r, half) for k, (p, r) in enumerate(zip(parts_mx, recv_b[:4]))]
    sum_d1 = add_halves("rs_add_f1_2", part_d1, recv_b[4], half)
    dwg1, dwu1, slab_d2, slab_d1 = _ffn_dwgu("ffn1", saved1, dg1, du1, comm=[rs_xc_spec([sums_f2[2], sum_d1])])
    parts_gu1 = [dwg1.reshape(NCHIP, Fps, D), dwu1.reshape(NCHIP, Fps, D)]
    recv_gu1 = comm_call("rs_xh_f1", [rs_xh_spec(parts_gu1)])
    sums_gu1 = [add_halves(f"rs_add_f1_{k}", p, r, half) for k, (p, r) in enumerate(zip(parts_gu1, recv_gu1))]
    dn_f1, *slabs_c = _ffn_dn("ffn1", dg1, du1, wg1, wu1, comm=[rs_xc_spec(sums_gu1 + sums_mx)])
    grad_x, _, g_ffn1n = rms_bwd("ffn1_bnorm", dn_f1, saved1[0], saved1[2], row(ffn1_norm), dh1, 1.0)

    order = ["ffn1_gate", "ffn1_up", "ffn1_down", "w_in", "w_pool", "w_glu", "w_out", "ffn2_gate", "ffn2_up",
             "ffn2_down"]
    own_sums = sums_gu1 + [sum_d1] + sums_mx + sums_f2
    slabs = slabs_c[:2] + [slab_d1] + slabs_c[2:] + [slab_g2, slab_u2, slab_d2]
    fulls = [sum_chips("rs_sum_" + k, t, u, place_idx) for k, t, u in zip(order, own_sums, slabs)]
    joined = comm_call("rs_join", [rs_jh_spec(fulls)])
    big = dict(zip(order, joined))
    transposed = ("ffn1_gate", "ffn1_up", "ffn2_gate", "ffn2_up")
    grads = dict(ffn1_gate=big["ffn1_gate"][:Fs], ffn1_up=big["ffn1_up"][:Fs], ffn1_down=big["ffn1_down"][:Fs],
                 w_in=big["w_in"], w_pool=big["w_pool"].reshape(w_pool.shape), w_glu=big["w_glu"], w_out=big["w_out"],
                 ffn2_gate=big["ffn2_gate"][:Fs], ffn2_up=big["ffn2_up"][:Fs], ffn2_down=big["ffn2_down"][:Fs])

    small = dict(ffn1_norm=g_ffn1n, mix_norm=g_mixn, pool_scale=g_pscale, lam_re=g_lam_re, lam_im=g_lam_im,
                 log_dt=g_log_dt, b_re=g_b_re, b_im=g_b_im, c_re=g_c_re, c_im=g_c_im, d_skip=g_dskip,
                 b_glu=g_bglu, pool_out_norm=g_pon, ssm_out_norm=g_son, ffn2_norm=g_ffn2n, final_norm=g_final)
    pieces = [jnp.pad(small[k].reshape(-1), (0, (-small[k].size) % LANES)) for k in small]
    pieces.append(loss_acc.reshape(-1))
    flat = jnp.concatenate(pieces)
    rows = -(-flat.size // (16 * LANES)) * 16
    flat = jnp.pad(flat, (0, rows * LANES - flat.size)).reshape(rows, LANES)
    red = all_reduce_small("all_reduce_small", flat).reshape(-1)
    off = 0
    for k in small:
        size = small[k].size
        grads[k] = red[off:off + size].reshape(weights[k].shape)
        off += size + (-size) % LANES
    loss = (0.5 / D) * red[off]

    deltas, new_m, new_v = {}, {}, {}
    for k in names:
        w = weights[k]
        m, v = moms[k]
        if k in transposed:
            d_, m_, v_ = adamw("adamw_" + k, w.T, grads[k], m.T, v.T)
            deltas[k], new_m[k], new_v[k], grads[k] = d_.T, m_.T, v_.T, grads[k].T
            continue
        if w.ndim >= 2 and w.shape[-1] >= LANES:
            shape2 = (w.size // w.shape[-1], w.shape[-1])
        elif w.size % LANES == 0:
            shape2 = (w.size // LANES, LANES)
        else:
            shape2 = (1, w.size)
        m, v = moms[k]
        d_, m_, v_ = adamw("adamw_" + k, w.reshape(shape2), grads[k].reshape(shape2), m.reshape(shape2),
                           v.reshape(shape2))
        deltas[k], new_m[k], new_v[k] = d_.reshape(w.shape), m_.reshape(w.shape), v_.reshape(w.shape)

    return (loss, grad_x.reshape(x.shape), *[grads[k] for k in names], *[deltas[k] for k in names],
            *[new_m[k] for k in names], *[new_v[k] for k in names])
```

```python
import functools
import math

import jax
import jax.numpy as jnp
from jax import lax
from jax.experimental import pallas as pl
from jax.experimental.pallas import tpu as pltpu

F32 = jnp.float32
CDT = jnp.bfloat16
NORM_EPS = 1e-6
POOL_WINDOWS = (2, 4, 8, 16)
POOL_HALO = 16
SSM_H = 16
SSM_ROWS = 2048
LANES = 128
FF_ALIGN = 256
NCHIP = 4
VMEM_LIMIT = 48 * 1024 * 1024
ADAM_LR = 0.001
ADAM_B1 = 0.9
ADAM_B2 = 0.999
ADAM_EPS = 1e-08
ADAM_WD = 0.01
ADAM_STEP = 10
MESH = pl.DeviceIdType.MESH


def _cp(n_grid):
    return pltpu.CompilerParams(dimension_semantics=("arbitrary",) * n_grid, vmem_limit_bytes=VMEM_LIMIT)


def _tile(n, pref, align=8):
    if n <= pref:
        return n
    t = (pref // align) * align
    while t >= align:
        if n % t == 0:
            return t
        t -= align
    return n


def _silu(x):
    return x * jax.nn.sigmoid(x)


_GELU_C = math.sqrt(2.0 / math.pi)


def _gelu(x):
    return x * (0.5 * (1.0 + jnp.tanh(_GELU_C * (x + 0.044715 * (x * x * x)))))


def _gelu_grad(x):
    t = jnp.tanh(_GELU_C * (x + 0.044715 * (x * x * x)))
    return 0.5 * (1.0 + t) + 0.5 * x * (1.0 - t * t) * (_GELU_C * (1.0 + 3.0 * 0.044715 * x * x))


def _comm_plumb(specs):
    arrays, out_shapes, aliases, scratch = [], [], {}, []
    for sp in specs:
        for i, j in sp["aliases"].items():
            aliases[len(arrays) + i] = len(out_shapes) + j
        arrays += sp["arrays"]
        out_shapes += sp["out_shapes"]
        scratch += [pltpu.SemaphoreType.DMA(sp["sems"]), pltpu.SemaphoreType.DMA(sp["sems"])]

    def make(in_refs, out_refs, sem_refs):
        starts, waits = [], []
        i0 = o0 = 0
        for n, sp in enumerate(specs):
            ni, no = len(sp["arrays"]), len(sp["out_shapes"])
            s, w = sp["make"](in_refs[i0:i0 + ni], out_refs[o0:o0 + no], sem_refs[2 * n], sem_refs[2 * n + 1])
            starts += s
            waits += w
            i0 += ni
            o0 += no
        return starts, waits

    return arrays, out_shapes, aliases, scratch, make


def comm_call(name, specs):
    arrays, out_shapes, aliases, scratch, make = _comm_plumb(specs)
    ni, no = len(arrays), len(out_shapes)

    def body(*refs):
        starts, waits = make(refs[:ni], refs[ni:ni + no], refs[ni + no:])
        for s in starts:
            s()
        for w in waits:
            w()

    return pl.pallas_call(body, name=name, in_specs=[_ANY] * ni, out_specs=[_ANY] * no, out_shape=out_shapes,
                          input_output_aliases=aliases, scratch_shapes=scratch)(*arrays)


def _mm(name, dn, grid, a_list, a_specs, b_list, b_specs, e_list, e_specs, out_shapes, out_specs,
        acc_shape, pair_to_acc, epi, comm=()):
    na, nb, ne, no = len(a_list), len(b_list), len(e_list), len(out_shapes)
    n_pairs = max(na, nb)
    assert na in (1, n_pairs) and nb in (1, n_pairs) and len(pair_to_acc) == n_pairs
    n_acc = max(pair_to_acc) + 1
    nk = grid[2]
    c_arrays, c_out_shapes, c_aliases, c_scratch, c_make = _comm_plumb(comm)
    nci, nco = len(c_arrays), len(c_out_shapes)
    n_in = na + nb + ne
    n_scr = n_acc if nk > 1 else 0

    def body(*refs):
        a_refs = refs[:na]
        b_refs = refs[na:na + nb]
        e_refs = refs[na + nb:n_in]
        o_refs = refs[n_in + nci:n_in + nci + no]
        acc_refs = refs[n_in + nci + no + nco:n_in + nci + no + nco + n_scr]
        if comm:
            starts, waits = c_make(refs[n_in:n_in + nci], refs[n_in + nci + no:n_in + nci + no + nco],
                                   refs[n_in + nci + no + nco + n_scr:])
            gi, gj, gk = pl.program_id(0), pl.program_id(1), pl.program_id(2)

            @pl.when((gi == 0) & (gj == 0) & (gk == 0))
            def _():
                for s in starts:
                    s()

        prods = [None] * n_acc
        a_vals = [a_ref[...].astype(CDT) for a_ref in a_refs]
        b_vals = [b_ref[...].astype(CDT) for b_ref in b_refs]
        for p in range(n_pairs):
            d = lax.dot_general(a_vals[p if na > 1 else 0], b_vals[p if nb > 1 else 0], dn,
                                preferred_element_type=F32)
            q = pair_to_acc[p]
            prods[q] = d if prods[q] is None else prods[q] + d

        def finish(accs):
            outs = epi(accs, [e[...] for e in e_refs])
            for o_ref, o in zip(o_refs, outs):
                o_ref[...] = o.astype(o_ref.dtype)

        if nk == 1:
            finish(prods)
        else:
            k = pl.program_id(2)

            @pl.when(k == 0)
            def _():
                for acc, p in zip(acc_refs, prods):
                    acc[...] = p

            @pl.when(k > 0)
            def _():
                for acc, p in zip(acc_refs, prods):
                    acc[...] += p

            @pl.when(k == nk - 1)
            def _():
                finish([acc[...] for acc in acc_refs])

        if comm:
            @pl.when((gi == grid[0] - 1) & (gj == grid[1] - 1) & (gk == grid[2] - 1))
            def _():
                for w in waits:
                    w()

    scratch = ([pltpu.VMEM(acc_shape, F32) for _ in range(n_acc)] if nk > 1 else []) + c_scratch
    outs = pl.pallas_call(
        body, name=name, grid=grid,
        in_specs=list(a_specs) + list(b_specs) + list(e_specs) + [_ANY] * nci,
        out_specs=list(out_specs) + [_ANY] * nco, out_shape=list(out_shapes) + c_out_shapes,
        input_output_aliases={n_in + i: no + j for i, j in c_aliases.items()},
        scratch_shapes=scratch, compiler_params=_cp(3),
    )(*a_list, *b_list, *e_list, *c_arrays)
    return outs


def _extra_specs(extras, tm, tn):
    arrs, specs = [], []
    for arr, kind, off in extras:
        arrs.append(arr)
        if kind == "mn":
            specs.append(pl.BlockSpec((tm, tn), lambda i, j, k, off=off: (i, j + off)))
        elif kind == "n":
            specs.append(pl.BlockSpec((1, tn), lambda i, j, k, off=off: (0, j + off)))
        else:
            raise ValueError(kind)
    return arrs, specs


def _first(accs, extras):
    return (accs[0],)


def mm_nn(name, a_list, b_list, *, tm, tn, tk=None, b3=False, extras=(), out_dtypes=(F32,), epi=_first,
          pair_to_acc=None, comm=()):
    M, K = a_list[0].shape
    if b3:
        s4, _, ns = b_list[0].shape
        N = s4 * ns
        tn = _tile(ns, tn, LANES)
        nps = ns // tn
    else:
        N = b_list[0].shape[1]
        tn = _tile(N, tn, LANES)
    tm = _tile(M, tm, 16)
    tk = _tile(K, tk or K, LANES)
    grid = (M // tm, N // tn, K // tk)
    a_specs = [pl.BlockSpec((tm, tk), lambda i, j, k: (i, k)) for _ in a_list]
    if b3:
        b_specs = [pl.BlockSpec((None, tk, tn), lambda i, j, k: (j // nps, k, j % nps)) for _ in b_list]
    else:
        b_specs = [pl.BlockSpec((tk, tn), lambda i, j, k: (k, j)) for _ in b_list]
    e_list, e_specs = _extra_specs(extras, tm, tn)
    out_shapes = [jax.ShapeDtypeStruct((M, N), dt) for dt in out_dtypes]
    out_specs = [pl.BlockSpec((tm, tn), lambda i, j, k: (i, j)) for _ in out_dtypes]
    pair_to_acc = pair_to_acc or [0] * max(len(a_list), len(b_list))
    dn = (((1,), (0,)), ((), ()))
    return _mm(name, dn, grid, a_list, a_specs, b_list, b_specs, e_list, e_specs, out_shapes, out_specs,
               (tm, tn), pair_to_acc, epi, comm)


def mm_nt(name, a_list, b_list, *, tm, tn, tk=None, b3=False, extras=(), out_dtypes=(F32,), epi=_first,
          pair_to_acc=None, comm=()):
    M, K = a_list[0].shape
    if b3:
        s4, N, ks = b_list[0].shape
        tk = _tile(ks, tk or ks, LANES)
        kps = ks // tk
    else:
        N = b_list[0].shape[0]
        tk = _tile(K, tk or K, LANES)
    tm = _tile(M, tm, 16)
    tn = _tile(N, tn, LANES)
    grid = (M // tm, N // tn, K // tk)
    a_specs = [pl.BlockSpec((tm, tk), lambda i, j, k: (i, k)) for _ in a_list]
    if b3:
        b_specs = [pl.BlockSpec((None, tn, tk), lambda i, j, k: (k // kps, j, k % kps)) for _ in b_list]
    else:
        b_specs = [pl.BlockSpec((tn, tk), lambda i, j, k: (j, k)) for _ in b_list]
    e_list, e_specs = _extra_specs(extras, tm, tn)
    out_shapes = [jax.ShapeDtypeStruct((M, N), dt) for dt in out_dtypes]
    out_specs = [pl.BlockSpec((tm, tn), lambda i, j, k: (i, j)) for _ in out_dtypes]
    pair_to_acc = pair_to_acc or [0] * max(len(a_list), len(b_list))
    dn = (((1,), (1,)), ((), ()))
    return _mm(name, dn, grid, a_list, a_specs, b_list, b_specs, e_list, e_specs, out_shapes, out_specs,
               (tm, tn), pair_to_acc, epi, comm)


def mm_tn(name, a_list, b_list, *, ti, tj, out_dtype=CDT, comm=()):
    T, I = a_list[0].shape
    J = b_list[0].shape[1]
    n_pairs = max(len(a_list), len(b_list))
    ti = _tile(I, ti, LANES)
    tj = _tile(J, tj, LANES)
    out_shapes = [jax.ShapeDtypeStruct((I, J), out_dtype) for _ in range(n_pairs)]
    out_specs = [pl.BlockSpec((ti, tj), lambda i, j, k: (i, j)) for _ in range(n_pairs)]
    grid = (I // ti, J // tj, 1)
    a_specs = [pl.BlockSpec((T, ti), lambda i, j, k: (0, i)) for _ in a_list]
    b_specs = [pl.BlockSpec((T, tj), lambda i, j, k: (0, j)) for _ in b_list]
    dn = (((0,), (0,)), ((), ()))
    return _mm(name, dn, grid, a_list, a_specs, b_list, b_specs, [], [], out_shapes, out_specs,
               (ti, tj), list(range(n_pairs)), lambda accs, extras: tuple(accs), comm)


def rms_fwd(name, x, gain):
    N, D = x.shape
    tm = _tile(N, 128)

    def body(x_ref, g_ref, n_ref, r_ref):
        xf = x_ref[...]
        r = lax.rsqrt(jnp.mean(xf * xf, axis=-1, keepdims=True) + NORM_EPS)
        n_ref[...] = ((xf * r) * g_ref[...]).astype(n_ref.dtype)
        r_ref[...] = r

    return pl.pallas_call(
        body, name=name, grid=(N // tm,),
        in_specs=[pl.BlockSpec((tm, D), lambda i: (i, 0)), pl.BlockSpec((1, D), lambda i: (0, 0))],
        out_specs=[pl.BlockSpec((tm, D), lambda i: (i, 0)), pl.BlockSpec((tm, 1), lambda i: (i, 0))],
        out_shape=[jax.ShapeDtypeStruct((N, D), CDT), jax.ShapeDtypeStruct((N, 1), F32)],
        compiler_params=_cp(1),
    )(x, gain)


def _rms_bwd_math(dn, x, r, g):
    xhat = x * r
    dgain = jnp.sum(dn * xhat, axis=0, keepdims=True)
    dxhat = dn * g
    dx = r * (dxhat - xhat * jnp.mean(dxhat * xhat, axis=-1, keepdims=True))
    return dx, dgain


def rms_bwd(name, dn, x, rstd, gain, dres, cast_scale):
    N, D = x.shape
    tm = _tile(N, 128)

    def body(dn_ref, x_ref, r_ref, g_ref, dres_ref, dx_ref, dc_ref, dg_ref):
        dx, dgain = _rms_bwd_math(dn_ref[...], x_ref[...], r_ref[...], g_ref[...])
        tot = dres_ref[...] + dx
        dx_ref[...] = tot
        dc_ref[...] = (cast_scale * tot).astype(dc_ref.dtype)

        @pl.when(pl.program_id(0) == 0)
        def _():
            dg_ref[...] = jnp.zeros_like(dg_ref)

        dg_ref[...] += dgain

    row = pl.BlockSpec((tm, D), lambda i: (i, 0))
    vec = pl.BlockSpec((1, D), lambda i: (0, 0))
    return pl.pallas_call(
        body, name=name, grid=(N // tm,),
        in_specs=[row, row, pl.BlockSpec((tm, 1), lambda i: (i, 0)), vec, row],
        out_specs=[row, row, vec],
        out_shape=[jax.ShapeDtypeStruct((N, D), F32), jax.ShapeDtypeStruct((N, D), CDT),
                   jax.ShapeDtypeStruct((1, D), F32)],
        compiler_params=_cp(1),
    )(dn, x, rstd, gain, dres)


def loss_head(name, h, gain, target):
    N, D = h.shape
    tm = _tile(N, 128)

    def body(h_ref, g_ref, t_ref, l_ref, dh_ref, dc_ref, dg_ref):
        x = h_ref[...]
        g = g_ref[...]
        r = lax.rsqrt(jnp.mean(x * x, axis=-1, keepdims=True) + NORM_EPS)
        err = (x * r) * g - t_ref[...]
        dy = err * (1.0 / D)
        dx, dgain = _rms_bwd_math(dy, x, r, g)
        dh_ref[...] = dx
        dc_ref[...] = (0.5 * dx).astype(dc_ref.dtype)

        @pl.when(pl.program_id(0) == 0)
        def _():
            dg_ref[...] = jnp.zeros_like(dg_ref)
            l_ref[...] = jnp.zeros_like(l_ref)

        dg_ref[...] += dgain
        l_ref[...] += jnp.sum(err * err)

    row = pl.BlockSpec((tm, D), lambda i: (i, 0))
    vec = pl.BlockSpec((1, D), lambda i: (0, 0))
    return pl.pallas_call(
        body, name=name, grid=(N // tm,),
        in_specs=[row, vec, row],
        out_specs=[pl.BlockSpec((1, LANES), lambda i: (0, 0)), row, row, vec],
        out_shape=[jax.ShapeDtypeStruct((1, LANES), F32), jax.ShapeDtypeStruct((N, D), F32),
                   jax.ShapeDtypeStruct((N, D), CDT), jax.ShapeDtypeStruct((1, D), F32)],
        compiler_params=_cp(1),
    )(h, gain, target)


def merge_norm_fwd(name, y_pool, y_ssm, g_pool, g_ssm):
    N, PW = y_pool.shape
    SW = y_ssm.shape[1]
    tm = _tile(N, 128)

    def body(yp_ref, ys_ref, gp_ref, gs_ref, m_ref, rp_ref, rs_ref):
        yp = yp_ref[...]
        ys = ys_ref[...]
        rp = lax.rsqrt(jnp.mean(yp * yp, axis=-1, keepdims=True) + NORM_EPS)
        rs = lax.rsqrt(jnp.mean(ys * ys, axis=-1, keepdims=True) + NORM_EPS)
        m_ref[:, :PW] = ((yp * rp) * gp_ref[...]).astype(m_ref.dtype)
        m_ref[:, PW:] = ((ys * rs) * gs_ref[...]).astype(m_ref.dtype)
        rp_ref[...] = rp
        rs_ref[...] = rs

    return pl.pallas_call(
        body, name=name, grid=(N // tm,),
        in_specs=[pl.BlockSpec((tm, PW), lambda i: (i, 0)), pl.BlockSpec((tm, SW), lambda i: (i, 0)),
                  pl.BlockSpec((1, PW), lambda i: (0, 0)), pl.BlockSpec((1, SW), lambda i: (0, 0))],
        out_specs=[pl.BlockSpec((tm, PW + SW), lambda i: (i, 0)), pl.BlockSpec((tm, 1), lambda i: (i, 0)),
                   pl.BlockSpec((tm, 1), lambda i: (i, 0))],
        out_shape=[jax.ShapeDtypeStruct((N, PW + SW), CDT), jax.ShapeDtypeStruct((N, 1), F32),
                   jax.ShapeDtypeStruct((N, 1), F32)],
        compiler_params=_cp(1),
    )(y_pool, y_ssm, g_pool, g_ssm)


def merge_norm_bwd(name, dmerged, y_pool, y_ssm, r_pool, r_ssm, g_pool, g_ssm):
    N, PW = y_pool.shape
    SW = y_ssm.shape[1]
    tm = _tile(N, 128)

    def body(dm_ref, yp_ref, ys_ref, rp_ref, rs_ref, gp_ref, gs_ref, dyp_ref, dys_ref, dgp_ref, dgs_ref):
        dxp, dgp = _rms_bwd_math(dm_ref[:, :PW], yp_ref[...], rp_ref[...], gp_ref[...])
        dxs, dgs = _rms_bwd_math(dm_ref[:, PW:], ys_ref[...], rs_ref[...], gs_ref[...])
        dyp_ref[...] = dxp
        dys_ref[...] = dxs

        @pl.when(pl.program_id(0) == 0)
        def _():
            dgp_ref[...] = jnp.zeros_like(dgp_ref)
            dgs_ref[...] = jnp.zeros_like(dgs_ref)

        dgp_ref[...] += dgp
        dgs_ref[...] += dgs

    col1 = pl.BlockSpec((tm, 1), lambda i: (i, 0))
    return pl.pallas_call(
        body, name=name, grid=(N // tm,),
        in_specs=[pl.BlockSpec((tm, PW + SW), lambda i: (i, 0)), pl.BlockSpec((tm, PW), lambda i: (i, 0)),
                  pl.BlockSpec((tm, SW), lambda i: (i, 0)), col1, col1,
                  pl.BlockSpec((1, PW), lambda i: (0, 0)), pl.BlockSpec((1, SW), lambda i: (0, 0))],
        out_specs=[pl.BlockSpec((tm, PW), lambda i: (i, 0)), pl.BlockSpec((tm, SW), lambda i: (i, 0)),
                   pl.BlockSpec((1, PW), lambda i: (0, 0)), pl.BlockSpec((1, SW), lambda i: (0, 0))],
        out_shape=[jax.ShapeDtypeStruct((N, PW), F32), jax.ShapeDtypeStruct((N, SW), F32),
                   jax.ShapeDtypeStruct((1, PW), F32), jax.ShapeDtypeStruct((1, SW), F32)],
        compiler_params=_cp(1),
    )(dmerged, y_pool, y_ssm, r_pool, r_ssm, g_pool, g_ssm)


def pool_fwd(name, z, w_pool, scale, n_seq, seq):
    G, Cg, _ = w_pool.shape
    N = n_seq * seq
    PW = G * Cg

    def body(z_ref, w_ref, s_ref, d_ref, y_ref, zs_ref):
        g = pl.program_id(0)
        zv = z_ref[...]
        zs_ref[pl.ds(0, POOL_HALO), :] = jnp.zeros((POOL_HALO, Cg), F32)
        zs_ref[pl.ds(POOL_HALO, seq), :] = zv
        t = lax.broadcasted_iota(jnp.int32, (seq, 1), 0)
        for gi, w in enumerate(POOL_WINDOWS):
            @pl.when(g == gi)
            def _(w=w):
                acc = zv
                for j in range(1, w):
                    acc = acc + zs_ref[pl.ds(POOL_HALO - j, seq), :]
                cnt = jnp.minimum(t + 1, w).astype(F32)
                d = (acc / cnt - zv).astype(d_ref.dtype)
                d_ref[...] = d
                out = jnp.dot(d, w_ref[...], preferred_element_type=F32)
                y_ref[...] = out * s_ref[...]

    blk = pl.BlockSpec((seq, Cg), lambda g, b: (b, g))
    return pl.pallas_call(
        body, name=name, grid=(G, n_seq),
        in_specs=[blk, pl.BlockSpec((None, Cg, Cg), lambda g, b: (g, 0, 0)),
                  pl.BlockSpec((1, Cg), lambda g, b: (0, g))],
        out_specs=[blk, blk],
        out_shape=[jax.ShapeDtypeStruct((N, PW), CDT), jax.ShapeDtypeStruct((N, PW), F32)],
        scratch_shapes=[pltpu.VMEM((seq + POOL_HALO, Cg), F32)],
        compiler_params=_cp(2),
    )(z, w_pool, scale)


def pool_bwd(name, dy_pool, d, w_pool, scale, n_seq, seq):
    G, Cg, _ = w_pool.shape
    N = n_seq * seq
    PW = G * Cg

    def body(dy_ref, d_ref, w_ref, s_ref, dz_ref, dw_ref, ds_ref, es_ref):
        g = pl.program_id(0)
        b = pl.program_id(1)
        dv = d_ref[...]
        wv = w_ref[...]
        dy = dy_ref[...]
        out = jnp.dot(dv, wv, preferred_element_type=F32)
        dout = (dy * s_ref[...]).astype(CDT)
        dw = lax.dot_general(dv, dout, (((0,), (0,)), ((), ())), preferred_element_type=F32)
        dd = lax.dot_general(dout, wv, (((1,), (1,)), ((), ())), preferred_element_type=F32)

        @pl.when(b == 0)
        def _():
            dw_ref[...] = jnp.zeros_like(dw_ref)
            ds_ref[...] = jnp.zeros_like(ds_ref)

        dw_ref[...] += dw
        ds_ref[...] += jnp.sum(out * dy, axis=0, keepdims=True)
        t = lax.broadcasted_iota(jnp.int32, (seq, 1), 0)
        es_ref[pl.ds(seq, POOL_HALO), :] = jnp.zeros((POOL_HALO, Cg), F32)
        for gi, w in enumerate(POOL_WINDOWS):
            @pl.when(g == gi)
            def _(w=w):
                cnt = jnp.minimum(t + 1, w).astype(F32)
                e = dd / cnt
                es_ref[pl.ds(0, seq), :] = e
                acc = e
                for j in range(1, w):
                    acc = acc + es_ref[pl.ds(j, seq), :]
                dz_ref[...] = (acc - dd).astype(dz_ref.dtype)

    blk = pl.BlockSpec((seq, Cg), lambda g, b: (b, g))
    return pl.pallas_call(
        body, name=name, grid=(G, n_seq),
        in_specs=[blk, blk, pl.BlockSpec((None, Cg, Cg), lambda g, b: (g, 0, 0)),
                  pl.BlockSpec((1, Cg), lambda g, b: (0, g))],
        out_specs=[blk, pl.BlockSpec((None, Cg, Cg), lambda g, b: (g, 0, 0)),
                   pl.BlockSpec((1, Cg), lambda g, b: (0, g))],
        out_shape=[jax.ShapeDtypeStruct((N, PW), CDT), jax.ShapeDtypeStruct((G, Cg, Cg), F32),
                   jax.ShapeDtypeStruct((1, PW), F32)],
        scratch_shapes=[pltpu.VMEM((seq + POOL_HALO, Cg), F32)],
        compiler_params=_cp(2),
    )(dy_pool, d, w_pool, scale)


def _ssm_param_math(lr, li, ldt, br, bi):
    dt = jnp.exp(ldt)
    mag = jnp.exp(lr * dt)
    ar = mag * jnp.cos(li * dt)
    ai = mag * jnp.sin(li * dt)
    den = lr * lr + li * li
    xr = ar - 1.0
    cr = (xr * lr + ai * li) / den
    ci = (ai * lr - xr * li) / den
    return ar, ai, cr * br - ci * bi, cr * bi + ci * br


def ssm_params_fwd(name, lr, li, ldt, br, bi):
    GP, H = br.shape

    def body(lr_ref, li_ref, ldt_ref, br_ref, bi_ref, ar_ref, ai_ref, bbr_ref, bbi_ref):
        ar, ai, bbr, bbi = _ssm_param_math(lr_ref[...], li_ref[...], ldt_ref[...], br_ref[...], bi_ref[...])
        ar_ref[...] = ar
        ai_ref[...] = ai
        bbr_ref[...] = bbr
        bbi_ref[...] = bbi

    c1 = jax.ShapeDtypeStruct((GP, 1), F32)
    ch = jax.ShapeDtypeStruct((GP, H), F32)
    tr = _tile(GP, 512)
    b1 = pl.BlockSpec((tr, 1), lambda i: (i, 0))
    bh = pl.BlockSpec((tr, H), lambda i: (i, 0))
    return pl.pallas_call(body, name=name, grid=(GP // tr,), in_specs=[b1, b1, b1, bh, bh],
                          out_specs=[b1, b1, bh, bh], out_shape=[c1, c1, ch, ch],
                          compiler_params=_cp(1))(lr, li, ldt, br, bi)


def ssm_params_bwd(name, lr, li, ldt, br, bi, dar, dai, dbbr, dbbi):
    GP, H = br.shape

    def body(lr_ref, li_ref, ldt_ref, br_ref, bi_ref, dar_ref, dai_ref, dbbr_ref, dbbi_ref,
             dlr_ref, dli_ref, dldt_ref, dbr_ref, dbi_ref):
        _, vjp = jax.vjp(_ssm_param_math, lr_ref[...], li_ref[...], ldt_ref[...], br_ref[...], bi_ref[...])
        dlr, dli, dldt, dbr, dbi = vjp((dar_ref[...], dai_ref[...], dbbr_ref[...], dbbi_ref[...]))
        dlr_ref[...] = dlr
        dli_ref[...] = dli
        dldt_ref[...] = dldt
        dbr_ref[...] = dbr
        dbi_ref[...] = dbi

    c1 = jax.ShapeDtypeStruct((GP, 1), F32)
    ch = jax.ShapeDtypeStruct((GP, H), F32)
    tr = _tile(GP, 512)
    b1 = pl.BlockSpec((tr, 1), lambda i: (i, 0))
    bh = pl.BlockSpec((tr, H), lambda i: (i, 0))
    return pl.pallas_call(body, name=name, grid=(GP // tr,), in_specs=[b1, b1, b1, bh, bh, b1, b1, bh, bh],
                          out_specs=[b1, b1, b1, bh, bh], out_shape=[c1, c1, c1, ch, ch],
                          compiler_params=_cp(1))(lr, li, ldt, br, bi, dar, dai, dbbr, dbbi)


def ssm_expand(name, src, col_off, mat, tm):
    N = src.shape[0]
    NC, cw, w2 = mat.shape
    W = w2 // 2
    tm = _tile(N, tm)

    def body(s_ref, m_ref, re_ref, im_ref):
        r = jnp.dot(s_ref[...].astype(CDT), m_ref[...], preferred_element_type=F32)
        re_ref[...] = r[:, :W]
        im_ref[...] = r[:, W:]

    out = pl.BlockSpec((tm, W), lambda i, c: (i, c))
    return pl.pallas_call(
        body, name=name, grid=(N // tm, NC),
        in_specs=[pl.BlockSpec((tm, cw), lambda i, c: (i, c + col_off)),
                  pl.BlockSpec((None, cw, w2), lambda i, c: (c, 0, 0))],
        out_specs=[out, out],
        out_shape=[jax.ShapeDtypeStruct((N, NC * W), F32)] * 2,
        compiler_params=_cp(2),
    )(src, mat)


def ssm_contract(name, re, im, mat, extras, epi, out_dtypes, tm):
    N = re.shape[0]
    NC, w2, cw = mat.shape
    W = w2 // 2
    tm = _tile(N, tm)
    ne = len(extras)

    def body(*refs):
        re_ref, im_ref, m_ref = refs[:3]
        e_refs = refs[3:3 + ne]
        o_refs = refs[3 + ne:]
        acc = jnp.dot(re_ref[...].astype(CDT), m_ref[pl.ds(0, W), :], preferred_element_type=F32)
        acc = acc + jnp.dot(im_ref[...].astype(CDT), m_ref[pl.ds(W, W), :], preferred_element_type=F32)
        outs = epi(acc, [e[...] for e in e_refs])
        for o_ref, o in zip(o_refs, outs):
            o_ref[...] = o.astype(o_ref.dtype)

    e_arrs, e_specs = [], []
    for arr, kind, off in extras:
        e_arrs.append(arr)
        if kind == "mn":
            e_specs.append(pl.BlockSpec((tm, cw), lambda i, c, off=off: (i, c + off)))
        else:
            e_specs.append(pl.BlockSpec((1, cw), lambda i, c, off=off: (0, c + off)))
    blk = pl.BlockSpec((tm, W), lambda i, c: (i, c))
    return pl.pallas_call(
        body, name=name, grid=(N // tm, NC),
        in_specs=[blk, blk, pl.BlockSpec((None, w2, cw), lambda i, c: (c, 0, 0))] + e_specs,
        out_specs=[pl.BlockSpec((tm, cw), lambda i, c: (i, c)) for _ in out_dtypes],
        out_shape=[jax.ShapeDtypeStruct((N, NC * cw), dt) for dt in out_dtypes],
        compiler_params=_cp(2),
    )(re, im, mat, *e_arrs)


def ssm_outer(name, src, col_off, re, im, NC, Wc, tm):
    N = src.shape[0]
    tm = _tile(N, tm)

    def body(s_ref, re_ref, im_ref, o1_ref, o2_ref):
        @pl.when(pl.program_id(1) == 0)
        def _():
            o1_ref[...] = jnp.zeros_like(o1_ref)
            o2_ref[...] = jnp.zeros_like(o2_ref)

        sv = s_ref[...].astype(CDT)
        dn = (((0,), (0,)), ((), ()))
        o1_ref[...] += lax.dot_general(sv, re_ref[...].astype(CDT), dn, preferred_element_type=F32)
        o2_ref[...] += lax.dot_general(sv, im_ref[...].astype(CDT), dn, preferred_element_type=F32)

    blk = pl.BlockSpec((tm, Wc), lambda c, i: (i, c))
    oblk = pl.BlockSpec((None, LANES, Wc), lambda c, i: (c, 0, 0))
    return pl.pallas_call(
        body, name=name, grid=(NC, N // tm),
        in_specs=[pl.BlockSpec((tm, LANES), lambda c, i: (i, c + col_off)), blk, blk],
        out_specs=[oblk, oblk],
        out_shape=[jax.ShapeDtypeStruct((NC, LANES, Wc), F32)] * 2,
        compiler_params=_cp(2),
    )(src, re, im)


def ssm_scan_fwd(name, bur, bui, ar, ai, n_seq, seq, tc):
    N, R, L = bur.shape
    tc = _tile(seq, tc, 1)
    nt = seq // tc

    def body(bur_ref, bui_ref, ar_ref, ai_ref, xr_ref, xi_ref, st_ref):
        @pl.when(pl.program_id(1) == 0)
        def _():
            st_ref[...] = jnp.zeros_like(st_ref)

        a_r = ar_ref[...]
        a_i = ai_ref[...]

        def step(t, carry):
            xr, xi = carry
            nr = a_r * xr - a_i * xi + bur_ref[t]
            ni = a_r * xi + a_i * xr + bui_ref[t]
            xr_ref[t] = nr
            xi_ref[t] = ni
            return nr, ni

        xr, xi = lax.fori_loop(0, tc, step, (st_ref[0], st_ref[1]))
        st_ref[0] = xr
        st_ref[1] = xi

    blk = pl.BlockSpec((tc, R, L), lambda b, j: (b * nt + j, 0, 0))
    par = pl.BlockSpec((R, L), lambda b, j: (0, 0))
    return pl.pallas_call(
        body, name=name, grid=(n_seq, nt),
        in_specs=[blk, blk, par, par], out_specs=[blk, blk],
        out_shape=[jax.ShapeDtypeStruct((N, R, L), F32)] * 2,
        scratch_shapes=[pltpu.VMEM((2, R, L), F32)],
        compiler_params=_cp(2),
    )(bur, bui, ar, ai)


def ssm_scan_bwd(name, gr, gi, xr, xi, ar, ai, n_seq, seq, tc):
    N, R, L = gr.shape
    tc = _tile(seq, tc, 1)
    nt = seq // tc

    def body(gr_ref, gi_ref, xr_ref, xi_ref, ar_ref, ai_ref, lr_ref, li_ref, dar_ref, dai_ref, st_ref):
        b = pl.program_id(0)
        j = pl.program_id(1)

        @pl.when((b == 0) & (j == 0))
        def _():
            dar_ref[...] = jnp.zeros_like(dar_ref)
            dai_ref[...] = jnp.zeros_like(dai_ref)

        @pl.when(j == 0)
        def _():
            st_ref[...] = jnp.zeros_like(st_ref)

        a_r = ar_ref[...]
        a_i = ai_ref[...]

        def step(s, carry):
            t = tc - 1 - s
            lr, li, dr, di = carry
            xrt = xr_ref[t]
            xit = xi_ref[t]
            dr = dr + (lr * xrt + li * xit)
            di = di + (li * xrt - lr * xit)
            nlr = gr_ref[t] + (a_r * lr + a_i * li)
            nli = gi_ref[t] + (a_r * li - a_i * lr)
            lr_ref[t] = nlr
            li_ref[t] = nli
            return nlr, nli, dr, di

        lr, li, dr, di = lax.fori_loop(0, tc, step, (st_ref[0], st_ref[1], dar_ref[...], dai_ref[...]))
        st_ref[0] = lr
        st_ref[1] = li
        dar_ref[...] = dr
        dai_ref[...] = di

    blk = pl.BlockSpec((tc, R, L), lambda b, j: (b * nt + nt - 1 - j, 0, 0))
    par = pl.BlockSpec((R, L), lambda b, j: (0, 0))
    return pl.pallas_call(
        body, name=name, grid=(n_seq, nt),
        in_specs=[blk, blk, blk, blk, par, par], out_specs=[blk, blk, par, par],
        out_shape=[jax.ShapeDtypeStruct((N, R, L), F32)] * 2 + [jax.ShapeDtypeStruct((R, L), F32)] * 2,
        scratch_shapes=[pltpu.VMEM((2, R, L), F32)],
        compiler_params=_cp(2),
    )(gr, gi, xr, xi, ar, ai)


def glu_bwd_pre(name, dy_ssm, y_pre, q):
    N, SW = y_pre.shape
    tm = _tile(N, 128)

    def body(dy_ref, y_ref, q_ref, dq_ref, dyg_ref, db_ref):
        dy = dy_ref[...]
        yg = _gelu(y_ref[...])
        s = jax.nn.sigmoid(q_ref[...])
        dq = dy * yg * (s * (1.0 - s))
        dq_ref[...] = dq.astype(dq_ref.dtype)
        dyg_ref[...] = dy * s

        @pl.when(pl.program_id(0) == 0)
        def _():
            db_ref[...] = jnp.zeros_like(db_ref)

        db_ref[...] += jnp.sum(dq, axis=0, keepdims=True)

    row = pl.BlockSpec((tm, SW), lambda i: (i, 0))
    vec = pl.BlockSpec((1, SW), lambda i: (0, 0))
    return pl.pallas_call(
        body, name=name, grid=(N // tm,), in_specs=[row, row, row], out_specs=[row, row, vec],
        out_shape=[jax.ShapeDtypeStruct((N, SW), CDT), jax.ShapeDtypeStruct((N, SW), F32),
                   jax.ShapeDtypeStruct((1, SW), F32)],
        compiler_params=_cp(1),
    )(dy_ssm, y_pre, q)


def colsum_prod(name, a, b, b_col_off, width):
    N = a.shape[0]
    tm = _tile(N, 128)

    def body(a_ref, b_ref, o_ref):
        @pl.when(pl.program_id(0) == 0)
        def _():
            o_ref[...] = jnp.zeros_like(o_ref)

        o_ref[...] += jnp.sum(a_ref[...] * b_ref[...], axis=0, keepdims=True)

    return pl.pallas_call(
        body, name=name, grid=(N // tm,),
        in_specs=[pl.BlockSpec((tm, width), lambda i: (i, 0)),
                  pl.BlockSpec((tm, width), lambda i: (i, b_col_off))],
        out_specs=pl.BlockSpec((1, width), lambda i: (0, 0)),
        out_shape=jax.ShapeDtypeStruct((1, width), F32),
        compiler_params=_cp(1),
    )(a, b)


def adamw(name, w, g, m, v):
    R, C = w.shape
    tr = _tile(R, max(8, (1 << 18) // C))

    def body(w_ref, g_ref, m_ref, v_ref, d_ref, nm_ref, nv_ref):
        gv = g_ref[...]
        nm = ADAM_B1 * m_ref[...] + (1.0 - ADAM_B1) * gv
        nv = ADAM_B2 * v_ref[...] + (1.0 - ADAM_B2) * jnp.square(gv)
        m_hat = nm / (1.0 - ADAM_B1 ** ADAM_STEP)
        v_hat = nv / (1.0 - ADAM_B2 ** ADAM_STEP)
        d_ref[...] = -ADAM_LR * (m_hat / (jnp.sqrt(v_hat) + ADAM_EPS) + ADAM_WD * w_ref[...])
        nm_ref[...] = nm
        nv_ref[...] = nv

    blk = pl.BlockSpec((tr, C), lambda i: (i, 0))
    sh = jax.ShapeDtypeStruct((R, C), F32)
    return pl.pallas_call(body, name=name, grid=(R // tr,), in_specs=[blk] * 4, out_specs=[blk] * 3,
                          out_shape=[sh] * 3, compiler_params=_cp(1))(w, g, m, v)


def add_halves(name, part, recv, c_idx):
    S4, R, C = part.shape
    h = R // 2
    tr = _tile(h, max(16, (1 << 19) // C), 16)
    nb = h // tr

    def body(c_ref, p_ref, r_ref, o_ref):
        o_ref[...] = (p_ref[...].astype(F32) + r_ref[...].astype(F32)).astype(o_ref.dtype)

    grid_spec = pltpu.PrefetchScalarGridSpec(
        num_scalar_prefetch=1, grid=(S4, nb),
        in_specs=[pl.BlockSpec((None, tr, C), lambda s, i, c_ref: (s, c_ref[0] * nb + i, 0)),
                  pl.BlockSpec((None, tr, C), lambda s, i, c_ref: (s, i, 0))],
        out_specs=pl.BlockSpec((None, tr, C), lambda s, i, c_ref: (s, i, 0)),
    )
    return pl.pallas_call(body, name=name, grid_spec=grid_spec,
                          out_shape=jax.ShapeDtypeStruct((S4, h, C), CDT),
                          compiler_params=_cp(2))(c_idx, part, recv)


def sum_chips(name, t, u, idx):
    _, h, C = t.shape
    tr = _tile(h, max(16, (1 << 18) // C), 16)
    nb = h // tr

    def body(idx_ref, t_ref, u_ref, o_ref):
        acc = t_ref[...].astype(F32)
        for r in range(NCHIP - 1):
            acc = acc + u_ref[r].astype(F32)
        o_ref[...] = acc

    grid_spec = pltpu.PrefetchScalarGridSpec(
        num_scalar_prefetch=1, grid=(nb,),
        in_specs=[pl.BlockSpec((None, tr, C), lambda i, idx_ref: (idx_ref[0], i, 0)),
                  pl.BlockSpec((NCHIP - 1, tr, C), lambda i, idx_ref: (0, i, 0))],
        out_specs=pl.BlockSpec((tr, C), lambda i, idx_ref: (idx_ref[1] * nb + i, 0)),
    )
    return pl.pallas_call(body, name=name, grid_spec=grid_spec,
                          out_shape=jax.ShapeDtypeStruct((2 * h, C), F32), compiler_params=_cp(1))(idx, t, u)


def _place():
    x, y, c = lax.axis_index("x"), lax.axis_index("y"), lax.axis_index("c")
    chips = [(1 - x, y), (x, 1 - y), (1 - x, 1 - y)]
    return x, y, c, chips


_ANY = pl.BlockSpec(memory_space=pl.ANY)


def gather_shards(name, shards):
    n = len(shards)

    def body(*refs):
        ins = refs[:n]
        outs = refs[n:2 * n]
        send_sems, recv_sems = refs[2 * n:]
        x, y, c, chips = _place()
        me = 2 * x + y
        sib = (x, y, 1 - c)
        sends = []
        for k in range(n):
            h = ins[k].shape[0] // 2
            cp = pltpu.make_async_remote_copy(
                src_ref=ins[k], dst_ref=outs[k].at[me], send_sem=send_sems.at[k, 6], recv_sem=recv_sems.at[k, 6],
                device_id=sib, device_id_type=MESH)
            cp.start()
            sends.append(cp)
            for r, (qx, qy) in enumerate(chips):
                cp = pltpu.make_async_remote_copy(
                    src_ref=ins[k].at[pl.ds(c * h, h)], dst_ref=outs[k].at[me, pl.ds(c * h, h)],
                    send_sem=send_sems.at[k, r], recv_sem=recv_sems.at[k, r],
                    device_id=(qx, qy, c), device_id_type=MESH)
                cp.start()
                sends.append(cp)
        for k in range(n):
            h = ins[k].shape[0] // 2
            for r, (qx, qy) in enumerate(chips):
                q = 2 * qx + qy
                region = outs[k].at[q, pl.ds(c * h, h)]
                pltpu.make_async_remote_copy(
                    src_ref=region, dst_ref=region, send_sem=send_sems.at[k, r], recv_sem=recv_sems.at[k, r],
                    device_id=(qx, qy, c), device_id_type=MESH).wait_recv()
                cp = pltpu.make_async_remote_copy(
                    src_ref=region, dst_ref=region, send_sem=send_sems.at[k, 3 + r],
                    recv_sem=recv_sems.at[k, 3 + r], device_id=sib, device_id_type=MESH)
                cp.start()
                sends.append(cp)
        for k in range(n):
            h = ins[k].shape[0] // 2
            for r, (qx, qy) in enumerate(chips):
                q = 2 * qx + qy
                region = outs[k].at[q, pl.ds((1 - c) * h, h)]
                pltpu.make_async_remote_copy(
                    src_ref=region, dst_ref=region, send_sem=send_sems.at[k, 3 + r],
                    recv_sem=recv_sems.at[k, 3 + r], device_id=sib, device_id_type=MESH).wait_recv()
            own = outs[k].at[me]
            pltpu.make_async_remote_copy(
                src_ref=own, dst_ref=own, send_sem=send_sems.at[k, 6], recv_sem=recv_sems.at[k, 6],
                device_id=sib, device_id_type=MESH).wait_recv()
        for cp in sends:
            cp.wait_send()

    return pl.pallas_call(
        body, name=name,
        in_specs=[_ANY] * n, out_specs=[_ANY] * n,
        out_shape=[jax.ShapeDtypeStruct((NCHIP,) + s.shape, s.dtype) for s in shards],
        scratch_shapes=[pltpu.SemaphoreType.DMA((n, 7)), pltpu.SemaphoreType.DMA((n, 7))],
    )(*shards)


def _remote(src, dst, ssem, rsem, dev):
    return pltpu.make_async_remote_copy(src_ref=src, dst_ref=dst, send_sem=ssem, recv_sem=rsem,
                                        device_id=dev, device_id_type=MESH)


def _spec(arrays, out_shapes, aliases, sem_cols, make):
    return dict(arrays=list(arrays), out_shapes=list(out_shapes), aliases=dict(aliases),
                sems=(len(arrays), sem_cols), make=make)


def gather_send_spec(shards):
    n = len(shards)

    def make(ins, outs, ss, rs):
        x, y, c, chips = _place()
        me = 2 * x + y
        sib = (x, y, 1 - c)
        starts, waits = [], []
        for k in range(n):
            h = ins[k].shape[0] // 2
            own = outs[k].at[me]
            full = _remote(ins[k], own, ss.at[k, 3], rs.at[k, 3], sib)
            starts.append(full.start)
            waits += [full.wait_send, _remote(own, own, ss.at[k, 3], rs.at[k, 3], sib).wait_recv]
            for r, (qx, qy) in enumerate(chips):
                cp = _remote(ins[k].at[pl.ds(c * h, h)], outs[k].at[me, pl.ds(c * h, h)],
                             ss.at[k, r], rs.at[k, r], (qx, qy, c))
                region = outs[k].at[2 * qx + qy, pl.ds(c * h, h)]
                starts.append(cp.start)
                waits += [cp.wait_send, _remote(region, region, ss.at[k, r], rs.at[k, r], (qx, qy, c)).wait_recv]
        return starts, waits

    return _spec(shards, [jax.ShapeDtypeStruct((NCHIP,) + s.shape, s.dtype) for s in shards], {}, 4, make)


def gather_pass_spec(bufs):
    n = len(bufs)

    def make(ins, outs, ss, rs):
        x, y, c, chips = _place()
        sib = (x, y, 1 - c)
        starts, waits = [], []
        for k in range(n):
            h = outs[k].shape[1] // 2
            for r, (qx, qy) in enumerate(chips):
                q = 2 * qx + qy
                region = outs[k].at[q, pl.ds(c * h, h)]
                other = outs[k].at[q, pl.ds((1 - c) * h, h)]
                cp = _remote(region, region, ss.at[k, r], rs.at[k, r], sib)
                starts.append(cp.start)
                waits += [cp.wait_send, _remote(other, other, ss.at[k, r], rs.at[k, r], sib).wait_recv]
        return starts, waits

    return _spec(bufs, [jax.ShapeDtypeStruct(b.shape, b.dtype) for b in bufs], {k: k for k in range(n)}, 3, make)


def rs_xh_spec(parts):
    n = len(parts)

    def make(ins, outs, ss, rs):
        x, y, c, _ = _place()
        sib = (x, y, 1 - c)
        starts, waits = [], []
        for k in range(n):
            h = ins[k].shape[1] // 2
            cp = _remote(ins[k].at[:, pl.ds((1 - c) * h, h)], outs[k], ss.at[k, 0], rs.at[k, 0], sib)
            starts.append(cp.start)
            waits.append(cp.wait)
        return starts, waits

    shapes = [jax.ShapeDtypeStruct((p.shape[0], p.shape[1] // 2) + p.shape[2:], p.dtype) for p in parts]
    return _spec(parts, shapes, {}, 1, make)


def rs_xc_spec(sums):
    n = len(sums)

    def make(ins, outs, ss, rs):
        x, y, c, chips = _place()
        starts, waits = [], []
        for k in range(n):
            for r, (qx, qy) in enumerate(chips):
                cp = _remote(ins[k].at[2 * qx + qy], outs[k].at[r], ss.at[k, r], rs.at[k, r], (qx, qy, c))
                starts.append(cp.start)
                waits.append(cp.wait)
        return starts, waits

    return _spec(sums, [jax.ShapeDtypeStruct((NCHIP - 1,) + s.shape[1:], s.dtype) for s in sums], {}, 3, make)


def rs_jh_spec(fulls):
    n = len(fulls)

    def make(ins, outs, ss, rs):
        x, y, c, _ = _place()
        sib = (x, y, 1 - c)
        starts, waits = [], []
        for k in range(n):
            h = outs[k].shape[0] // 2
            mine = outs[k].at[pl.ds(c * h, h)]
            other = outs[k].at[pl.ds((1 - c) * h, h)]
            cp = _remote(mine, mine, ss.at[k, 0], rs.at[k, 0], sib)
            starts.append(cp.start)
            waits += [cp.wait_send, _remote(other, other, ss.at[k, 0], rs.at[k, 0], sib).wait_recv]
        return starts, waits

    return _spec(fulls, [jax.ShapeDtypeStruct(f.shape, f.dtype) for f in fulls], {k: k for k in range(n)}, 1, make)


def all_reduce_small(name, buf):
    R, L = buf.shape
    h = R // 2

    def body(x_ref, o_ref, sib_ref, chip_ref, send_sems, recv_sems):
        x, y, c, chips = _place()
        me = 2 * x + y
        sib = (x, y, 1 - c)
        first = pltpu.make_async_remote_copy(src_ref=x_ref, dst_ref=sib_ref, send_sem=send_sems.at[0],
                                             recv_sem=recv_sems.at[0], device_id=sib, device_id_type=MESH)
        first.start()
        first.wait()
        mine = pl.ds(pl.multiple_of(c * h, 8), h)
        other = pl.ds(pl.multiple_of((1 - c) * h, 8), h)
        chip_ref[me] = x_ref[mine, :] + sib_ref[mine, :]
        cps = []
        for r, (qx, qy) in enumerate(chips):
            cp = pltpu.make_async_remote_copy(
                src_ref=chip_ref.at[me], dst_ref=chip_ref.at[me], send_sem=send_sems.at[1 + r],
                recv_sem=recv_sems.at[1 + r], device_id=(qx, qy, c), device_id_type=MESH)
            cp.start()
            cps.append(cp)
        for r, (qx, qy) in enumerate(chips):
            q = 2 * qx + qy
            pltpu.make_async_remote_copy(
                src_ref=chip_ref.at[q], dst_ref=chip_ref.at[q], send_sem=send_sems.at[1 + r],
                recv_sem=recv_sems.at[1 + r], device_id=(qx, qy, c), device_id_type=MESH).wait_recv()
        for cp in cps:
            cp.wait_send()
        o_ref[mine, :] = ((chip_ref[0] + chip_ref[1]) + chip_ref[2]) + chip_ref[3]
        last = pltpu.make_async_remote_copy(src_ref=o_ref.at[mine], dst_ref=o_ref.at[mine],
                                            send_sem=send_sems.at[4], recv_sem=recv_sems.at[4],
                                            device_id=sib, device_id_type=MESH)
        last.start()
        last.wait_send()
        pltpu.make_async_remote_copy(src_ref=o_ref.at[other], dst_ref=o_ref.at[other],
                                     send_sem=send_sems.at[4], recv_sem=recv_sems.at[4],
                                     device_id=sib, device_id_type=MESH).wait_recv()

    vm = pl.BlockSpec(memory_space=pltpu.VMEM)
    return pl.pallas_call(
        body, name=name, in_specs=[vm], out_specs=vm,
        out_shape=jax.ShapeDtypeStruct((R, L), F32),
        scratch_shapes=[pltpu.VMEM((R, L), F32), pltpu.VMEM((NCHIP, h, L), F32),
                        pltpu.SemaphoreType.DMA((5,)), pltpu.SemaphoreType.DMA((5,))],
        compiler_params=pltpu.CompilerParams(vmem_limit_bytes=VMEM_LIMIT),
    )(buf)


def _ffn_fwd(tag, h, gain, wg3, wu3, wd, up_comm=(), down_comm=None, wd_from_up=None):
    n, rstd = rms_fwd(tag + "_norm", h, gain)

    def epi(accs, extras):
        g, u = accs
        return g, u, _silu(g) * u

    g, u, a, *up_res = mm_nt(tag + "_up", [n], [wg3, wu3], tm=1024, tn=256, out_dtypes=(CDT, CDT, CDT),
                             epi=epi, pair_to_acc=[0, 1], comm=up_comm)

    def epi_down(accs, extras):
        return (extras[0] + 0.5 * accs[0],)

    if wd_from_up is not None:
        wd = wd_from_up(up_res)
    h_out, *down_res = mm_nn(tag + "_down", [a], [wd], tm=1024, tn=512, tk=wd.shape[0] // NCHIP,
                             extras=[(h, "mn", 0)], epi=epi_down, comm=down_comm(up_res) if down_comm else ())
    return h_out, (h, n, rstd, g, u, a), up_res, down_res, wd


def _ffn_bact(tag, saved, wd, dfb, comm=()):
    _, _, _, g, u, _ = saved

    def epi_act(accs, extras):
        da = accs[0]
        gv = extras[0].astype(F32)
        uv = extras[1].astype(F32)
        s = jax.nn.sigmoid(gv)
        return da * uv * (s * (1.0 + gv * (1.0 - s))), da * (gv * s)

    return mm_nt(tag + "_bact", [dfb], [wd], tm=1024, tn=256, extras=[(g, "mn", 0), (u, "mn", 0)],
                 out_dtypes=(CDT, CDT), epi=epi_act, comm=comm)


def _ffn_dwd(tag, saved, dfb, comm=()):
    return mm_tn(tag + "_dwd", [saved[5]], [dfb], ti=512, tj=512, comm=comm)


def _ffn_dwgu(tag, saved, dg, du, comm=()):
    return mm_tn(tag + "_dwgu", [dg, du], [saved[1]], ti=512, tj=512, comm=comm)


def _ffn_dn(tag, dg, du, wg3, wu3, comm=()):
    return mm_nn(tag + "_dn", [dg, du], [wg3, wu3], tm=1024, tn=512, tk=wg3.shape[0] // (2 * NCHIP),
                 pair_to_acc=[0, 0], comm=comm)


def _rs3(p):
    return p.reshape(p.shape[0], -1, p.shape[-1])


def _blockdiag(m, gpc):
    G, a, b = m.shape
    nc = G // gpc
    mask = jnp.eye(gpc, dtype=m.dtype)[None, :, None, :, None]
    out = m.reshape(nc, gpc, a, 1, b) * mask
    return out.reshape(nc, gpc * a, gpc * b)


def _diag_blocks(o, gpc, a, b):
    nc = o.shape[0]
    o5 = o.reshape(nc, gpc, a, gpc, b)
    mask = jnp.eye(gpc, dtype=o.dtype)[None, :, None, :, None]
    return jnp.sum(o5 * mask, axis=3).reshape(nc * gpc, a, b)


def kernel(x, ffn1_norm, ffn1_gate, ffn1_up, ffn1_down, mix_norm, w_in, w_pool, pool_scale, lam_re, lam_im, log_dt, b_re, b_im, c_re, c_im, d_skip, w_glu, b_glu, pool_out_norm, ssm_out_norm, w_out, ffn2_norm, ffn2_gate, ffn2_up, ffn2_down, final_norm, loss_target, m_ffn1_norm, m_ffn1_gate, m_ffn1_up, m_ffn1_down, m_mix_norm, m_w_in, m_w_pool, m_pool_scale, m_lam_re, m_lam_im, m_log_dt, m_b_re, m_b_im, m_c_re, m_c_im, m_d_skip, m_w_glu, m_b_glu, m_pool_out_norm, m_ssm_out_norm, m_w_out, m_ffn2_norm, m_ffn2_gate, m_ffn2_up, m_ffn2_down, m_final_norm, v_ffn1_norm, v_ffn1_gate, v_ffn1_up, v_ffn1_down, v_mix_norm, v_w_in, v_w_pool, v_pool_scale, v_lam_re, v_lam_im, v_log_dt, v_b_re, v_b_im, v_c_re, v_c_im, v_d_skip, v_w_glu, v_b_glu, v_pool_out_norm, v_ssm_out_norm, v_w_out, v_ffn2_norm, v_ffn2_gate, v_ffn2_up, v_ffn2_down, v_final_norm):
    weights = dict(ffn1_norm=ffn1_norm, ffn1_gate=ffn1_gate, ffn1_up=ffn1_up, ffn1_down=ffn1_down, mix_norm=mix_norm, w_in=w_in, w_pool=w_pool, pool_scale=pool_scale, lam_re=lam_re, lam_im=lam_im, log_dt=log_dt, b_re=b_re, b_im=b_im, c_re=c_re, c_im=c_im, d_skip=d_skip, w_glu=w_glu, b_glu=b_glu, pool_out_norm=pool_out_norm, ssm_out_norm=ssm_out_norm, w_out=w_out, ffn2_norm=ffn2_norm, ffn2_gate=ffn2_gate, ffn2_up=ffn2_up, ffn2_down=ffn2_down, final_norm=final_norm)
    moms = dict(ffn1_norm=(m_ffn1_norm, v_ffn1_norm), ffn1_gate=(m_ffn1_gate, v_ffn1_gate), ffn1_up=(m_ffn1_up, v_ffn1_up), ffn1_down=(m_ffn1_down, v_ffn1_down), mix_norm=(m_mix_norm, v_mix_norm), w_in=(m_w_in, v_w_in), w_pool=(m_w_pool, v_w_pool), pool_scale=(m_pool_scale, v_pool_scale), lam_re=(m_lam_re, v_lam_re), lam_im=(m_lam_im, v_lam_im), log_dt=(m_log_dt, v_log_dt), b_re=(m_b_re, v_b_re), b_im=(m_b_im, v_b_im), c_re=(m_c_re, v_c_re), c_im=(m_c_im, v_c_im), d_skip=(m_d_skip, v_d_skip), w_glu=(m_w_glu, v_w_glu), b_glu=(m_b_glu, v_b_glu), pool_out_norm=(m_pool_out_norm, v_pool_out_norm), ssm_out_norm=(m_ssm_out_norm, v_ssm_out_norm), w_out=(m_w_out, v_w_out), ffn2_norm=(m_ffn2_norm, v_ffn2_norm), ffn2_gate=(m_ffn2_gate, v_ffn2_gate), ffn2_up=(m_ffn2_up, v_ffn2_up), ffn2_down=(m_ffn2_down, v_ffn2_down), final_norm=(m_final_norm, v_final_norm))
    names = list(weights)

    n_seq, seq, D = x.shape
    N = n_seq * seq
    Fs = ffn1_gate.shape[1]
    Fps = -(-Fs // FF_ALIGN) * FF_ALIGN
    G, _, Cg = w_pool.shape
    PW = G * Cg
    SW = d_skip.shape[0]
    SG, P = lam_re.shape
    H = SSM_H
    gpc = LANES // H
    NC = SG // gpc
    W = gpc * P
    GP = SG * P
    place_idx = jnp.stack([2 * lax.axis_index("x") + lax.axis_index("y"), lax.axis_index("c")]).astype(jnp.int32)

    row = lambda v: v.reshape(1, -1)
    xf = x.reshape(N, D)
    tgt = loss_target.reshape(N, D)

    pad_r = lambda w: jnp.pad(w.astype(CDT), ((0, Fps - Fs), (0, 0)))
    pad_c = lambda w: pad_r(w.T)
    rows4 = lambda w4: w4.reshape(NCHIP * Fps, D)
    wg1, wu1, wd1 = map(rows4, gather_shards("gather_ffn1", [pad_c(ffn1_gate), pad_c(ffn1_up), pad_r(ffn1_down)]))
    mix_shards = [w_in.astype(CDT), w_out.astype(CDT), w_glu.astype(CDT), w_pool.astype(CDT)]
    f2_shards = [pad_c(ffn2_gate), pad_c(ffn2_up), pad_r(ffn2_down)]

    h1, saved1, _, down_res, _ = _ffn_fwd(
        "ffn1", xf, row(ffn1_norm), wg1, wu1, wd1, up_comm=[gather_send_spec(mix_shards + f2_shards[:1])],
        down_comm=lambda sent: [gather_pass_spec(sent), gather_send_spec(f2_shards[1:2])])
    w_in_f, w_out_f, w_glu_f, w_pool_f, wg2, sent_u2 = down_res
    wg2 = rows4(wg2)
    w_in_f = w_in_f.reshape(D, PW + SW)
    w_out_f = w_out_f.reshape(PW + SW, D)
    w_glu_f = w_glu_f.reshape(SW, SW)
    w_pool_f = jnp.swapaxes(w_pool_f, 0, 1).reshape(G, Cg, Cg)
    n2, rstd2 = rms_fwd("mix_norm", h1, row(mix_norm))
    z, wu2 = mm_nn("mix_in", [n2], [w_in_f], tm=1024, tn=256, comm=[gather_pass_spec([sent_u2])])
    wu2 = rows4(wu2)

    d_pool, y_pool = pool_fwd("pool_fwd", z, w_pool_f, row(pool_scale), n_seq, seq)

    col = lambda v: v.reshape(GP, 1)
    lr_c, li_c = col(lam_re), col(lam_im)
    ldt_c = col(jnp.broadcast_to(log_dt[:, None], (SG, P)))
    br_c, bi_c = b_re.reshape(GP, H), b_im.reshape(GP, H)
    ar, ai, bbr, bbi = ssm_params_fwd("ssm_params", lr_c, li_c, ldt_c, br_c, bi_c)
    ar2, ai2 = ar.reshape(GP // LANES, LANES), ai.reshape(GP // LANES, LANES)
    bbr_t = jnp.swapaxes(bbr.reshape(SG, P, H), 1, 2)
    bbi_t = jnp.swapaxes(bbi.reshape(SG, P, H), 1, 2)
    b_cat = jnp.concatenate([_blockdiag(bbr_t, gpc), _blockdiag(bbi_t, gpc)], axis=2).astype(CDT)
    b_cat_t = jnp.swapaxes(b_cat, 1, 2)
    c_cat_t = jnp.concatenate([_blockdiag(c_re, gpc), _blockdiag(-c_im, gpc)], axis=2).astype(CDT)
    c_cat = jnp.swapaxes(c_cat_t, 1, 2)

    u_off = PW // LANES
    bur, bui = ssm_expand("ssm_bu", z, u_off, b_cat, SSM_ROWS)
    v3 = lambda a: a.reshape(N, GP // LANES, LANES)
    xs_r, xs_i = ssm_scan_fwd("ssm_scan", v3(bur), v3(bui), ar2, ai2, n_seq, seq, 64)
    xs_r, xs_i = xs_r.reshape(N, GP), xs_i.reshape(N, GP)

    def epi_y(acc, extras):
        y = acc + extras[1] * extras[0]
        return y, _gelu(y)

    y_pre, yg = ssm_contract("ssm_y", xs_r, xs_i, c_cat, [(z, "mn", u_off), (row(d_skip), "n", 0)], epi_y,
                             (F32, CDT), SSM_ROWS)

    def epi_glu(accs, extras):
        q = accs[0] + extras[1]
        return q, _gelu(extras[0]) * jax.nn.sigmoid(q)

    q_glu, y_ssm = mm_nn("glu", [yg], [w_glu_f], tm=1024, tn=256,
                         extras=[(y_pre, "mn", 0), (row(b_glu), "n", 0)], out_dtypes=(F32, F32), epi=epi_glu)
    merged, rstd_p, rstd_s = merge_norm_fwd("merge_norm", y_pool, y_ssm, row(pool_out_norm), row(ssm_out_norm))

    def epi_res(accs, extras):
        return (extras[0] + accs[0],)

    (h2,) = mm_nn("mix_out", [merged], [w_out_f], tm=1024, tn=256, extras=[(h1, "mn", 0)], epi=epi_res)
    h3, saved2, _, _, wd2 = _ffn_fwd(
        "ffn2", h2, row(ffn2_norm), wg2, wu2, None, up_comm=[gather_send_spec(f2_shards[2:])],
        wd_from_up=lambda sent: rows4(comm_call("gather_ffn2_down", [gather_pass_spec(sent)])[0]))

    half = place_idx[1:]
    loss_acc, dh3, dfb3, g_final = loss_head("loss_head", h3, row(final_norm), tgt)
    dg2, du2 = _ffn_bact("ffn2", saved2, wd2, dfb3)
    (dwd2,) = _ffn_dwd("ffn2", saved2, dfb3)
    dwg2, dwu2 = _ffn_dwgu("ffn2", saved2, dg2, du2)
    parts_f2 = [p.reshape(NCHIP, Fps, D) for p in (dwg2, dwu2, dwd2)]
    dn_f2, *recv_f2 = _ffn_dn("ffn2", dg2, du2, wg2, wu2, comm=[rs_xh_spec(parts_f2)])
    dh2, dh2c, g_ffn2n = rms_bwd("ffn2_bnorm", dn_f2, saved2[0], saved2[2], row(ffn2_norm), dh3, 1.0)
    sums_f2 = [add_halves(f"rs_add_f2_{k}", p, r, half) for k, (p, r) in enumerate(zip(parts_f2, recv_f2))]

    (dmerged,) = mm_nt("mix_out_bx", [dh2c], [w_out_f], tm=1024, tn=256)
    (dw_out,) = mm_tn("mix_out_bw", [merged], [dh2c], ti=512, tj=512)
    dy_pool, dy_ssm, g_pon, g_son = merge_norm_bwd("merge_norm_b", dmerged, y_pool, y_ssm, rstd_p, rstd_s,
                                                   row(pool_out_norm), row(ssm_out_norm))
    dq, dyg1, g_bglu = glu_bwd_pre("glu_b_pre", dy_ssm, y_pre, q_glu)

    def epi_dyg(accs, extras):
        return ((accs[0] + extras[0]) * _gelu_grad(extras[1]),)

    (dy_pre,) = mm_nt("glu_bx", [dq], [w_glu_f], tm=1024, tn=256, extras=[(dyg1, "mn", 0), (y_pre, "mn", 0)],
                      epi=epi_dyg)
    (dw_glu,) = mm_tn("glu_bw", [yg], [dq], ti=512, tj=512)
    g_dskip = colsum_prod("dskip", dy_pre, z, PW // SW, SW)
    gxr, gxi = ssm_expand("ssm_by", dy_pre, 0, c_cat_t, SSM_ROWS)
    dc_r, dc_i = ssm_outer("ssm_dc", dy_pre, 0, xs_r, xs_i, NC, W, SSM_ROWS)
    lam_r, lam_i, dar, dai = ssm_scan_bwd("ssm_scan_b", v3(gxr), v3(gxi), v3(xs_r), v3(xs_i), ar2, ai2,
                                          n_seq, seq, 64)
    lam_r, lam_i = lam_r.reshape(N, GP), lam_i.reshape(N, GP)
    db_r, db_i = ssm_outer("ssm_db", z, u_off, lam_r, lam_i, NC, W, SSM_ROWS)

    def epi_du(acc, extras):
        return (acc + extras[0] * extras[1],)

    (du_ssm,) = ssm_contract("ssm_bu_b", lam_r, lam_i, b_cat_t, [(dy_pre, "mn", 0), (row(d_skip), "n", 0)],
                             epi_du, (CDT,), SSM_ROWS)
    g_c_re = _diag_blocks(dc_r, gpc, H, P)
    g_c_im = -_diag_blocks(dc_i, gpc, H, P)
    dbbr = jnp.swapaxes(_diag_blocks(db_r, gpc, H, P), 1, 2).reshape(GP, H)
    dbbi = jnp.swapaxes(_diag_blocks(db_i, gpc, H, P), 1, 2).reshape(GP, H)
    dlr, dli, dldt, dbr, dbi = ssm_params_bwd("ssm_params_b", lr_c, li_c, ldt_c, br_c, bi_c,
                                              dar.reshape(GP, 1), dai.reshape(GP, 1), dbbr, dbbi)
    g_lam_re, g_lam_im = dlr.reshape(SG, P), dli.reshape(SG, P)
    g_log_dt = jnp.sum(dldt.reshape(SG, P), axis=1)
    g_b_re, g_b_im = dbr.reshape(SG, P, H), dbi.reshape(SG, P, H)
    dz_pool, dw_pool, g_pscale = pool_bwd("pool_bwd", dy_pool, d_pool, w_pool_f, row(pool_scale), n_seq, seq)
    dz = jnp.concatenate([dz_pool, du_ssm], axis=1)
    (dn2,) = mm_nt("mix_in_bx", [dz], [w_in_f], tm=1024, tn=256)
    (dw_in,) = mm_tn("mix_in_bw", [n2], [dz], ti=512, tj=512)
    dh1, dh1c, g_mixn = rms_bwd("mix_norm_b", dn2, h1, rstd2, row(mix_norm), dh2, 0.5)
    dwp4 = jnp.swapaxes(dw_pool.astype(CDT).reshape(G, NCHIP, Cg // NCHIP, Cg), 0, 1)
    parts_mx = [_rs3(p) for p in (dw_in.reshape(NCHIP, D // NCHIP, PW + SW), dwp4,
                                  dw_glu.reshape(NCHIP, SW // NCHIP, SW),
                                  dw_out.reshape(NCHIP, (PW + SW) // NCHIP, D))]
    dwd1, slab_g2 = _ffn_dwd("ffn1", saved1, dh1c, comm=[rs_xc_spec(sums_f2[:1])])
    part_d1 = dwd1.reshape(NCHIP, Fps, D)
    dg1, du1, slab_u2, *recv_b = _ffn_bact("ffn1", saved1, wd1, dh1c,
                                           comm=[rs_xc_spec(sums_f2[1:2]), rs_xh_spec(parts_mx + [part_d1])])
    sums_mx = [add_halves(f"rs_add_mx_{k}", p, r, half) for k, (p, r) in enumerate(zip(parts_mx, recv_b[:4]))]
    sum_d1 = add_halves("rs_add_f1_2", part_d1, recv_b[4], half)
    dwg1, dwu1, slab_d2, slab_d1 = _ffn_dwgu("ffn1", saved1, dg1, du1, comm=[rs_xc_spec([sums_f2[2], sum_d1])])
    parts_gu1 = [dwg1.reshape(NCHIP, Fps, D), dwu1.reshape(NCHIP, Fps, D)]
    recv_gu1 = comm_call("rs_xh_f1", [rs_xh_spec(parts_gu1)])
    sums_gu1 = [add_halves(f"rs_add_f1_{k}", p, r, half) for k, (p, r) in enumerate(zip(parts_gu1, recv_gu1))]
    dn_f1, *slabs_c = _ffn_dn("ffn1", dg1, du1, wg1, wu1, comm=[rs_xc_spec(sums_gu1 + sums_mx)])
    grad_x, _, g_ffn1n = rms_bwd("ffn1_bnorm", dn_f1, saved1[0], saved1[2], row(ffn1_norm), dh1, 1.0)

    order = ["ffn1_gate", "ffn1_up", "ffn1_down", "w_in", "w_pool", "w_glu", "w_out", "ffn2_gate", "ffn2_up",
             "ffn2_down"]
    own_sums = sums_gu1 + [sum_d1] + sums_mx + sums_f2
    slabs = slabs_c[:2] + [slab_d1] + slabs_c[2:] + [slab_g2, slab_u2, slab_d2]
    fulls = [sum_chips("rs_sum_" + k, t, u, place_idx) for k, t, u in zip(order, own_sums, slabs)]
    joined = comm_call("rs_join", [rs_jh_spec(fulls)])
    big = dict(zip(order, joined))
    transposed = ("ffn1_gate", "ffn1_up", "ffn2_gate", "ffn2_up")
    grads = dict(ffn1_gate=big["ffn1_gate"][:Fs], ffn1_up=big["ffn1_up"][:Fs], ffn1_down=big["ffn1_down"][:Fs],
                 w_in=big["w_in"], w_pool=big["w_pool"].reshape(w_pool.shape), w_glu=big["w_glu"], w_out=big["w_out"],
                 ffn2_gate=big["ffn2_gate"][:Fs], ffn2_up=big["ffn2_up"][:Fs], ffn2_down=big["ffn2_down"][:Fs])

    small = dict(ffn1_norm=g_ffn1n, mix_norm=g_mixn, pool_scale=g_pscale, lam_re=g_lam_re, lam_im=g_lam_im,
                 log_dt=g_log_dt, b_re=g_b_re, b_im=g_b_im, c_re=g_c_re, c_im=g_c_im, d_skip=g_dskip,
                 b_glu=g_bglu, pool_out_norm=g_pon, ssm_out_norm=g_son, ffn2_norm=g_ffn2n, final_norm=g_final)
    pieces = [jnp.pad(small[k].reshape(-1), (0, (-small[k].size) % LANES)) for k in small]
    pieces.append(loss_acc.reshape(-1))
    flat = jnp.concatenate(pieces)
    rows = -(-flat.size // (16 * LANES)) * 16
    flat = jnp.pad(flat, (0, rows * LANES - flat.size)).reshape(rows, LANES)
    red = all_reduce_small("all_reduce_small", flat).reshape(-1)
    off = 0
    for k in small:
        size = small[k].size
        grads[k] = red[off:off + size].reshape(weights[k].shape)
        off += size + (-size) % LANES
    loss = (0.5 / D) * red[off]

    deltas, new_m, new_v = {}, {}, {}
    for k in names:
        w = weights[k]
        m, v = moms[k]
        if k in transposed:
            d_, m_, v_ = adamw("adamw_" + k, w.T, grads[k], m.T, v.T)
            deltas[k], new_m[k], new_v[k], grads[k] = d_.T, m_.T, v_.T, grads[k].T
            continue
        if w.ndim >= 2 and w.shape[-1] >= LANES:
            shape2 = (w.size // w.shape[-1], w.shape[-1])
        elif w.size % LANES == 0:
            shape2 = (w.size // LANES, LANES)
        else:
            shape2 = (1, w.size)
        m, v = moms[k]
        d_, m_, v_ = adamw("adamw_" + k, w.reshape(shape2), grads[k].reshape(shape2), m.reshape(shape2),
                           v.reshape(shape2))
        deltas[k], new_m[k], new_v[k] = d_.reshape(w.shape), m_.reshape(w.shape), v_.reshape(w.shape)

    return (loss, grad_x.reshape(x.shape), *[grads[k] for k in names], *[deltas[k] for k in names],
            *[new_m[k] for k in names], *[new_v[k] for k in names])
```

```python
import functools
import math

import jax
import jax.numpy as jnp
from jax import lax
from jax.experimental import pallas as pl
from jax.experimental.pallas import tpu as pltpu

F32 = jnp.float32
CDT = jnp.bfloat16
NORM_EPS = 1e-6
POOL_WINDOWS = (2, 4, 8, 16)
POOL_HALO = 16
SSM_H = 16
SSM_ROWS = 2048
LANES = 128
FF_ALIGN = 256
NCHIP = 4
VMEM_LIMIT = 48 * 1024 * 1024
ADAM_LR = 0.001
ADAM_B1 = 0.9
ADAM_B2 = 0.999
ADAM_EPS = 1e-08
ADAM_WD = 0.01
ADAM_STEP = 10
MESH = pl.DeviceIdType.MESH


def _cp(n_grid):
    return pltpu.CompilerParams(dimension_semantics=("arbitrary",) * n_grid, vmem_limit_bytes=VMEM_LIMIT)


def _tile(n, pref, align=8):
    if n <= pref:
        return n
    t = (pref // align) * align
    while t >= align:
        if n % t == 0:
            return t
        t -= align
    return n


def _silu(x):
    return x * jax.nn.sigmoid(x)


_GELU_C = math.sqrt(2.0 / math.pi)


def _gelu(x):
    return x * (0.5 * (1.0 + jnp.tanh(_GELU_C * (x + 0.044715 * (x * x * x)))))


def _gelu_grad(x):
    t = jnp.tanh(_GELU_C * (x + 0.044715 * (x * x * x)))
    return 0.5 * (1.0 + t) + 0.5 * x * (1.0 - t * t) * (_GELU_C * (1.0 + 3.0 * 0.044715 * x * x))


def _comm_plumb(specs):
    arrays, out_shapes, aliases, scratch = [], [], {}, []
    for sp in specs:
        for i, j in sp["aliases"].items():
            aliases[len(arrays) + i] = len(out_shapes) + j
        arrays += sp["arrays"]
        out_shapes += sp["out_shapes"]
        scratch += [pltpu.SemaphoreType.DMA(sp["sems"]), pltpu.SemaphoreType.DMA(sp["sems"])]

    def make(in_refs, out_refs, sem_refs):
        starts, waits = [], []
        i0 = o0 = 0
        for n, sp in enumerate(specs):
            ni, no = len(sp["arrays"]), len(sp["out_shapes"])
            s, w = sp["make"](in_refs[i0:i0 + ni], out_refs[o0:o0 + no], sem_refs[2 * n], sem_refs[2 * n + 1])
            starts += s
            waits += w
            i0 += ni
            o0 += no
        return starts, waits

    return arrays, out_shapes, aliases, scratch, make


def comm_call(name, specs):
    arrays, out_shapes, aliases, scratch, make = _comm_plumb(specs)
    ni, no = len(arrays), len(out_shapes)

    def body(*refs):
        starts, waits = make(refs[:ni], refs[ni:ni + no], refs[ni + no:])
        for s in starts:
            s()
        for w in waits:
            w()

    return pl.pallas_call(body, name=name, in_specs=[_ANY] * ni, out_specs=[_ANY] * no, out_shape=out_shapes,
                          input_output_aliases=aliases, scratch_shapes=scratch)(*arrays)


def _mm(name, dn, grid, a_list, a_specs, b_list, b_specs, e_list, e_specs, out_shapes, out_specs,
        acc_shape, pair_to_acc, epi, comm=()):
    na, nb, ne, no = len(a_list), len(b_list), len(e_list), len(out_shapes)
    n_pairs = max(na, nb)
    assert na in (1, n_pairs) and nb in (1, n_pairs) and len(pair_to_acc) == n_pairs
    n_acc = max(pair_to_acc) + 1
    nk = grid[2]
    c_arrays, c_out_shapes, c_aliases, c_scratch, c_make = _comm_plumb(comm)
    nci, nco = len(c_arrays), len(c_out_shapes)
    n_in = na + nb + ne
    n_scr = n_acc if nk > 1 else 0

    def body(*refs):
        a_refs = refs[:na]
        b_refs = refs[na:na + nb]
        e_refs = refs[na + nb:n_in]
        o_refs = refs[n_in + nci:n_in + nci + no]
        acc_refs = refs[n_in + nci + no + nco:n_in + nci + no + nco + n_scr]
        if comm:
            starts, waits = c_make(refs[n_in:n_in + nci], refs[n_in + nci + no:n_in + nci + no + nco],
                                   refs[n_in + nci + no + nco + n_scr:])
            gi, gj, gk = pl.program_id(0), pl.program_id(1), pl.program_id(2)

            @pl.when((gi == 0) & (gj == 0) & (gk == 0))
            def _():
                for s in starts:
                    s()

        prods = [None] * n_acc
        a_vals = [a_ref[...].astype(CDT) for a_ref in a_refs]
        b_vals = [b_ref[...].astype(CDT) for b_ref in b_refs]
        for p in range(n_pairs):
            d = lax.dot_general(a_vals[p if na > 1 else 0], b_vals[p if nb > 1 else 0], dn,
                                preferred_element_type=F32)
            q = pair_to_acc[p]
            prods[q] = d if prods[q] is None else prods[q] + d

        def finish(accs):
            outs = epi(accs, [e[...] for e in e_refs])
            for o_ref, o in zip(o_refs, outs):
                o_ref[...] = o.astype(o_ref.dtype)

        if nk == 1:
            finish(prods)
        else:
            k = pl.program_id(2)

            @pl.when(k == 0)
            def _():
                for acc, p in zip(acc_refs, prods):
                    acc[...] = p

            @pl.when(k > 0)
            def _():
                for acc, p in zip(acc_refs, prods):
                    acc[...] += p

            @pl.when(k == nk - 1)
            def _():
                finish([acc[...] for acc in acc_refs])

        if comm:
            @pl.when((gi == grid[0] - 1) & (gj == grid[1] - 1) & (gk == grid[2] - 1))
            def _():
                for w in waits:
                    w()

    scratch = ([pltpu.VMEM(acc_shape, F32) for _ in range(n_acc)] if nk > 1 else []) + c_scratch
    outs = pl.pallas_call(
        body, name=name, grid=grid,
        in_specs=list(a_specs) + list(b_specs) + list(e_specs) + [_ANY] * nci,
        out_specs=list(out_specs) + [_ANY] * nco, out_shape=list(out_shapes) + c_out_shapes,
        input_output_aliases={n_in + i: no + j for i, j in c_aliases.items()},
        scratch_shapes=scratch, compiler_params=_cp(3),
    )(*a_list, *b_list, *e_list, *c_arrays)
    return outs


def _extra_specs(extras, tm, tn):
    arrs, specs = [], []
    for arr, kind, off in extras:
        arrs.append(arr)
        if kind == "mn":
            specs.append(pl.BlockSpec((tm, tn), lambda i, j, k, off=off: (i, j + off)))
        elif kind == "n":
            specs.append(pl.BlockSpec((1, tn), lambda i, j, k, off=off: (0, j + off)))
        else:
            raise ValueError(kind)
    return arrs, specs


def _first(accs, extras):
    return (accs[0],)


def mm_nn(name, a_list, b_list, *, tm, tn, tk=None, b3=False, extras=(), out_dtypes=(F32,), epi=_first,
          pair_to_acc=None, comm=()):
    M, K = a_list[0].shape
    if b3:
        s4, _, ns = b_list[0].shape
        N = s4 * ns
        tn = _tile(ns, tn, LANES)
        nps = ns // tn
    else:
        N = b_list[0].shape[1]
        tn = _tile(N, tn, LANES)
    tm = _tile(M, tm, 16)
    tk = _tile(K, tk or K, LANES)
    grid = (M // tm, N // tn, K // tk)
    a_specs = [pl.BlockSpec((tm, tk), lambda i, j, k: (i, k)) for _ in a_list]
    if b3:
        b_specs = [pl.BlockSpec((None, tk, tn), lambda i, j, k: (j // nps, k, j % nps)) for _ in b_list]
    else:
        b_specs = [pl.BlockSpec((tk, tn), lambda i, j, k: (k, j)) for _ in b_list]
    e_list, e_specs = _extra_specs(extras, tm, tn)
    out_shapes = [jax.ShapeDtypeStruct((M, N), dt) for dt in out_dtypes]
    out_specs = [pl.BlockSpec((tm, tn), lambda i, j, k: (i, j)) for _ in out_dtypes]
    pair_to_acc = pair_to_acc or [0] * max(len(a_list), len(b_list))
    dn = (((1,), (0,)), ((), ()))
    return _mm(name, dn, grid, a_list, a_specs, b_list, b_specs, e_list, e_specs, out_shapes, out_specs,
               (tm, tn), pair_to_acc, epi, comm)


def mm_nt(name, a_list, b_list, *, tm, tn, tk=None, b3=False, extras=(), out_dtypes=(F32,), epi=_first,
          pair_to_acc=None, comm=()):
    M, K = a_list[0].shape
    if b3:
        s4, N, ks = b_list[0].shape
        tk = _tile(ks, tk or ks, LANES)
        kps = ks // tk
    else:
        N = b_list[0].shape[0]
        tk = _tile(K, tk or K, LANES)
    tm = _tile(M, tm, 16)
    tn = _tile(N, tn, LANES)
    grid = (M // tm, N // tn, K // tk)
    a_specs = [pl.BlockSpec((tm, tk), lambda i, j, k: (i, k)) for _ in a_list]
    if b3:
        b_specs = [pl.BlockSpec((None, tn, tk), lambda i, j, k: (k // kps, j, k % kps)) for _ in b_list]
    else:
        b_specs = [pl.BlockSpec((tn, tk), lambda i, j, k: (j, k)) for _ in b_list]
    e_list, e_specs = _extra_specs(extras, tm, tn)
    out_shapes = [jax.ShapeDtypeStruct((M, N), dt) for dt in out_dtypes]
    out_specs = [pl.BlockSpec((tm, tn), lambda i, j, k: (i, j)) for _ in out_dtypes]
    pair_to_acc = pair_to_acc or [0] * max(len(a_list), len(b_list))
    dn = (((1,), (1,)), ((), ()))
    return _mm(name, dn, grid, a_list, a_specs, b_list, b_specs, e_list, e_specs, out_shapes, out_specs,
               (tm, tn), pair_to_acc, epi, comm)


def mm_tn(name, a_list, b_list, *, ti, tj, out_dtype=CDT, comm=()):
    T, I = a_list[0].shape
    J = b_list[0].shape[1]
    n_pairs = max(len(a_list), len(b_list))
    ti = _tile(I, ti, LANES)
    tj = _tile(J, tj, LANES)
    out_shapes = [jax.ShapeDtypeStruct((I, J), out_dtype) for _ in range(n_pairs)]
    out_specs = [pl.BlockSpec((ti, tj), lambda i, j, k: (i, j)) for _ in range(n_pairs)]
    grid = (I // ti, J // tj, 1)
    a_specs = [pl.BlockSpec((T, ti), lambda i, j, k: (0, i)) for _ in a_list]
    b_specs = [pl.BlockSpec((T, tj), lambda i, j, k: (0, j)) for _ in b_list]
    dn = (((0,), (0,)), ((), ()))
    return _mm(name, dn, grid, a_list, a_specs, b_list, b_specs, [], [], out_shapes, out_specs,
               (ti, tj), list(range(n_pairs)), lambda accs, extras: tuple(accs), comm)


def rms_fwd(name, x, gain):
    N, D = x.shape
    tm = _tile(N, 128)

    def body(x_ref, g_ref, n_ref, r_ref):
        xf = x_ref[...]
        r = lax.rsqrt(jnp.mean(xf * xf, axis=-1, keepdims=True) + NORM_EPS)
        n_ref[...] = ((xf * r) * g_ref[...]).astype(n_ref.dtype)
        r_ref[...] = r

    return pl.pallas_call(
        body, name=name, grid=(N // tm,),
        in_specs=[pl.BlockSpec((tm, D), lambda i: (i, 0)), pl.BlockSpec((1, D), lambda i: (0, 0))],
        out_specs=[pl.BlockSpec((tm, D), lambda i: (i, 0)), pl.BlockSpec((tm, 1), lambda i: (i, 0))],
        out_shape=[jax.ShapeDtypeStruct((N, D), CDT), jax.ShapeDtypeStruct((N, 1), F32)],
        compiler_params=_cp(1),
    )(x, gain)


def _rms_bwd_math(dn, x, r, g):
    xhat = x * r
    dgain = jnp.sum(dn * xhat, axis=0, keepdims=True)
    dxhat = dn * g
    dx = r * (dxhat - xhat * jnp.mean(dxhat * xhat, axis=-1, keepdims=True))
    return dx, dgain


def rms_bwd(name, dn, x, rstd, gain, dres, cast_scale):
    N, D = x.shape
    tm = _tile(N, 128)

    def body(dn_ref, x_ref, r_ref, g_ref, dres_ref, dx_ref, dc_ref, dg_ref):
        dx, dgain = _rms_bwd_math(dn_ref[...], x_ref[...], r_ref[...], g_ref[...])
        tot = dres_ref[...] + dx
        dx_ref[...] = tot
        dc_ref[...] = (cast_scale * tot).astype(dc_ref.dtype)

        @pl.when(pl.program_id(0) == 0)
        def _():
            dg_ref[...] = jnp.zeros_like(dg_ref)

        dg_ref[...] += dgain

    row = pl.BlockSpec((tm, D), lambda i: (i, 0))
    vec = pl.BlockSpec((1, D), lambda i: (0, 0))
    return pl.pallas_call(
        body, name=name, grid=(N // tm,),
        in_specs=[row, row, pl.BlockSpec((tm, 1), lambda i: (i, 0)), vec, row],
        out_specs=[row, row, vec],
        out_shape=[jax.ShapeDtypeStruct((N, D), F32), jax.ShapeDtypeStruct((N, D), CDT),
                   jax.ShapeDtypeStruct((1, D), F32)],
        compiler_params=_cp(1),
    )(dn, x, rstd, gain, dres)


def loss_head(name, h, gain, target):
    N, D = h.shape
    tm = _tile(N, 128)

    def body(h_ref, g_ref, t_ref, l_ref, dh_ref, dc_ref, dg_ref):
        x = h_ref[...]
        g = g_ref[...]
        r = lax.rsqrt(jnp.mean(x * x, axis=-1, keepdims=True) + NORM_EPS)
        err = (x * r) * g - t_ref[...]
        dy = err * (1.0 / D)
        dx, dgain = _rms_bwd_math(dy, x, r, g)
        dh_ref[...] = dx
        dc_ref[...] = (0.5 * dx).astype(dc_ref.dtype)

        @pl.when(pl.program_id(0) == 0)
        def _():
            dg_ref[...] = jnp.zeros_like(dg_ref)
            l_ref[...] = jnp.zeros_like(l_ref)

        dg_ref[...] += dgain
        l_ref[...] += jnp.sum(err * err)

    row = pl.BlockSpec((tm, D), lambda i: (i, 0))
    vec = pl.BlockSpec((1, D), lambda i: (0, 0))
    return pl.pallas_call(
        body, name=name, grid=(N // tm,),
        in_specs=[row, vec, row],
        out_specs=[pl.BlockSpec((1, LANES), lambda i: (0, 0)), row, row, vec],
        out_shape=[jax.ShapeDtypeStruct((1, LANES), F32), jax.ShapeDtypeStruct((N, D), F32),
                   jax.ShapeDtypeStruct((N, D), CDT), jax.ShapeDtypeStruct((1, D), F32)],
        compiler_params=_cp(1),
    )(h, gain, target)


def merge_norm_fwd(name, y_pool, y_ssm, g_pool, g_ssm):
    N, PW = y_pool.shape
    SW = y_ssm.shape[1]
    tm = _tile(N, 128)

    def body(yp_ref, ys_ref, gp_ref, gs_ref, m_ref, rp_ref, rs_ref):
        yp = yp_ref[...]
        ys = ys_ref[...]
        rp = lax.rsqrt(jnp.mean(yp * yp, axis=-1, keepdims=True) + NORM_EPS)
        rs = lax.rsqrt(jnp.mean(ys * ys, axis=-1, keepdims=True) + NORM_EPS)
        m_ref[:, :PW] = ((yp * rp) * gp_ref[...]).astype(m_ref.dtype)
        m_ref[:, PW:] = ((ys * rs) * gs_ref[...]).astype(m_ref.dtype)
        rp_ref[...] = rp
        rs_ref[...] = rs

    return pl.pallas_call(
        body, name=name, grid=(N // tm,),
        in_specs=[pl.BlockSpec((tm, PW), lambda i: (i, 0)), pl.BlockSpec((tm, SW), lambda i: (i, 0)),
                  pl.BlockSpec((1, PW), lambda i: (0, 0)), pl.BlockSpec((1, SW), lambda i: (0, 0))],
        out_specs=[pl.BlockSpec((tm, PW + SW), lambda i: (i, 0)), pl.BlockSpec((tm, 1), lambda i: (i, 0)),
                   pl.BlockSpec((tm, 1), lambda i: (i, 0))],
        out_shape=[jax.ShapeDtypeStruct((N, PW + SW), CDT), jax.ShapeDtypeStruct((N, 1), F32),
                   jax.ShapeDtypeStruct((N, 1), F32)],
        compiler_params=_cp(1),
    )(y_pool, y_ssm, g_pool, g_ssm)


def merge_norm_bwd(name, dmerged, y_pool, y_ssm, r_pool, r_ssm, g_pool, g_ssm):
    N, PW = y_pool.shape
    SW = y_ssm.shape[1]
    tm = _tile(N, 128)

    def body(dm_ref, yp_ref, ys_ref, rp_ref, rs_ref, gp_ref, gs_ref, dyp_ref, dys_ref, dgp_ref, dgs_ref):
        dxp, dgp = _rms_bwd_math(dm_ref[:, :PW], yp_ref[...], rp_ref[...], gp_ref[...])
        dxs, dgs = _rms_bwd_math(dm_ref[:, PW:], ys_ref[...], rs_ref[...], gs_ref[...])
        dyp_ref[...] = dxp
        dys_ref[...] = dxs

        @pl.when(pl.program_id(0) == 0)
        def _():
            dgp_ref[...] = jnp.zeros_like(dgp_ref)
            dgs_ref[...] = jnp.zeros_like(dgs_ref)

        dgp_ref[...] += dgp
        dgs_ref[...] += dgs

    col1 = pl.BlockSpec((tm, 1), lambda i: (i, 0))
    return pl.pallas_call(
        body, name=name, grid=(N // tm,),
        in_specs=[pl.BlockSpec((tm, PW + SW), lambda i: (i, 0)), pl.BlockSpec((tm, PW), lambda i: (i, 0)),
                  pl.BlockSpec((tm, SW), lambda i: (i, 0)), col1, col1,
                  pl.BlockSpec((1, PW), lambda i: (0, 0)), pl.BlockSpec((1, SW), lambda i: (0, 0))],
        out_specs=[pl.BlockSpec((tm, PW), lambda i: (i, 0)), pl.BlockSpec((tm, SW), lambda i: (i, 0)),
                   pl.BlockSpec((1, PW), lambda i: (0, 0)), pl.BlockSpec((1, SW), lambda i: (0, 0))],
        out_shape=[jax.ShapeDtypeStruct((N, PW), F32), jax.ShapeDtypeStruct((N, SW), F32),
                   jax.ShapeDtypeStruct((1, PW), F32), jax.ShapeDtypeStruct((1, SW), F32)],
        compiler_params=_cp(1),
    )(dmerged, y_pool, y_ssm, r_pool, r_ssm, g_pool, g_ssm)


def pool_fwd(name, z, w_pool, scale, n_seq, seq):
    G, Cg, _ = w_pool.shape
    N = n_seq * seq
    PW = G * Cg

    def body(z_ref, w_ref, s_ref, d_ref, y_ref, zs_ref):
        g = pl.program_id(0)
        zv = z_ref[...]
        zs_ref[pl.ds(0, POOL_HALO), :] = jnp.zeros((POOL_HALO, Cg), F32)
        zs_ref[pl.ds(POOL_HALO, seq), :] = zv
        t = lax.broadcasted_iota(jnp.int32, (seq, 1), 0)
        for gi, w in enumerate(POOL_WINDOWS):
            @pl.when(g == gi)
            def _(w=w):
                acc = zv
                for j in range(1, w):
                    acc = acc + zs_ref[pl.ds(POOL_HALO - j, seq), :]
                cnt = jnp.minimum(t + 1, w).astype(F32)
                d = (acc / cnt - zv).astype(d_ref.dtype)
                d_ref[...] = d
                out = jnp.dot(d, w_ref[...], preferred_element_type=F32)
                y_ref[...] = out * s_ref[...]

    blk = pl.BlockSpec((seq, Cg), lambda g, b: (b, g))
    return pl.pallas_call(
        body, name=name, grid=(G, n_seq),
        in_specs=[blk, pl.BlockSpec((None, Cg, Cg), lambda g, b: (g, 0, 0)),
                  pl.BlockSpec((1, Cg), lambda g, b: (0, g))],
        out_specs=[blk, blk],
        out_shape=[jax.ShapeDtypeStruct((N, PW), CDT), jax.ShapeDtypeStruct((N, PW), F32)],
        scratch_shapes=[pltpu.VMEM((seq + POOL_HALO, Cg), F32)],
        compiler_params=_cp(2),
    )(z, w_pool, scale)


def pool_bwd(name, dy_pool, d, w_pool, scale, n_seq, seq):
    G, Cg, _ = w_pool.shape
    N = n_seq * seq
    PW = G * Cg

    def body(dy_ref, d_ref, w_ref, s_ref, dz_ref, dw_ref, ds_ref, es_ref):
        g = pl.program_id(0)
        b = pl.program_id(1)
        dv = d_ref[...]
        wv = w_ref[...]
        dy = dy_ref[...]
        out = jnp.dot(dv, wv, preferred_element_type=F32)
        dout = (dy * s_ref[...]).astype(CDT)
        dw = lax.dot_general(dv, dout, (((0,), (0,)), ((), ())), preferred_element_type=F32)
        dd = lax.dot_general(dout, wv, (((1,), (1,)), ((), ())), preferred_element_type=F32)

        @pl.when(b == 0)
        def _():
            dw_ref[...] = jnp.zeros_like(dw_ref)
            ds_ref[...] = jnp.zeros_like(ds_ref)

        dw_ref[...] += dw
        ds_ref[...] += jnp.sum(out * dy, axis=0, keepdims=True)
        t = lax.broadcasted_iota(jnp.int32, (seq, 1), 0)
        es_ref[pl.ds(seq, POOL_HALO), :] = jnp.zeros((POOL_HALO, Cg), F32)
        for gi, w in enumerate(POOL_WINDOWS):
            @pl.when(g == gi)
            def _(w=w):
                cnt = jnp.minimum(t + 1, w).astype(F32)
                e = dd / cnt
                es_ref[pl.ds(0, seq), :] = e
                acc = e
                for j in range(1, w):
                    acc = acc + es_ref[pl.ds(j, seq), :]
                dz_ref[...] = (acc - dd).astype(dz_ref.dtype)

    blk = pl.BlockSpec((seq, Cg), lambda g, b: (b, g))
    return pl.pallas_call(
        body, name=name, grid=(G, n_seq),
        in_specs=[blk, blk, pl.BlockSpec((None, Cg, Cg), lambda g, b: (g, 0, 0)),
                  pl.BlockSpec((1, Cg), lambda g, b: (0, g))],
        out_specs=[blk, pl.BlockSpec((None, Cg, Cg), lambda g, b: (g, 0, 0)),
                   pl.BlockSpec((1, Cg), lambda g, b: (0, g))],
        out_shape=[jax.ShapeDtypeStruct((N, PW), CDT), jax.ShapeDtypeStruct((G, Cg, Cg), F32),
                   jax.ShapeDtypeStruct((1, PW), F32)],
        scratch_shapes=[pltpu.VMEM((seq + POOL_HALO, Cg), F32)],
        compiler_params=_cp(2),
    )(dy_pool, d, w_pool, scale)


def _ssm_param_math(lr, li, ldt, br, bi):
    dt = jnp.exp(ldt)
    mag = jnp.exp(lr * dt)
    ar = mag * jnp.cos(li * dt)
    ai = mag * jnp.sin(li * dt)
    den = lr * lr + li * li
    xr = ar - 1.0
    cr = (xr * lr + ai * li) / den
    ci = (ai * lr - xr * li) / den
    return ar, ai, cr * br - ci * bi, cr * bi + ci * br


def ssm_params_fwd(name, lr, li, ldt, br, bi):
    GP, H = br.shape

    def body(lr_ref, li_ref, ldt_ref, br_ref, bi_ref, ar_ref, ai_ref, bbr_ref, bbi_ref):
        ar, ai, bbr, bbi = _ssm_param_math(lr_ref[...], li_ref[...], ldt_ref[...], br_ref[...], bi_ref[...])
        ar_ref[...] = ar
        ai_ref[...] = ai
        bbr_ref[...] = bbr
        bbi_ref[...] = bbi

    c1 = jax.ShapeDtypeStruct((GP, 1), F32)
    ch = jax.ShapeDtypeStruct((GP, H), F32)
    tr = _tile(GP, 512)
    b1 = pl.BlockSpec((tr, 1), lambda i: (i, 0))
    bh = pl.BlockSpec((tr, H), lambda i: (i, 0))
    return pl.pallas_call(body, name=name, grid=(GP // tr,), in_specs=[b1, b1, b1, bh, bh],
                          out_specs=[b1, b1, bh, bh], out_shape=[c1, c1, ch, ch],
                          compiler_params=_cp(1))(lr, li, ldt, br, bi)


def ssm_params_bwd(name, lr, li, ldt, br, bi, dar, dai, dbbr, dbbi):
    GP, H = br.shape

    def body(lr_ref, li_ref, ldt_ref, br_ref, bi_ref, dar_ref, dai_ref, dbbr_ref, dbbi_ref,
             dlr_ref, dli_ref, dldt_ref, dbr_ref, dbi_ref):
        _, vjp = jax.vjp(_ssm_param_math, lr_ref[...], li_ref[...], ldt_ref[...], br_ref[...], bi_ref[...])
        dlr, dli, dldt, dbr, dbi = vjp((dar_ref[...], dai_ref[...], dbbr_ref[...], dbbi_ref[...]))
        dlr_ref[...] = dlr
        dli_ref[...] = dli
        dldt_ref[...] = dldt
        dbr_ref[...] = dbr
        dbi_ref[...] = dbi

    c1 = jax.ShapeDtypeStruct((GP, 1), F32)
    ch = jax.ShapeDtypeStruct((GP, H), F32)
    tr = _tile(GP, 512)
    b1 = pl.BlockSpec((tr, 1), lambda i: (i, 0))
    bh = pl.BlockSpec((tr, H), lambda i: (i, 0))
    return pl.pallas_call(body, name=name, grid=(GP // tr,), in_specs=[b1, b1, b1, bh, bh, b1, b1, bh, bh],
                          out_specs=[b1, b1, b1, bh, bh], out_shape=[c1, c1, c1, ch, ch],
                          compiler_params=_cp(1))(lr, li, ldt, br, bi, dar, dai, dbbr, dbbi)


def ssm_expand(name, src, col_off, mat, tm):
    N = src.shape[0]
    NC, cw, w2 = mat.shape
    W = w2 // 2
    tm = _tile(N, tm)

    def body(s_ref, m_ref, re_ref, im_ref):
        r = jnp.dot(s_ref[...].astype(CDT), m_ref[...], preferred_element_type=F32)
        re_ref[...] = r[:, :W]
        im_ref[...] = r[:, W:]

    out = pl.BlockSpec((tm, W), lambda i, c: (i, c))
    return pl.pallas_call(
        body, name=name, grid=(N // tm, NC),
        in_specs=[pl.BlockSpec((tm, cw), lambda i, c: (i, c + col_off)),
                  pl.BlockSpec((None, cw, w2), lambda i, c: (c, 0, 0))],
        out_specs=[out, out],
        out_shape=[jax.ShapeDtypeStruct((N, NC * W), F32)] * 2,
        compiler_params=_cp(2),
    )(src, mat)


def ssm_contract(name, re, im, mat, extras, epi, out_dtypes, tm):
    N = re.shape[0]
    NC, w2, cw = mat.shape
    W = w2 // 2
    tm = _tile(N, tm)
    ne = len(extras)

    def body(*refs):
        re_ref, im_ref, m_ref = refs[:3]
        e_refs = refs[3:3 + ne]
        o_refs = refs[3 + ne:]
        acc = jnp.dot(re_ref[...].astype(CDT), m_ref[pl.ds(0, W), :], preferred_element_type=F32)
        acc = acc + jnp.dot(im_ref[...].astype(CDT), m_ref[pl.ds(W, W), :], preferred_element_type=F32)
        outs = epi(acc, [e[...] for e in e_refs])
        for o_ref, o in zip(o_refs, outs):
            o_ref[...] = o.astype(o_ref.dtype)

    e_arrs, e_specs = [], []
    for arr, kind, off in extras:
        e_arrs.append(arr)
        if kind == "mn":
            e_specs.append(pl.BlockSpec((tm, cw), lambda i, c, off=off: (i, c + off)))
        else:
            e_specs.append(pl.BlockSpec((1, cw), lambda i, c, off=off: (0, c + off)))
    blk = pl.BlockSpec((tm, W), lambda i, c: (i, c))
    return pl.pallas_call(
        body, name=name, grid=(N // tm, NC),
        in_specs=[blk, blk, pl.BlockSpec((None, w2, cw), lambda i, c: (c, 0, 0))] + e_specs,
        out_specs=[pl.BlockSpec((tm, cw), lambda i, c: (i, c)) for _ in out_dtypes],
        out_shape=[jax.ShapeDtypeStruct((N, NC * cw), dt) for dt in out_dtypes],
        compiler_params=_cp(2),
    )(re, im, mat, *e_arrs)


def ssm_outer(name, src, col_off, re, im, NC, Wc, tm):
    N = src.shape[0]
    tm = _tile(N, tm)

    def body(s_ref, re_ref, im_ref, o1_ref, o2_ref):
        @pl.when(pl.program_id(1) == 0)
        def _():
            o1_ref[...] = jnp.zeros_like(o1_ref)
            o2_ref[...] = jnp.zeros_like(o2_ref)

        sv = s_ref[...].astype(CDT)
        dn = (((0,), (0,)), ((), ()))
        o1_ref[...] += lax.dot_general(sv, re_ref[...].astype(CDT), dn, preferred_element_type=F32)
        o2_ref[...] += lax.dot_general(sv, im_ref[...].astype(CDT), dn, preferred_element_type=F32)

    blk = pl.BlockSpec((tm, Wc), lambda c, i: (i, c))
    oblk = pl.BlockSpec((None, LANES, Wc), lambda c, i: (c, 0, 0))
    return pl.pallas_call(
        body, name=name, grid=(NC, N // tm),
        in_specs=[pl.BlockSpec((tm, LANES), lambda c, i: (i, c + col_off)), blk, blk],
        out_specs=[oblk, oblk],
        out_shape=[jax.ShapeDtypeStruct((NC, LANES, Wc), F32)] * 2,
        compiler_params=_cp(2),
    )(src, re, im)


def ssm_scan_fwd(name, bur, bui, ar, ai, n_seq, seq, tc):
    N, R, L = bur.shape
    tc = _tile(seq, tc, 1)
    nt = seq // tc

    def body(bur_ref, bui_ref, ar_ref, ai_ref, xr_ref, xi_ref, st_ref):
        @pl.when(pl.program_id(1) == 0)
        def _():
            st_ref[...] = jnp.zeros_like(st_ref)

        a_r = ar_ref[...]
        a_i = ai_ref[...]

        def step(t, carry):
            xr, xi = carry
            nr = a_r * xr - a_i * xi + bur_ref[t]
            ni = a_r * xi + a_i * xr + bui_ref[t]
            xr_ref[t] = nr
            xi_ref[t] = ni
            return nr, ni

        xr, xi = lax.fori_loop(0, tc, step, (st_ref[0], st_ref[1]))
        st_ref[0] = xr
        st_ref[1] = xi

    blk = pl.BlockSpec((tc, R, L), lambda b, j: (b * nt + j, 0, 0))
    par = pl.BlockSpec((R, L), lambda b, j: (0, 0))
    return pl.pallas_call(
        body, name=name, grid=(n_seq, nt),
        in_specs=[blk, blk, par, par], out_specs=[blk, blk],
        out_shape=[jax.ShapeDtypeStruct((N, R, L), F32)] * 2,
        scratch_shapes=[pltpu.VMEM((2, R, L), F32)],
        compiler_params=_cp(2),
    )(bur, bui, ar, ai)


def ssm_scan_bwd(name, gr, gi, xr, xi, ar, ai, n_seq, seq, tc):
    N, R, L = gr.shape
    tc = _tile(seq, tc, 1)
    nt = seq // tc

    def body(gr_ref, gi_ref, xr_ref, xi_ref, ar_ref, ai_ref, lr_ref, li_ref, dar_ref, dai_ref, st_ref):
        b = pl.program_id(0)
        j = pl.program_id(1)

        @pl.when((b == 0) & (j == 0))
        def _():
            dar_ref[...] = jnp.zeros_like(dar_ref)
            dai_ref[...] = jnp.zeros_like(dai_ref)

        @pl.when(j == 0)
        def _():
            st_ref[...] = jnp.zeros_like(st_ref)

        a_r = ar_ref[...]
        a_i = ai_ref[...]

        def step(s, carry):
            t = tc - 1 - s
            lr, li, dr, di = carry
            xrt = xr_ref[t]
            xit = xi_ref[t]
            dr = dr + (lr * xrt + li * xit)
            di = di + (li * xrt - lr * xit)
            nlr = gr_ref[t] + (a_r * lr + a_i * li)
            nli = gi_ref[t] + (a_r * li - a_i * lr)
            lr_ref[t] = nlr
            li_ref[t] = nli
            return nlr, nli, dr, di

        lr, li, dr, di = lax.fori_loop(0, tc, step, (st_ref[0], st_ref[1], dar_ref[...], dai_ref[...]))
        st_ref[0] = lr
        st_ref[1] = li
        dar_ref[...] = dr
        dai_ref[...] = di

    blk = pl.BlockSpec((tc, R, L), lambda b, j: (b * nt + nt - 1 - j, 0, 0))
    par = pl.BlockSpec((R, L), lambda b, j: (0, 0))
    return pl.pallas_call(
        body, name=name, grid=(n_seq, nt),
        in_specs=[blk, blk, blk, blk, par, par], out_specs=[blk, blk, par, par],
        out_shape=[jax.ShapeDtypeStruct((N, R, L), F32)] * 2 + [jax.ShapeDtypeStruct((R, L), F32)] * 2,
        scratch_shapes=[pltpu.VMEM((2, R, L), F32)],
        compiler_params=_cp(2),
    )(gr, gi, xr, xi, ar, ai)


def glu_bwd_pre(name, dy_ssm, y_pre, q):
    N, SW = y_pre.shape
    tm = _tile(N, 128)

    def body(dy_ref, y_ref, q_ref, dq_ref, dyg_ref, db_ref):
        dy = dy_ref[...]
        yg = _gelu(y_ref[...])
        s = jax.nn.sigmoid(q_ref[...])
        dq = dy * yg * (s * (1.0 - s))
        dq_ref[...] = dq.astype(dq_ref.dtype)
        dyg_ref[...] = dy * s

        @pl.when(pl.program_id(0) == 0)
        def _():
            db_ref[...] = jnp.zeros_like(db_ref)

        db_ref[...] += jnp.sum(dq, axis=0, keepdims=True)

    row = pl.BlockSpec((tm, SW), lambda i: (i, 0))
    vec = pl.BlockSpec((1, SW), lambda i: (0, 0))
    return pl.pallas_call(
        body, name=name, grid=(N // tm,), in_specs=[row, row, row], out_specs=[row, row, vec],
        out_shape=[jax.ShapeDtypeStruct((N, SW), CDT), jax.ShapeDtypeStruct((N, SW), F32),
                   jax.ShapeDtypeStruct((1, SW), F32)],
        compiler_params=_cp(1),
    )(dy_ssm, y_pre, q)


def colsum_prod(name, a, b, b_col_off, width):
    N = a.shape[0]
    tm = _tile(N, 128)

    def body(a_ref, b_ref, o_ref):
        @pl.when(pl.program_id(0) == 0)
        def _():
            o_ref[...] = jnp.zeros_like(o_ref)

        o_ref[...] += jnp.sum(a_ref[...] * b_ref[...], axis=0, keepdims=True)

    return pl.pallas_call(
        body, name=name, grid=(N // tm,),
        in_specs=[pl.BlockSpec((tm, width), lambda i: (i, 0)),
                  pl.BlockSpec((tm, width), lambda i: (i, b_col_off))],
        out_specs=pl.BlockSpec((1, width), lambda i: (0, 0)),
        out_shape=jax.ShapeDtypeStruct((1, width), F32),
        compiler_params=_cp(1),
    )(a, b)


def adamw(name, w, g, m, v):
    R, C = w.shape
    tr = _tile(R, max(8, (1 << 18) // C))

    def body(w_ref, g_ref, m_ref, v_ref, d_ref, nm_ref, nv_ref):
        gv = g_ref[...]
        nm = ADAM_B1 * m_ref[...] + (1.0 - ADAM_B1) * gv
        nv = ADAM_B2 * v_ref[...] + (1.0 - ADAM_B2) * jnp.square(gv)
        m_hat = nm / (1.0 - ADAM_B1 ** ADAM_STEP)
        v_hat = nv / (1.0 - ADAM_B2 ** ADAM_STEP)
        d_ref[...] = -ADAM_LR * (m_hat / (jnp.sqrt(v_hat) + ADAM_EPS) + ADAM_WD * w_ref[...])
        nm_ref[...] = nm
        nv_ref[...] = nv

    blk = pl.BlockSpec((tr, C), lambda i: (i, 0))
    sh = jax.ShapeDtypeStruct((R, C), F32)
    return pl.pallas_call(body, name=name, grid=(R // tr,), in_specs=[blk] * 4, out_specs=[blk] * 3,
                          out_shape=[sh] * 3, compiler_params=_cp(1))(w, g, m, v)


def add_halves(name, part, recv, c_idx):
    S4, R, C = part.shape
    h = R // 2
    tr = _tile(h, max(16, (1 << 19) // C), 16)
    nb = h // tr

    def body(c_ref, p_ref, r_ref, o_ref):
        o_ref[...] = (p_ref[...].astype(F32) + r_ref[...].astype(F32)).astype(o_ref.dtype)

    grid_spec = pltpu.PrefetchScalarGridSpec(
        num_scalar_prefetch=1, grid=(S4, nb),
        in_specs=[pl.BlockSpec((None, tr, C), lambda s, i, c_ref: (s, c_ref[0] * nb + i, 0)),
                  pl.BlockSpec((None, tr, C), lambda s, i, c_ref: (s, i, 0))],
        out_specs=pl.BlockSpec((None, tr, C), lambda s, i, c_ref: (s, i, 0)),
    )
    return pl.pallas_call(body, name=name, grid_spec=grid_spec,
                          out_shape=jax.ShapeDtypeStruct((S4, h, C), CDT),
                          compiler_params=_cp(2))(c_idx, part, recv)


def sum_chips(name, t, u, idx):
    _, h, C = t.shape
    tr = _tile(h, max(16, (1 << 18) // C), 16)
    nb = h // tr

    def body(idx_ref, t_ref, u_ref, o_ref):
        acc = t_ref[...].astype(F32)
        for r in range(NCHIP - 1):
            acc = acc + u_ref[r].astype(F32)
        o_ref[...] = acc

    grid_spec = pltpu.PrefetchScalarGridSpec(
        num_scalar_prefetch=1, grid=(nb,),
        in_specs=[pl.BlockSpec((None, tr, C), lambda i, idx_ref: (idx_ref[0], i, 0)),
                  pl.BlockSpec((NCHIP - 1, tr, C), lambda i, idx_ref: (0, i, 0))],
        out_specs=pl.BlockSpec((tr, C), lambda i, idx_ref: (idx_ref[1] * nb + i, 0)),
    )
    return pl.pallas_call(body, name=name, grid_spec=grid_spec,
                          out_shape=jax.ShapeDtypeStruct((2 * h, C), F32), compiler_params=_cp(1))(idx, t, u)


def _place():
    x, y, c = lax.axis_index("x"), lax.axis_index("y"), lax.axis_index("c")
    chips = [(1 - x, y), (x, 1 - y), (1 - x, 1 - y)]
    return x, y, c, chips


_ANY = pl.BlockSpec(memory_space=pl.ANY)


def gather_shards(name, shards):
    n = len(shards)

    def body(*refs):
        ins = refs[:n]
        outs = refs[n:2 * n]
        send_sems, recv_sems = refs[2 * n:]
        x, y, c, chips = _place()
        me = 2 * x + y
        sib = (x, y, 1 - c)
        sends = []
        for k in range(n):
            h = ins[k].shape[0] // 2
            cp = pltpu.make_async_remote_copy(
                src_ref=ins[k], dst_ref=outs[k].at[me], send_sem=send_sems.at[k, 6], recv_sem=recv_sems.at[k, 6],
                device_id=sib, device_id_type=MESH)
            cp.start()
            sends.append(cp)
            for r, (qx, qy) in enumerate(chips):
                cp = pltpu.make_async_remote_copy(
                    src_ref=ins[k].at[pl.ds(c * h, h)], dst_ref=outs[k].at[me, pl.ds(c * h, h)],
                    send_sem=send_sems.at[k, r], recv_sem=recv_sems.at[k, r],
                    device_id=(qx, qy, c), device_id_type=MESH)
                cp.start()
                sends.append(cp)
        for k in range(n):
            h = ins[k].shape[0] // 2
            for r, (qx, qy) in enumerate(chips):
                q = 2 * qx + qy
                region = outs[k].at[q, pl.ds(c * h, h)]
                pltpu.make_async_remote_copy(
                    src_ref=region, dst_ref=region, send_sem=send_sems.at[k, r], recv_sem=recv_sems.at[k, r],
                    device_id=(qx, qy, c), device_id_type=MESH).wait_recv()
                cp = pltpu.make_async_remote_copy(
                    src_ref=region, dst_ref=region, send_sem=send_sems.at[k, 3 + r],
                    recv_sem=recv_sems.at[k, 3 + r], device_id=sib, device_id_type=MESH)
                cp.start()
                sends.append(cp)
        for k in range(n):
            h = ins[k].shape[0] // 2
            for r, (qx, qy) in enumerate(chips):
                q = 2 * qx + qy
                region = outs[k].at[q, pl.ds((1 - c) * h, h)]
                pltpu.make_async_remote_copy(
                    src_ref=region, dst_ref=region, send_sem=send_sems.at[k, 3 + r],
                    recv_sem=recv_sems.at[k, 3 + r], device_id=sib, device_id_type=MESH).wait_recv()
            own = outs[k].at[me]
            pltpu.make_async_remote_copy(
                src_ref=own, dst_ref=own, send_sem=send_sems.at[k, 6], recv_sem=recv_sems.at[k, 6],
                device_id=sib, device_id_type=MESH).wait_recv()
        for cp in sends:
            cp.wait_send()

    return pl.pallas_call(
        body, name=name,
        in_specs=[_ANY] * n, out_specs=[_ANY] * n,
        out_shape=[jax.ShapeDtypeStruct((NCHIP,) + s.shape, s.dtype) for s in shards],
        scratch_shapes=[pltpu.SemaphoreType.DMA((n, 7)), pltpu.SemaphoreType.DMA((n, 7))],
    )(*shards)


def _remote(src, dst, ssem, rsem, dev):
    return pltpu.make_async_remote_copy(src_ref=src, dst_ref=dst, send_sem=ssem, recv_sem=rsem,
                                        device_id=dev, device_id_type=MESH)


def _spec(arrays, out_shapes, aliases, sem_cols, make):
    return dict(arrays=list(arrays), out_shapes=list(out_shapes), aliases=dict(aliases),
                sems=(len(arrays), sem_cols), make=make)


def gather_send_spec(shards):
    n = len(shards)

    def make(ins, outs, ss, rs):
        x, y, c, chips = _place()
        me = 2 * x + y
        sib = (x, y, 1 - c)
        starts, waits = [], []
        for k in range(n):
            h = ins[k].shape[0] // 2
            own = outs[k].at[me]
            full = _remote(ins[k], own, ss.at[k, 3], rs.at[k, 3], sib)
            starts.append(full.start)
            waits += [full.wait_send, _remote(own, own, ss.at[k, 3], rs.at[k, 3], sib).wait_recv]
            for r, (qx, qy) in enumerate(chips):
                cp = _remote(ins[k].at[pl.ds(c * h, h)], outs[k].at[me, pl.ds(c * h, h)],
                             ss.at[k, r], rs.at[k, r], (qx, qy, c))
                region = outs[k].at[2 * qx + qy, pl.ds(c * h, h)]
                starts.append(cp.start)
                waits += [cp.wait_send, _remote(region, region, ss.at[k, r], rs.at[k, r], (qx, qy, c)).wait_recv]
        return starts, waits

    return _spec(shards, [jax.ShapeDtypeStruct((NCHIP,) + s.shape, s.dtype) for s in shards], {}, 4, make)


def gather_pass_spec(bufs):
    n = len(bufs)

    def make(ins, outs, ss, rs):
        x, y, c, chips = _place()
        sib = (x, y, 1 - c)
        starts, waits = [], []
        for k in range(n):
            h = outs[k].shape[1] // 2
            for r, (qx, qy) in enumerate(chips):
                q = 2 * qx + qy
                region = outs[k].at[q, pl.ds(c * h, h)]
                other = outs[k].at[q, pl.ds((1 - c) * h, h)]
                cp = _remote(region, region, ss.at[k, r], rs.at[k, r], sib)
                starts.append(cp.start)
                waits += [cp.wait_send, _remote(other, other, ss.at[k, r], rs.at[k, r], sib).wait_recv]
        return starts, waits

    return _spec(bufs, [jax.ShapeDtypeStruct(b.shape, b.dtype) for b in bufs], {k: k for k in range(n)}, 3, make)


def rs_xh_spec(parts):
    n = len(parts)

    def make(ins, outs, ss, rs):
        x, y, c, _ = _place()
        sib = (x, y, 1 - c)
        starts, waits = [], []
        for k in range(n):
            h = ins[k].shape[1] // 2
            cp = _remote(ins[k].at[:, pl.ds((1 - c) * h, h)], outs[k], ss.at[k, 0], rs.at[k, 0], sib)
            starts.append(cp.start)
            waits.append(cp.wait)
        return starts, waits

    shapes = [jax.ShapeDtypeStruct((p.shape[0], p.shape[1] // 2) + p.shape[2:], p.dtype) for p in parts]
    return _spec(parts, shapes, {}, 1, make)


def rs_xc_spec(sums):
    n = len(sums)

    def make(ins, outs, ss, rs):
        x, y, c, chips = _place()
        starts, waits = [], []
        for k in range(n):
            for r, (qx, qy) in enumerate(chips):
                cp = _remote(ins[k].at[2 * qx + qy], outs[k].at[r], ss.at[k, r], rs.at[k, r], (qx, qy, c))
                starts.append(cp.start)
                waits.append(cp.wait)
        return starts, waits

    return _spec(sums, [jax.ShapeDtypeStruct((NCHIP - 1,) + s.shape[1:], s.dtype) for s in sums], {}, 3, make)


def rs_jh_spec(fulls):
    n = len(fulls)

    def make(ins, outs, ss, rs):
        x, y, c, _ = _place()
        sib = (x, y, 1 - c)
        starts, waits = [], []
        for k in range(n):
            h = outs[k].shape[0] // 2
            mine = outs[k].at[pl.ds(c * h, h)]
            other = outs[k].at[pl.ds((1 - c) * h, h)]
            cp = _remote(mine, mine, ss.at[k, 0], rs.at[k, 0], sib)
            starts.append(cp.start)
            waits += [cp.wait_send, _remote(other, other, ss.at[k, 0], rs.at[k, 0], sib).wait_recv]
        return starts, waits

    return _spec(fulls, [jax.ShapeDtypeStruct(f.shape, f.dtype) for f in fulls], {k: k for k in range(n)}, 1, make)


def all_reduce_small(name, buf):
    R, L = buf.shape
    h = R // 2

    def body(x_ref, o_ref, sib_ref, chip_ref, send_sems, recv_sems):
        x, y, c, chips = _place()
        me = 2 * x + y
        sib = (x, y, 1 - c)
        first = pltpu.make_async_remote_copy(src_ref=x_ref, dst_ref=sib_ref, send_sem=send_sems.at[0],
                                             recv_sem=recv_sems.at[0], device_id=sib, device_id_type=MESH)
        first.start()
        first.wait()
        mine = pl.ds(pl.multiple_of(c * h, 8), h)
        other = pl.ds(pl.multiple_of((1 - c) * h, 8), h)
        chip_ref[me] = x_ref[mine, :] + sib_ref[mine, :]
        cps = []
        for r, (qx, qy) in enumerate(chips):
            cp = pltpu.make_async_remote_copy(
                src_ref=chip_ref.at[me], dst_ref=chip_ref.at[me], send_sem=send_sems.at[1 + r],
                recv_sem=recv_sems.at[1 + r], device_id=(qx, qy, c), device_id_type=MESH)
            cp.start()
            cps.append(cp)
        for r, (qx, qy) in enumerate(chips):
            q = 2 * qx + qy
            pltpu.make_async_remote_copy(
                src_ref=chip_ref.at[q], dst_ref=chip_ref.at[q], send_sem=send_sems.at[1 + r],
                recv_sem=recv_sems.at[1 + r], device_id=(qx, qy, c), device_id_type=MESH).wait_recv()
        for cp in cps:
            cp.wait_send()
        o_ref[mine, :] = ((chip_ref[0] + chip_ref[1]) + chip_ref[2]) + chip_ref[3]
        last = pltpu.make_async_remote_copy(src_ref=o_ref.at[mine], dst_ref=o_ref.at[mine],
                                            send_sem=send_sems.at[4], recv_sem=recv_sems.at[4],
                                            device_id=sib, device_id_type=MESH)
        last.start()
        last.wait_send()
        pltpu.make_async_remote_copy(src_ref=o_ref.at[other], dst_ref=o_ref.at[other],
                                     send_sem=send_sems.at[4], recv_sem=recv_sems.at[4],
                                     device_id=sib, device_id_type=MESH).wait_recv()

    vm = pl.BlockSpec(memory_space=pltpu.VMEM)
    return pl.pallas_call(
        body, name=name, in_specs=[vm], out_specs=vm,
        out_shape=jax.ShapeDtypeStruct((R, L), F32),
        scratch_shapes=[pltpu.VMEM((R, L), F32), pltpu.VMEM((NCHIP, h, L), F32),
                        pltpu.SemaphoreType.DMA((5,)), pltpu.SemaphoreType.DMA((5,))],
        compiler_params=pltpu.CompilerParams(vmem_limit_bytes=VMEM_LIMIT),
    )(buf)


def _ffn_fwd(tag, h, gain, wg3, wu3, wd, up_comm=(), down_comm=None, wd_from_up=None):
    n, rstd = rms_fwd(tag + "_norm", h, gain)

    def epi(accs, extras):
        g, u = accs
        return g, u, _silu(g) * u

    g, u, a, *up_res = mm_nt(tag + "_up", [n], [wg3, wu3], tm=1024, tn=256, out_dtypes=(CDT, CDT, CDT),
                             epi=epi, pair_to_acc=[0, 1], comm=up_comm)

    def epi_down(accs, extras):
        return (extras[0] + 0.5 * accs[0],)

    if wd_from_up is not None:
        wd = wd_from_up(up_res)
    h_out, *down_res = mm_nn(tag + "_down", [a], [wd], tm=1024, tn=512, tk=wd.shape[0] // NCHIP,
                             extras=[(h, "mn", 0)], epi=epi_down, comm=down_comm(up_res) if down_comm else ())
    return h_out, (h, n, rstd, g, u, a), up_res, down_res, wd


def _ffn_bact(tag, saved, wd, dfb, comm=()):
    _, _, _, g, u, _ = saved

    def epi_act(accs, extras):
        da = accs[0]
        gv = extras[0].astype(F32)
        uv = extras[1].astype(F32)
        s = jax.nn.sigmoid(gv)
        return da * uv * (s * (1.0 + gv * (1.0 - s))), da * (gv * s)

    return mm_nt(tag + "_bact", [dfb], [wd], tm=1024, tn=256, extras=[(g, "mn", 0), (u, "mn", 0)],
                 out_dtypes=(CDT, CDT), epi=epi_act, comm=comm)


def _ffn_dwd(tag, saved, dfb, comm=()):
    return mm_tn(tag + "_dwd", [saved[5]], [dfb], ti=512, tj=512, comm=comm)


def _ffn_dwgu(tag, saved, dg, du, comm=()):
    return mm_tn(tag + "_dwgu", [dg, du], [saved[1]], ti=512, tj=512, comm=comm)


def _ffn_dn(tag, dg, du, wg3, wu3, comm=()):
    return mm_nn(tag + "_dn", [dg, du], [wg3, wu3], tm=1024, tn=512, tk=wg3.shape[0] // (2 * NCHIP),
                 pair_to_acc=[0, 0], comm=comm)


def _rs3(p):
    return p.reshape(p.shape[0], -1, p.shape[-1])


def _blockdiag(m, gpc):
    G, a, b = m.shape
    nc = G // gpc
    mask = jnp.eye(gpc, dtype=m.dtype)[None, :, None, :, None]
    out = m.reshape(nc, gpc, a, 1, b) * mask
    return out.reshape(nc, gpc * a, gpc * b)


def _diag_blocks(o, gpc, a, b):
    nc = o.shape[0]
    o5 = o.reshape(nc, gpc, a, gpc, b)
    mask = jnp.eye(gpc, dtype=o.dtype)[None, :, None, :, None]
    return jnp.sum(o5 * mask, axis=3).reshape(nc * gpc, a, b)


def kernel(x, ffn1_norm, ffn1_gate, ffn1_up, ffn1_down, mix_norm, w_in, w_pool, pool_scale, lam_re, lam_im, log_dt, b_re, b_im, c_re, c_im, d_skip, w_glu, b_glu, pool_out_norm, ssm_out_norm, w_out, ffn2_norm, ffn2_gate, ffn2_up, ffn2_down, final_norm, loss_target, m_ffn1_norm, m_ffn1_gate, m_ffn1_up, m_ffn1_down, m_mix_norm, m_w_in, m_w_pool, m_pool_scale, m_lam_re, m_lam_im, m_log_dt, m_b_re, m_b_im, m_c_re, m_c_im, m_d_skip, m_w_glu, m_b_glu, m_pool_out_norm, m_ssm_out_norm, m_w_out, m_ffn2_norm, m_ffn2_gate, m_ffn2_up, m_ffn2_down, m_final_norm, v_ffn1_norm, v_ffn1_gate, v_ffn1_up, v_ffn1_down, v_mix_norm, v_w_in, v_w_pool, v_pool_scale, v_lam_re, v_lam_im, v_log_dt, v_b_re, v_b_im, v_c_re, v_c_im, v_d_skip, v_w_glu, v_b_glu, v_pool_out_norm, v_ssm_out_norm, v_w_out, v_ffn2_norm, v_ffn2_gate, v_ffn2_up, v_ffn2_down, v_final_norm):
    weights = dict(ffn1_norm=ffn1_norm, ffn1_gate=ffn1_gate, ffn1_up=ffn1_up, ffn1_down=ffn1_down, mix_norm=mix_norm, w_in=w_in, w_pool=w_pool, pool_scale=pool_scale, lam_re=lam_re, lam_im=lam_im, log_dt=log_dt, b_re=b_re, b_im=b_im, c_re=c_re, c_im=c_im, d_skip=d_skip, w_glu=w_glu, b_glu=b_glu, pool_out_norm=pool_out_norm, ssm_out_norm=ssm_out_norm, w_out=w_out, ffn2_norm=ffn2_norm, ffn2_gate=ffn2_gate, ffn2_up=ffn2_up, ffn2_down=ffn2_down, final_norm=final_norm)
    moms = dict(ffn1_norm=(m_ffn1_norm, v_ffn1_norm), ffn1_gate=(m_ffn1_gate, v_ffn1_gate), ffn1_up=(m_ffn1_up, v_ffn1_up), ffn1_down=(m_ffn1_down, v_ffn1_down), mix_norm=(m_mix_norm, v_mix_norm), w_in=(m_w_in, v_w_in), w_pool=(m_w_pool, v_w_pool), pool_scale=(m_pool_scale, v_pool_scale), lam_re=(m_lam_re, v_lam_re), lam_im=(m_lam_im, v_lam_im), log_dt=(m_log_dt, v_log_dt), b_re=(m_b_re, v_b_re), b_im=(m_b_im, v_b_im), c_re=(m_c_re, v_c_re), c_im=(m_c_im, v_c_im), d_skip=(m_d_skip, v_d_skip), w_glu=(m_w_glu, v_w_glu), b_glu=(m_b_glu, v_b_glu), pool_out_norm=(m_pool_out_norm, v_pool_out_norm), ssm_out_norm=(m_ssm_out_norm, v_ssm_out_norm), w_out=(m_w_out, v_w_out), ffn2_norm=(m_ffn2_norm, v_ffn2_norm), ffn2_gate=(m_ffn2_gate, v_ffn2_gate), ffn2_up=(m_ffn2_up, v_ffn2_up), ffn2_down=(m_ffn2_down, v_ffn2_down), final_norm=(m_final_norm, v_final_norm))
    names = list(weights)

    n_seq, seq, D = x.shape
    N = n_seq * seq
    Fs = ffn1_gate.shape[1]
    Fps = -(-Fs // FF_ALIGN) * FF_ALIGN
    G, _, Cg = w_pool.shape
    PW = G * Cg
    SW = d_skip.shape[0]
    SG, P = lam_re.shape
    H = SSM_H
    gpc = LANES // H
    NC = SG // gpc
    W = gpc * P
    GP = SG * P
    place_idx = jnp.stack([2 * lax.axis_index("x") + lax.axis_index("y"), lax.axis_index("c")]).astype(jnp.int32)

    row = lambda v: v.reshape(1, -1)
    xf = x.reshape(N, D)
    tgt = loss_target.reshape(N, D)

    pad_r = lambda w: jnp.pad(w.astype(CDT), ((0, Fps - Fs), (0, 0)))
    pad_c = lambda w: pad_r(w.T)
    rows4 = lambda w4: w4.reshape(NCHIP * Fps, D)
    wg1, wu1 = map(rows4, gather_shards("gather_ffn1", [pad_c(ffn1_gate), pad_c(ffn1_up)]))
    mix_shards = [w_in.astype(CDT), w_out.astype(CDT), w_glu.astype(CDT), w_pool.astype(CDT)]
    f2_shards = [pad_c(ffn2_gate), pad_c(ffn2_up), pad_r(ffn2_down)]
    passed_mix = []

    def wd1_from_up(sent):
        passed = comm_call("gather_ffn1_down", [gather_pass_spec(sent)])
        passed_mix.extend(passed[1:])
        return rows4(passed[0])

    h1, saved1, _, (sent_g2,), wd1 = _ffn_fwd(
        "ffn1", xf, row(ffn1_norm), wg1, wu1, None, up_comm=[gather_send_spec([pad_r(ffn1_down)] + mix_shards)],
        down_comm=lambda sent: [gather_send_spec(f2_shards[:1])], wd_from_up=wd1_from_up)
    w_in_f, w_out_f, w_glu_f, w_pool_f = passed_mix
    w_in_f = w_in_f.reshape(D, PW + SW)
    w_out_f = w_out_f.reshape(PW + SW, D)
    w_glu_f = w_glu_f.reshape(SW, SW)
    w_pool_f = jnp.swapaxes(w_pool_f, 0, 1).reshape(G, Cg, Cg)
    n2, rstd2 = rms_fwd("mix_norm", h1, row(mix_norm))
    z, wg2, sent_u2 = mm_nn("mix_in", [n2], [w_in_f], tm=1024, tn=256,
                            comm=[gather_pass_spec([sent_g2]), gather_send_spec(f2_shards[1:2])])
    wg2 = rows4(wg2)

    d_pool, y_pool = pool_fwd("pool_fwd", z, w_pool_f, row(pool_scale), n_seq, seq)

    col = lambda v: v.reshape(GP, 1)
    lr_c, li_c = col(lam_re), col(lam_im)
    ldt_c = col(jnp.broadcast_to(log_dt[:, None], (SG, P)))
    br_c, bi_c = b_re.reshape(GP, H), b_im.reshape(GP, H)
    ar, ai, bbr, bbi = ssm_params_fwd("ssm_params", lr_c, li_c, ldt_c, br_c, bi_c)
    ar2, ai2 = ar.reshape(GP // LANES, LANES), ai.reshape(GP // LANES, LANES)
    bbr_t = jnp.swapaxes(bbr.reshape(SG, P, H), 1, 2)
    bbi_t = jnp.swapaxes(bbi.reshape(SG, P, H), 1, 2)
    b_cat = jnp.concatenate([_blockdiag(bbr_t, gpc), _blockdiag(bbi_t, gpc)], axis=2).astype(CDT)
    b_cat_t = jnp.swapaxes(b_cat, 1, 2)
    c_cat_t = jnp.concatenate([_blockdiag(c_re, gpc), _blockdiag(-c_im, gpc)], axis=2).astype(CDT)
    c_cat = jnp.swapaxes(c_cat_t, 1, 2)

    u_off = PW // LANES
    bur, bui = ssm_expand("ssm_bu", z, u_off, b_cat, SSM_ROWS)
    v3 = lambda a: a.reshape(N, GP // LANES, LANES)
    xs_r, xs_i = ssm_scan_fwd("ssm_scan", v3(bur), v3(bui), ar2, ai2, n_seq, seq, 64)
    xs_r, xs_i = xs_r.reshape(N, GP), xs_i.reshape(N, GP)

    def epi_y(acc, extras):
        y = acc + extras[1] * extras[0]
        return y, _gelu(y)

    y_pre, yg = ssm_contract("ssm_y", xs_r, xs_i, c_cat, [(z, "mn", u_off), (row(d_skip), "n", 0)], epi_y,
                             (F32, CDT), SSM_ROWS)

    def epi_glu(accs, extras):
        q = accs[0] + extras[1]
        return q, _gelu(extras[0]) * jax.nn.sigmoid(q)

    q_glu, y_ssm, wu2 = mm_nn("glu", [yg], [w_glu_f], tm=1024, tn=256,
                              extras=[(y_pre, "mn", 0), (row(b_glu), "n", 0)], out_dtypes=(F32, F32), epi=epi_glu,
                              comm=[gather_pass_spec([sent_u2])])
    wu2 = rows4(wu2)
    merged, rstd_p, rstd_s = merge_norm_fwd("merge_norm", y_pool, y_ssm, row(pool_out_norm), row(ssm_out_norm))

    def epi_res(accs, extras):
        return (extras[0] + accs[0],)

    (h2,) = mm_nn("mix_out", [merged], [w_out_f], tm=1024, tn=256, extras=[(h1, "mn", 0)], epi=epi_res)
    h3, saved2, _, _, wd2 = _ffn_fwd(
        "ffn2", h2, row(ffn2_norm), wg2, wu2, None, up_comm=[gather_send_spec(f2_shards[2:])],
        wd_from_up=lambda sent: rows4(comm_call("gather_ffn2_down", [gather_pass_spec(sent)])[0]))

    half = place_idx[1:]
    loss_acc, dh3, dfb3, g_final = loss_head("loss_head", h3, row(final_norm), tgt)
    dg2, du2 = _ffn_bact("ffn2", saved2, wd2, dfb3)
    (dwd2,) = _ffn_dwd("ffn2", saved2, dfb3)
    dwg2, dwu2 = _ffn_dwgu("ffn2", saved2, dg2, du2)
    parts_f2 = [p.reshape(NCHIP, Fps, D) for p in (dwg2, dwu2, dwd2)]
    dn_f2, *recv_f2 = _ffn_dn("ffn2", dg2, du2, wg2, wu2, comm=[rs_xh_spec(parts_f2)])
    dh2, dh2c, g_ffn2n = rms_bwd("ffn2_bnorm", dn_f2, saved2[0], saved2[2], row(ffn2_norm), dh3, 1.0)
    sums_f2 = [add_halves(f"rs_add_f2_{k}", p, r, half) for k, (p, r) in enumerate(zip(parts_f2, recv_f2))]

    (dmerged,) = mm_nt("mix_out_bx", [dh2c], [w_out_f], tm=1024, tn=256)
    (dw_out,) = mm_tn("mix_out_bw", [merged], [dh2c], ti=512, tj=512)
    dy_pool, dy_ssm, g_pon, g_son = merge_norm_bwd("merge_norm_b", dmerged, y_pool, y_ssm, rstd_p, rstd_s,
                                                   row(pool_out_norm), row(ssm_out_norm))
    dq, dyg1, g_bglu = glu_bwd_pre("glu_b_pre", dy_ssm, y_pre, q_glu)

    def epi_dyg(accs, extras):
        return ((accs[0] + extras[0]) * _gelu_grad(extras[1]),)

    (dy_pre,) = mm_nt("glu_bx", [dq], [w_glu_f], tm=1024, tn=256, extras=[(dyg1, "mn", 0), (y_pre, "mn", 0)],
                      epi=epi_dyg)
    (dw_glu,) = mm_tn("glu_bw", [yg], [dq], ti=512, tj=512)
    g_dskip = colsum_prod("dskip", dy_pre, z, PW // SW, SW)
    gxr, gxi = ssm_expand("ssm_by", dy_pre, 0, c_cat_t, SSM_ROWS)
    dc_r, dc_i = ssm_outer("ssm_dc", dy_pre, 0, xs_r, xs_i, NC, W, SSM_ROWS)
    lam_r, lam_i, dar, dai = ssm_scan_bwd("ssm_scan_b", v3(gxr), v3(gxi), v3(xs_r), v3(xs_i), ar2, ai2,
                                          n_seq, seq, 64)
    lam_r, lam_i = lam_r.reshape(N, GP), lam_i.reshape(N, GP)
    db_r, db_i = ssm_outer("ssm_db", z, u_off, lam_r, lam_i, NC, W, SSM_ROWS)

    def epi_du(acc, extras):
        return (acc + extras[0] * extras[1],)

    (du_ssm,) = ssm_contract("ssm_bu_b", lam_r, lam_i, b_cat_t, [(dy_pre, "mn", 0), (row(d_skip), "n", 0)],
                             epi_du, (CDT,), SSM_ROWS)
    g_c_re = _diag_blocks(dc_r, gpc, H, P)
    g_c_im = -_diag_blocks(dc_i, gpc, H, P)
    dbbr = jnp.swapaxes(_diag_blocks(db_r, gpc, H, P), 1, 2).reshape(GP, H)
    dbbi = jnp.swapaxes(_diag_blocks(db_i, gpc, H, P), 1, 2).reshape(GP, H)
    dlr, dli, dldt, dbr, dbi = ssm_params_bwd("ssm_params_b", lr_c, li_c, ldt_c, br_c, bi_c,
                                              dar.reshape(GP, 1), dai.reshape(GP, 1), dbbr, dbbi)
    g_lam_re, g_lam_im = dlr.reshape(SG, P), dli.reshape(SG, P)
    g_log_dt = jnp.sum(dldt.reshape(SG, P), axis=1)
    g_b_re, g_b_im = dbr.reshape(SG, P, H), dbi.reshape(SG, P, H)
    dz_pool, dw_pool, g_pscale = pool_bwd("pool_bwd", dy_pool, d_pool, w_pool_f, row(pool_scale), n_seq, seq)
    dz = jnp.concatenate([dz_pool, du_ssm], axis=1)
    (dn2,) = mm_nt("mix_in_bx", [dz], [w_in_f], tm=1024, tn=256)
    (dw_in,) = mm_tn("mix_in_bw", [n2], [dz], ti=512, tj=512)
    dh1, dh1c, g_mixn = rms_bwd("mix_norm_b", dn2, h1, rstd2, row(mix_norm), dh2, 0.5)
    dwp4 = jnp.swapaxes(dw_pool.astype(CDT).reshape(G, NCHIP, Cg // NCHIP, Cg), 0, 1)
    parts_mx = [_rs3(p) for p in (dw_in.reshape(NCHIP, D // NCHIP, PW + SW), dwp4,
                                  dw_glu.reshape(NCHIP, SW // NCHIP, SW),
                                  dw_out.reshape(NCHIP, (PW + SW) // NCHIP, D))]
    dwd1, slab_g2 = _ffn_dwd("ffn1", saved1, dh1c, comm=[rs_xc_spec(sums_f2[:1])])
    part_d1 = dwd1.reshape(NCHIP, Fps, D)
    dg1, du1, slab_u2, *recv_b = _ffn_bact("ffn1", saved1, wd1, dh1c,
                                           comm=[rs_xc_spec(sums_f2[1:2]), rs_xh_spec(parts_mx + [part_d1])])
    sums_mx = [add_halves(f"rs_add_mx_{k}", p, r, half) for k, (p, r) in enumerate(zip(parts_mx, recv_b[:4]))]
    sum_d1 = add_halves("rs_add_f1_2", part_d1, recv_b[4], half)
    dwg1, dwu1, slab_d2, slab_d1 = _ffn_dwgu("ffn1", saved1, dg1, du1, comm=[rs_xc_spec([sums_f2[2], sum_d1])])
    parts_gu1 = [dwg1.reshape(NCHIP, Fps, D), dwu1.reshape(NCHIP, Fps, D)]
    recv_gu1 = comm_call("rs_xh_f1", [rs_xh_spec(parts_gu1)])
    sums_gu1 = [add_halves(f"rs_add_f1_{k}", p, r, half) for k, (p, r) in enumerate(zip(parts_gu1, recv_gu1))]
    dn_f1, *slabs_c = _ffn_dn("ffn1", dg1, du1, wg1, wu1, comm=[rs_xc_spec(sums_gu1 + sums_mx)])
    grad_x, _, g_ffn1n = rms_bwd("ffn1_bnorm", dn_f1, saved1[0], saved1[2], row(ffn1_norm), dh1, 1.0)

    order = ["ffn1_gate", "ffn1_up", "ffn1_down", "w_in", "w_pool", "w_glu", "w_out", "ffn2_gate", "ffn2_up",
             "ffn2_down"]
    own_sums = sums_gu1 + [sum_d1] + sums_mx + sums_f2
    slabs = slabs_c[:2] + [slab_d1] + slabs_c[2:] + [slab_g2, slab_u2, slab_d2]
    fulls = [sum_chips("rs_sum_" + k, t, u, place_idx) for k, t, u in zip(order, own_sums, slabs)]
    joined = comm_call("rs_join", [rs_jh_spec(fulls)])
    big = dict(zip(order, joined))
    transposed = ("ffn1_gate", "ffn1_up", "ffn2_gate", "ffn2_up")
    grads = dict(ffn1_gate=big["ffn1_gate"][:Fs], ffn1_up=big["ffn1_up"][:Fs], ffn1_down=big["ffn1_down"][:Fs],
                 w_in=big["w_in"], w_pool=big["w_pool"].reshape(w_pool.shape), w_glu=big["w_glu"], w_out=big["w_out"],
                 ffn2_gate=big["ffn2_gate"][:Fs], ffn2_up=big["ffn2_up"][:Fs], ffn2_down=big["ffn2_down"][:Fs])

    small = dict(ffn1_norm=g_ffn1n, mix_norm=g_mixn, pool_scale=g_pscale, lam_re=g_lam_re, lam_im=g_lam_im,
                 log_dt=g_log_dt, b_re=g_b_re, b_im=g_b_im, c_re=g_c_re, c_im=g_c_im, d_skip=g_dskip,
                 b_glu=g_bglu, pool_out_norm=g_pon, ssm_out_norm=g_son, ffn2_norm=g_ffn2n, final_norm=g_final)
    pieces = [jnp.pad(small[k].reshape(-1), (0, (-small[k].size) % LANES)) for k in small]
    pieces.append(loss_acc.reshape(-1))
    flat = jnp.concatenate(pieces)
    rows = -(-flat.size // (16 * LANES)) * 16
    flat = jnp.pad(flat, (0, rows * LANES - flat.size)).reshape(rows, LANES)
    red = all_reduce_small("all_reduce_small", flat).reshape(-1)
    off = 0
    for k in small:
        size = small[k].size
        grads[k] = red[off:off + size].reshape(weights[k].shape)
        off += size + (-size) % LANES
    loss = (0.5 / D) * red[off]

    deltas, new_m, new_v = {}, {}, {}
    for k in names:
        w = weights[k]
        m, v = moms[k]
        if k in transposed:
            d_, m_, v_ = adamw("adamw_" + k, w.T, grads[k], m.T, v.T)
            deltas[k], new_m[k], new_v[k], grads[k] = d_.T, m_.T, v_.T, grads[k].T
            continue
        if w.ndim >= 2 and w.shape[-1] >= LANES:
            shape2 = (w.size // w.shape[-1], w.shape[-1])
        elif w.size % LANES == 0:
            shape2 = (w.size // LANES, LANES)
        else:
            shape2 = (1, w.size)
        m, v = moms[k]
        d_, m_, v_ = adamw("adamw_" + k, w.reshape(shape2), grads[k].reshape(shape2), m.reshape(shape2),
                           v.reshape(shape2))
        deltas[k], new_m[k], new_v[k] = d_.reshape(w.shape), m_.reshape(w.shape), v_.reshape(w.shape)

    return (loss, grad_x.reshape(x.shape), *[grads[k] for k in names], *[deltas[k] for k in names],
            *[new_m[k] for k in names], *[new_v[k] for k in names])
```
